```python
import jax, jax.numpy as jnp
from jax import lax
import numpy as np

D_MODEL = 2048
BATCH = 2
SEQ = 8192
DEPTH = 2

RET_HEADS = 4
RET_HEAD_DIM = D_MODEL // 2 // RET_HEADS
RET_WIDTH = RET_HEADS * RET_HEAD_DIM
CONV_WIDTH = D_MODEL - RET_WIDTH
CONV_K = 3
RET_CHUNK = 128
EVEN_IN = 4 * RET_WIDTH + 4 * CONV_WIDTH
ATTN_HEAD_DIM = 64
ATTN_Q_HEADS = D_MODEL // ATTN_HEAD_DIM
ATTN_KV_HEADS = ATTN_Q_HEADS // 8
ATTN_WIDTH = ATTN_Q_HEADS * ATTN_HEAD_DIM
KV_WIDTH = ATTN_KV_HEADS * ATTN_HEAD_DIM
ODD_IN = 2 * ATTN_WIDTH + 2 * KV_WIDTH
WINDOW = 128
BLOCK = 128
ROPE_THETA = 10000.0
EPS = 1e-6
N_EVEN = (DEPTH + 1) // 2
N_ODD = DEPTH // 2

kernel_name = "hybrid_retention_shortconv_swa_sinks"


def rms_norm(x, w):
    xf = x.astype(jnp.float32)
    y = xf * lax.rsqrt(jnp.mean(xf * xf, axis=-1, keepdims=True) + EPS)
    return (y * w.astype(jnp.float32)).astype(x.dtype)


def rms_norm_nogain(x):
    xf = x.astype(jnp.float32)
    return (xf * lax.rsqrt(jnp.mean(xf * xf, axis=-1, keepdims=True) + EPS)).astype(x.dtype)


def rope(x, pos):
    d = x.shape[-1]
    inv = 1.0 / (ROPE_THETA ** (jnp.arange(0, d, 2, dtype=jnp.float32) / d))
    ang = pos.astype(jnp.float32)[:, None] * inv[None, :]
    cos = jnp.cos(ang)[None, :, None, :]
    sin = jnp.sin(ang)[None, :, None, :]
    xf = x.astype(jnp.float32)
    x1, x2 = xf[..., : d // 2], xf[..., d // 2:]
    return jnp.concatenate([x1 * cos - x2 * sin, x2 * cos + x1 * sin], axis=-1).astype(x.dtype)


def retention_chunkwise(q, k, v):
    b, s, h, d = q.shape
    c = RET_CHUNK
    n = s // c
    dt = q.dtype
    log_g = jnp.log(1.0 - 2.0 ** (-5.0 - jnp.arange(h, dtype=jnp.float32)))
    idx = jnp.arange(c, dtype=jnp.float32)
    diff = idx[:, None] - idx[None, :]
    intra = jnp.where(diff >= 0, jnp.exp(log_g[:, None, None] * jnp.maximum(diff, 0.0)), 0.0).astype(dt)
    q_dec = jnp.exp(log_g[:, None] * (idx[None, :] + 1.0)).astype(dt)
    k_dec = jnp.exp(log_g[:, None] * (c - 1.0 - idx[None, :])).astype(dt)
    chunk_dec = jnp.exp(log_g * c).astype(dt)
    qc = q.reshape(b, n, c, h, d)
    kc = (k * (d ** -0.5)).reshape(b, n, c, h, d)
    vc = v.reshape(b, n, c, h, d)
    scores = jnp.einsum('bnihd,bnjhd->bnhij', qc, kc) * intra[None, None]
    inner = jnp.einsum('bnhij,bnjhe->bnihe', scores, vc)
    kv = jnp.einsum('bnjhd,bnjhe,hj->bnhde', kc, vc, k_dec)

    def step(state, kv_n):
        return chunk_dec[None, :, None, None] * state + kv_n, state

    _, prev = lax.scan(step, jnp.zeros((b, h, d, d), dt), jnp.moveaxis(kv, 1, 0))
    prev = jnp.moveaxis(prev, 0, 1)
    cross = jnp.einsum('bnihd,bnhde,hi->bnihe', qc, prev, q_dec)
    return (inner + cross).reshape(b, s, h, d)


def even_mixer(h, w_in, conv_w, w_out, pos):
    b, s, _ = h.shape
    proj = h @ w_in
    q, k, v, g_ret, gate_b, gate_c, u, g_conv = jnp.split(proj, 8, axis=-1)
    shp = (b, s, RET_HEADS, RET_HEAD_DIM)
    q = rope(q.reshape(shp), pos)
    k = rope(k.reshape(shp), pos)
    o = retention_chunkwise(q, k, v.reshape(shp))
    ret_out = rms_norm_nogain(o).reshape(b, s, RET_WIDTH) * jax.nn.silu(g_ret)
    conv = lax.conv_general_dilated(
        gate_c * u, conv_w[:, None, :].astype(u.dtype), window_strides=(1,),
        padding=[(CONV_K - 1, 0)], dimension_numbers=('NWC', 'WIO', 'NWC'),
        feature_group_count=CONV_WIDTH)
    conv_out = gate_b * conv * jax.nn.silu(g_conv)
    return jnp.concatenate([ret_out, conv_out], axis=-1) @ w_out


def swa_sinks(q, k, v, sinks):
    b, s, hq, d = q.shape
    hk = k.shape[2]
    g = hq // hk
    n = s // BLOCK
    qb = q.reshape(b, n, BLOCK, hk, g, d)
    kb = k.reshape(b, n, BLOCK, hk, d)
    vb = v.reshape(b, n, BLOCK, hk, d)
    pad = ((0, 0), (1, 0), (0, 0), (0, 0), (0, 0))
    kk = jnp.concatenate([jnp.pad(kb, pad)[:, :-1], kb], axis=2)
    vv = jnp.concatenate([jnp.pad(vb, pad)[:, :-1], vb], axis=2)
    scores = jnp.einsum('bnihgd,bnjhd->bnhgij', qb, kk).astype(jnp.float32) * (d ** -0.5)
    qi = jnp.arange(BLOCK)[:, None] + BLOCK
    kj = jnp.arange(2 * BLOCK)[None, :]
    band = (kj <= qi) & (qi - kj < WINDOW)
    valid = band[None] & ((jnp.arange(n)[:, None, None] > 0) | (kj >= BLOCK)[None])
    scores = jnp.where(valid[None, :, None, None], scores, -1e30)
    sink = sinks.astype(jnp.float32).reshape(hk, g)[None, None, :, :, None, None]
    m = jnp.maximum(jnp.max(scores, axis=-1, keepdims=True), sink)
    p = jnp.exp(scores - m)
    p = p / (jnp.sum(p, axis=-1, keepdims=True) + jnp.exp(sink - m))
    o = jnp.einsum('bnhgij,bnjhd->bnihgd', p.astype(v.dtype), vv)
    return o.reshape(b, s, hq, d)


def odd_mixer(h, w_in, q_norm_w, k_norm_w, sinks, w_out, pos):
    b, s, _ = h.shape
    proj = h @ w_in
    q, k, v, gate = jnp.split(proj, [ATTN_WIDTH, ATTN_WIDTH + KV_WIDTH, ATTN_WIDTH + 2 * KV_WIDTH], axis=-1)
    q = rope(rms_norm(q.reshape(b, s, ATTN_Q_HEADS, ATTN_HEAD_DIM), q_norm_w), pos)
    k = rope(rms_norm(k.reshape(b, s, ATTN_KV_HEADS, ATTN_HEAD_DIM), k_norm_w), pos)
    v = v.reshape(b, s, ATTN_KV_HEADS, ATTN_HEAD_DIM)
    o = swa_sinks(q, k, v, sinks).reshape(b, s, ATTN_WIDTH)
    return (o * jax.nn.silu(gate)) @ w_out


def setup_inputs(seed: int = 0) -> dict:
    key = jax.random.key(seed)
    ks = jax.random.split(key, 11)
    f32 = jnp.float32
    x = jax.random.normal(ks[0], (BATCH, SEQ, D_MODEL), f32)
    ev_norm_w = 1.0 + 0.02 * jax.random.normal(ks[1], (N_EVEN, D_MODEL), f32)
    ev_w_in = jax.random.normal(ks[2], (N_EVEN, D_MODEL, EVEN_IN), f32) * D_MODEL ** -0.5
    ev_conv_w = jax.random.normal(ks[3], (N_EVEN, CONV_K, CONV_WIDTH), f32) * CONV_K ** -0.5
    ev_w_out = jax.random.normal(ks[4], (N_EVEN, D_MODEL, D_MODEL), f32) * D_MODEL ** -0.5
    od_norm_w = 1.0 + 0.02 * jax.random.normal(ks[5], (N_ODD, D_MODEL), f32)
    od_w_in = jax.random.normal(ks[6], (N_ODD, D_MODEL, ODD_IN), f32) * D_MODEL ** -0.5
    od_q_norm_w = 1.0 + 0.02 * jax.random.normal(ks[7], (N_ODD, ATTN_HEAD_DIM), f32)
    od_k_norm_w = 1.0 + 0.02 * jax.random.normal(ks[8], (N_ODD, ATTN_HEAD_DIM), f32)
    od_sinks = 0.5 * jax.random.normal(ks[9], (N_ODD, ATTN_Q_HEADS), f32)
    od_w_out = jax.random.normal(ks[10], (N_ODD, ATTN_WIDTH, D_MODEL), f32) * ATTN_WIDTH ** -0.5
    return {"x": x, "ev_norm_w": ev_norm_w, "ev_w_in": ev_w_in, "ev_conv_w": ev_conv_w,
            "ev_w_out": ev_w_out, "od_norm_w": od_norm_w, "od_w_in": od_w_in,
            "od_q_norm_w": od_q_norm_w, "od_k_norm_w": od_k_norm_w, "od_sinks": od_sinks,
            "od_w_out": od_w_out}


def reference(x, ev_norm_w, ev_w_in, ev_conv_w, ev_w_out, od_norm_w, od_w_in,
              od_q_norm_w, od_k_norm_w, od_sinks, od_w_out):
    pos = jnp.arange(x.shape[1])
    for layer in range(DEPTH):
        i = layer // 2
        if layer % 2 == 0:
            h = rms_norm(x, ev_norm_w[i])
            x = x + even_mixer(h, ev_w_in[i], ev_conv_w[i], ev_w_out[i], pos)
        else:
            h = rms_norm(x, od_norm_w[i])
            x = x + odd_mixer(h, od_w_in[i], od_q_norm_w[i], od_k_norm_w[i], od_sinks[i], od_w_out[i], pos)
    return x
```

```python
import functools

import numpy as np
import jax
import jax.numpy as jnp
from jax import lax
from jax.experimental import pallas as pl
from jax.experimental.pallas import tpu as pltpu

D_MODEL = 2048
RET_HEADS = 4
RET_HEAD_DIM = 256
RET_WIDTH = RET_HEADS * RET_HEAD_DIM
CONV_WIDTH = D_MODEL - RET_WIDTH
CONV_K = 3
RET_CHUNK = 128
EVEN_IN = 4 * RET_WIDTH + 4 * CONV_WIDTH
ATTN_HEAD_DIM = 64
ATTN_Q_HEADS = 32
ATTN_KV_HEADS = 4
ATTN_GROUP = ATTN_Q_HEADS // ATTN_KV_HEADS
ATTN_WIDTH = ATTN_Q_HEADS * ATTN_HEAD_DIM
KV_WIDTH = ATTN_KV_HEADS * ATTN_HEAD_DIM
BLOCK = 128
ROPE_THETA = 10000.0
EPS = 1e-6
NEG = -1e30

LANES = 128
MXU_N = 256
IN_TM = 512
IN_TN = 1024
OUT_TM = 512
CONV_TILE = 256
HALO = 8
VMEM_LIMIT = 56 * 1024 * 1024

F32 = jnp.float32
BF16 = jnp.bfloat16


def _dot(a, b):
    return jnp.dot(a, b, preferred_element_type=F32)


def _dot_nt(a, b):
    return lax.dot_general(a, b, (((1,), (1,)), ((), ())), preferred_element_type=F32)


def _dot_tn(a, b):
    return lax.dot_general(a, b, (((0,), (0,)), ((), ())), preferred_element_type=F32)


def _silu(g):
    return g / (1.0 + jnp.exp(-g))


def _rms_rows(x, nw):
    ms = jnp.mean(x * x, axis=-1, keepdims=True)
    return x * lax.rsqrt(ms + EPS) * nw


def _even_in_kernel(x_ref, nw_ref, w_ref, cos_ref, sin_ref, cw_ref,
                    qkvg_ref, conv_ref, h_scr, cu_scr, *, seq_tiles):
    i = pl.program_id(0)
    j = pl.program_id(1)
    tm = x_ref.shape[0]
    nsub = IN_TN // MXU_N

    @pl.when(j == 0)
    def _():
        h_scr[...] = _rms_rows(x_ref[...], nw_ref[...]).astype(BF16)

    def sub_dot(s):
        return _dot(h_scr[...], w_ref[:, s * MXU_N:(s + 1) * MXU_N])

    @pl.when(j < 2)
    def _():
        cos = cos_ref[...]
        sin = sin_ref[...]
        for s in range(nsub):
            acc = sub_dot(s)
            x1 = acc[:, :LANES]
            x2 = acc[:, LANES:]
            qkvg_ref[:, s * MXU_N:s * MXU_N + LANES] = (x1 * cos - x2 * sin).astype(BF16)
            qkvg_ref[:, s * MXU_N + LANES:(s + 1) * MXU_N] = (x2 * cos + x1 * sin).astype(BF16)

    @pl.when(j == 2)
    def _():
        for s in range(nsub):
            qkvg_ref[:, s * MXU_N:(s + 1) * MXU_N] = sub_dot(s).astype(BF16)

    @pl.when(j == 3)
    def _():
        for s in range(nsub):
            qkvg_ref[:, s * MXU_N:(s + 1) * MXU_N] = _silu(sub_dot(s)).astype(BF16)

    @pl.when(j >= 4)
    def _():
        cs = j - 4
        gate_b = sub_dot(0)
        cu = sub_dot(1) * sub_dot(2)
        at_seq_start = (i % seq_tiles) == 0

        @pl.when(at_seq_start)
        def _():
            cu_scr[cs, HALO - (CONV_K - 1):HALO, :] = jnp.zeros((CONV_K - 1, CONV_TILE), F32)

        @pl.when(jnp.logical_not(at_seq_start))
        def _():
            cu_scr[cs, HALO - (CONV_K - 1):HALO, :] = cu_scr[cs, HALO + tm - (CONV_K - 1):HALO + tm, :]

        cu_scr[cs, HALO:HALO + tm, :] = cu
        cw = cw_ref[0]
        conv = cw[CONV_K - 1:CONV_K, :] * cu
        for t in range(CONV_K - 1):
            shift = CONV_K - 1 - t
            conv = conv + cw[t:t + 1, :] * cu_scr[cs, HALO - shift:HALO - shift + tm, :]
        conv_ref[...] = (gate_b * conv * _silu(sub_dot(3))).astype(BF16)


def _even_in(x2d, nw, w, cos, sin, cw, seq):
    t = x2d.shape[0]
    tm = IN_TM
    seq_tiles = seq // tm
    nj = EVEN_IN // IN_TN
    n_conv_tiles = CONV_WIDTH // CONV_TILE
    kern = functools.partial(_even_in_kernel, seq_tiles=seq_tiles)
    return pl.pallas_call(
        kern,
        grid=(t // tm, nj),
        in_specs=[
            pl.BlockSpec((tm, D_MODEL), lambda i, j: (i, 0)),
            pl.BlockSpec((1, D_MODEL), lambda i, j: (0, 0)),
            pl.BlockSpec((D_MODEL, IN_TN), lambda i, j: (0, j)),
            pl.BlockSpec((tm, LANES), lambda i, j: (i % seq_tiles, 0)),
            pl.BlockSpec((tm, LANES), lambda i, j: (i % seq_tiles, 0)),
            pl.BlockSpec((1, 8, CONV_TILE), lambda i, j: (jnp.maximum(j - 4, 0), 0, 0)),
        ],
        out_specs=[
            pl.BlockSpec((tm, IN_TN), lambda i, j: (i, jnp.minimum(j, 3))),
            pl.BlockSpec((tm, CONV_TILE), lambda i, j: (i, jnp.maximum(j - 4, 0))),
        ],
        out_shape=[
            jax.ShapeDtypeStruct((t, 4 * RET_WIDTH), BF16),
            jax.ShapeDtypeStruct((t, CONV_WIDTH), BF16),
        ],
        scratch_shapes=[
            pltpu.VMEM((tm, D_MODEL), BF16),
            pltpu.VMEM((n_conv_tiles, HALO + tm, CONV_TILE), F32),
        ],
        compiler_params=pltpu.CompilerParams(
            dimension_semantics=("arbitrary", "arbitrary"),
            vmem_limit_bytes=VMEM_LIMIT),
        name="even_in",
    )(x2d, nw, w, cos, sin, cw)


def _even_out_kernel(q_ref, k_ref, v_ref, g_ref, conv_ref, x_ref, w_ref,
                     dmat_ref, qdec_ref, kdec_ref, cdec_ref,
                     o_ref, state_scr, y_scr):
    tm = x_ref.shape[0]

    @pl.when(pl.program_id(1) == 0)
    def _():
        state_scr[...] = jnp.zeros_like(state_scr)

    for c in range(tm // RET_CHUNK):
        rows = slice(c * RET_CHUNK, (c + 1) * RET_CHUNK)
        for h in range(RET_HEADS):
            cols = slice(h * RET_HEAD_DIM, (h + 1) * RET_HEAD_DIM)
            qh = q_ref[rows, cols]
            kh = k_ref[rows, cols]
            vh = v_ref[rows, cols]
            scores = _dot_nt(qh, kh) * dmat_ref[h]
            inner = _dot(scores.astype(BF16), vh)
            state = state_scr[h]
            cross = _dot(qh, state.astype(BF16)) * qdec_ref[h]
            o = inner + cross
            kd = (kh.astype(F32) * kdec_ref[h]).astype(BF16)
            state_scr[h] = cdec_ref[h] * state + _dot_tn(kd, vh)
            ms = jnp.mean(o * o, axis=-1, keepdims=True)
            ro = o * lax.rsqrt(ms + EPS) * g_ref[rows, cols].astype(F32)
            y_scr[rows, cols] = ro.astype(BF16)
    y_scr[:, RET_WIDTH:] = conv_ref[...]

    nb = 512
    for n in range(D_MODEL // nb):
        cs = slice(n * nb, (n + 1) * nb)
        o_ref[:, cs] = x_ref[:, cs] + _dot(y_scr[...], w_ref[:, cs])


def _even_out(qkvg, conv, x2d, w_out, dmat, qdec, kdec, cdec, batch, seq):
    t = x2d.shape[0]
    tm = OUT_TM
    ns = seq // tm

    def row(b, i):
        return b * ns + i

    def qkvg_spec(col):
        return pl.BlockSpec((tm, RET_WIDTH), lambda b, i: (row(b, i), col))

    def const_spec(shape):
        return pl.BlockSpec(shape, lambda b, i: (0,) * len(shape))

    return pl.pallas_call(
        _even_out_kernel,
        grid=(batch, ns),
        in_specs=[
            qkvg_spec(0), qkvg_spec(1), qkvg_spec(2), qkvg_spec(3),
            pl.BlockSpec((tm, CONV_WIDTH), lambda b, i: (row(b, i), 0)),
            pl.BlockSpec((tm, D_MODEL), lambda b, i: (row(b, i), 0)),
            const_spec((D_MODEL, D_MODEL)),
            const_spec(dmat.shape), const_spec(qdec.shape),
            const_spec(kdec.shape), const_spec(cdec.shape),
        ],
        out_specs=pl.BlockSpec((tm, D_MODEL), lambda b, i: (row(b, i), 0)),
        out_shape=jax.ShapeDtypeStruct((t, D_MODEL), F32),
        scratch_shapes=[
            pltpu.VMEM((RET_HEADS, RET_HEAD_DIM, RET_HEAD_DIM), F32),
            pltpu.VMEM((tm, D_MODEL), BF16),
        ],
        compiler_params=pltpu.CompilerParams(
            dimension_semantics=("arbitrary", "arbitrary"),
            vmem_limit_bytes=VMEM_LIMIT),
        name="even_out",
    )(qkvg, qkvg, qkvg, qkvg, conv, x2d, w_out, dmat, qdec, kdec, cdec)


def _odd_in_kernel(x_ref, nw_ref, w_ref, qa_ref, qb_ref, ka_ref, kb_ref, ones_ref,
                   q_ref, kv_ref, g_ref, h_scr):
    j = pl.program_id(1)
    nsub = IN_TN // MXU_N

    @pl.when(j == 0)
    def _():
        h_scr[...] = _rms_rows(x_ref[...], nw_ref[...]).astype(BF16)

    def sub_dot(s):
        return _dot(h_scr[...], w_ref[:, s * MXU_N:(s + 1) * MXU_N])

    def norm_rope(acc, ta, tb):
        ss = _dot((acc * acc).astype(BF16), ones_ref[...])
        r = lax.rsqrt(ss * (1.0 / ATTN_HEAD_DIM) + EPS)
        half = ATTN_HEAD_DIM // 2
        lane = lax.broadcasted_iota(jnp.int32, acc.shape, 1)
        first_half = (lane % ATTN_HEAD_DIM) < half
        partner = jnp.where(first_half,
                            pltpu.roll(acc, MXU_N - half, axis=1),
                            pltpu.roll(acc, half, axis=1))
        ta2 = jnp.concatenate([ta, ta], axis=1)
        tb2 = jnp.concatenate([tb, tb], axis=1)
        return (r * (acc * ta2 + partner * tb2)).astype(BF16)

    @pl.when(j < 2)
    def _():
        ta = qa_ref[...]
        tb = qb_ref[...]
        for s in range(nsub):
            q_ref[:, s * MXU_N:(s + 1) * MXU_N] = norm_rope(sub_dot(s), ta, tb)

    @pl.when(j == 2)
    def _():
        ta = ka_ref[...]
        tb = kb_ref[...]
        for s in range(nsub // 2):
            kv_ref[:, s * MXU_N:(s + 1) * MXU_N] = norm_rope(sub_dot(s), ta, tb)
        for s in range(nsub // 2, nsub):
            kv_ref[:, s * MXU_N:(s + 1) * MXU_N] = sub_dot(s).astype(BF16)

    @pl.when(j >= 3)
    def _():
        for s in range(nsub):
            g_ref[:, s * MXU_N:(s + 1) * MXU_N] = _silu(sub_dot(s)).astype(BF16)


def _odd_in(x2d, nw, w, qa, qb, ka, kb, ones_bd, seq):
    t = x2d.shape[0]
    tm = IN_TM
    seq_tiles = seq // tm
    nj = w.shape[1] // IN_TN

    def tab_spec():
        return pl.BlockSpec((tm, LANES), lambda i, j: (i % seq_tiles, 0))

    return pl.pallas_call(
        _odd_in_kernel,
        grid=(t // tm, nj),
        in_specs=[
            pl.BlockSpec((tm, D_MODEL), lambda i, j: (i, 0)),
            pl.BlockSpec((1, D_MODEL), lambda i, j: (0, 0)),
            pl.BlockSpec((D_MODEL, IN_TN), lambda i, j: (0, j)),
            tab_spec(), tab_spec(), tab_spec(), tab_spec(),
            pl.BlockSpec((MXU_N, MXU_N), lambda i, j: (0, 0)),
        ],
        out_specs=[
            pl.BlockSpec((tm, IN_TN), lambda i, j: (i, jnp.minimum(j, 1))),
            pl.BlockSpec((tm, IN_TN), lambda i, j: (i, 0)),
            pl.BlockSpec((tm, IN_TN), lambda i, j: (i, jnp.clip(j - 3, 0, 1))),
        ],
        out_shape=[
            jax.ShapeDtypeStruct((t, ATTN_WIDTH), BF16),
            jax.ShapeDtypeStruct((t, 4 * KV_WIDTH), BF16),
            jax.ShapeDtypeStruct((t, ATTN_WIDTH), BF16),
        ],
        scratch_shapes=[pltpu.VMEM((tm, D_MODEL), BF16)],
        compiler_params=pltpu.CompilerParams(
            dimension_semantics=("arbitrary", "arbitrary"),
            vmem_limit_bytes=VMEM_LIMIT),
        name="odd_in",
    )(x2d, nw, w, qa, qb, ka, kb, ones_bd)


def _odd_out_kernel(sink_ref, q_ref, kv_ref, kvp_ref, g_ref, x_ref, w_ref,
                    o_ref, y_scr):
    tm = x_ref.shape[0]
    at_seq_start = pl.program_id(1) == 0
    pairs = ATTN_GROUP // 2
    half_w = 2 * KV_WIDTH

    lane_kv = lax.broadcasted_iota(jnp.int32, (2 * BLOCK, LANES), 1)
    low_kv = lane_kv < ATTN_HEAD_DIM
    rr = lax.broadcasted_iota(jnp.int32, (BLOCK, BLOCK), 0)
    cc = lax.broadcasted_iota(jnp.int32, (BLOCK, BLOCK), 1)
    tri = cc <= rr
    low_o = cc < ATTN_HEAD_DIM

    for bl in range(tm // BLOCK):
        rows = slice(bl * BLOCK, (bl + 1) * BLOCK)
        for c in range(ATTN_KV_HEADS):
            kcols = slice(c * LANES, (c + 1) * LANES)
            vcols = slice(half_w + c * LANES, half_w + (c + 1) * LANES)
            if bl == 0:
                k_prev = kvp_ref[:, kcols]
                v_prev = kvp_ref[:, vcols]
            else:
                prows = slice((bl - 1) * BLOCK, bl * BLOCK)
                k_prev = kv_ref[prows, kcols]
                v_prev = kv_ref[prows, vcols]
            kk = jnp.concatenate([k_prev, kv_ref[rows, kcols]], axis=0)
            vv = jnp.concatenate([v_prev, kv_ref[rows, vcols]], axis=0)
            zero = jnp.zeros_like(kk)
            k_par = (jnp.where(low_kv, kk, zero), jnp.where(low_kv, zero, kk))
            qs = jnp.concatenate(
                [q_ref[rows, (pairs * c + a) * LANES:(pairs * c + a + 1) * LANES]
                 for a in range(pairs)], axis=0)
            probs = []
            denoms = []
            for par in range(2):
                s = _dot_nt(qs, k_par[par])
                for a in range(pairs):
                    sa = s[a * BLOCK:(a + 1) * BLOCK]
                    s_prev = sa[:, :BLOCK]
                    if bl == 0:
                        s_prev = jnp.where(at_seq_start, NEG, s_prev)
                    f = jnp.where(tri, sa[:, BLOCK:], s_prev)
                    sink = sink_ref[ATTN_GROUP * c + 2 * a + par]
                    m = jnp.maximum(jnp.max(f, axis=1, keepdims=True), sink)
                    p = jnp.exp(f - m)
                    denoms.append(jnp.sum(p, axis=1, keepdims=True) + jnp.exp(sink - m))
                    probs.append(jnp.concatenate(
                        [jnp.where(tri, 0.0, p), jnp.where(tri, p, 0.0)], axis=1).astype(BF16))
            o = _dot(jnp.concatenate(probs, axis=0), vv)
            for a in range(pairs):
                o_even = o[a * BLOCK:(a + 1) * BLOCK] / denoms[a]
                o_odd = o[(pairs + a) * BLOCK:(pairs + a + 1) * BLOCK] / denoms[pairs + a]
                ocols = slice((pairs * c + a) * LANES, (pairs * c + a + 1) * LANES)
                gated = jnp.where(low_o, o_even, o_odd) * g_ref[rows, ocols].astype(F32)
                y_scr[rows, ocols] = gated.astype(BF16)

    nb = 512
    for n in range(D_MODEL // nb):
        cs = slice(n * nb, (n + 1) * nb)
        o_ref[:, cs] = x_ref[:, cs] + _dot(y_scr[...], w_ref[:, cs])


def _odd_out(sinks, q, kv, g, x2d, w_out, batch, seq):
    t = x2d.shape[0]
    tm = OUT_TM
    ns = seq // tm
    bpt = tm // BLOCK

    def row(b, i):
        return b * ns + i

    return pl.pallas_call(
        _odd_out_kernel,
        grid=(batch, ns),
        in_specs=[
            pl.BlockSpec(memory_space=pltpu.SMEM),
            pl.BlockSpec((tm, ATTN_WIDTH), lambda b, i: (row(b, i), 0)),
            pl.BlockSpec((tm, 4 * KV_WIDTH), lambda b, i: (row(b, i), 0)),
            pl.BlockSpec((BLOCK, 4 * KV_WIDTH),
                         lambda b, i: (jnp.maximum(row(b, i) * bpt - 1, 0), 0)),
            pl.BlockSpec((tm, ATTN_WIDTH), lambda b, i: (row(b, i), 0)),
            pl.BlockSpec((tm, D_MODEL), lambda b, i: (row(b, i), 0)),
            pl.BlockSpec((ATTN_WIDTH, D_MODEL), lambda b, i: (0, 0)),
        ],
        out_specs=pl.BlockSpec((tm, D_MODEL), lambda b, i: (row(b, i), 0)),
        out_shape=jax.ShapeDtypeStruct((t, D_MODEL), F32),
        scratch_shapes=[pltpu.VMEM((tm, ATTN_WIDTH), BF16)],
        compiler_params=pltpu.CompilerParams(
            dimension_semantics=("arbitrary", "arbitrary"),
            vmem_limit_bytes=VMEM_LIMIT),
        name="odd_out",
    )(sinks, q, kv, kv, g, x2d, w_out)


def _rope_angles(seq, dim):
    inv = 1.0 / (ROPE_THETA ** (jnp.arange(0, dim, 2, dtype=F32) / dim))
    return jnp.arange(seq).astype(F32)[:, None] * inv[None, :]


def _retention_tables():
    h = np.arange(RET_HEADS, dtype=np.float64)
    log_g = np.log(1.0 - 2.0 ** (-5.0 - h))
    idx = np.arange(RET_CHUNK, dtype=np.float64)
    diff = idx[:, None] - idx[None, :]
    scale = RET_HEAD_DIM ** -0.5
    intra = np.where(diff >= 0, np.exp(log_g[:, None, None] * np.maximum(diff, 0.0)), 0.0)
    q_dec = np.exp(log_g[:, None] * (idx[None, :] + 1.0))
    k_dec = np.exp(log_g[:, None] * (RET_CHUNK - 1.0 - idx[None, :]))
    c_dec = np.exp(log_g * RET_CHUNK)
    wide = (RET_HEADS, RET_CHUNK, RET_HEAD_DIM)
    dmat = jnp.asarray(intra * scale, F32)
    qdec = jnp.asarray(np.broadcast_to(q_dec[:, :, None], wide), F32)
    kdec = jnp.asarray(np.broadcast_to(k_dec[:, :, None] * scale, wide), F32)
    cdec = jnp.asarray(np.broadcast_to(c_dec[:, None, None], (RET_HEADS, 1, RET_HEAD_DIM)), F32)
    return dmat, qdec, kdec, cdec


def _head_rope_tables(norm_w, seq, scale):
    half = ATTN_HEAD_DIM // 2
    ang = _rope_angles(seq, ATTN_HEAD_DIM)
    cos = jnp.cos(ang)
    sin = jnp.sin(ang)
    cos_h = jnp.concatenate([cos, cos], axis=1)
    sin_h = jnp.concatenate([-sin, sin], axis=1)
    w = norm_w.astype(F32)
    w_partner = jnp.concatenate([w[half:], w[:half]])
    ta = cos_h * (w * scale)[None, :]
    tb = sin_h * (w_partner * scale)[None, :]
    reps = LANES // ATTN_HEAD_DIM
    return jnp.tile(ta, (1, reps)), jnp.tile(tb, (1, reps))


def kernel(x, ev_norm_w, ev_w_in, ev_conv_w, ev_w_out, od_norm_w, od_w_in,
           od_q_norm_w, od_k_norm_w, od_sinks, od_w_out):
    batch, seq, d = x.shape
    assert d == D_MODEL and seq % IN_TM == 0 and seq % OUT_TM == 0
    x2d = x.reshape(batch * seq, d)

    w_in = ev_w_in[0]
    n_conv_tiles = CONV_WIDTH // CONV_TILE
    w_conv = w_in[:, 4 * RET_WIDTH:].reshape(d, 4, n_conv_tiles, CONV_TILE)
    w_conv = jnp.transpose(w_conv, (0, 2, 1, 3)).reshape(d, 4 * CONV_WIDTH)
    w_even = jnp.concatenate([w_in[:, :4 * RET_WIDTH], w_conv], axis=1).astype(BF16)
    ang = _rope_angles(seq, RET_HEAD_DIM)
    cw = ev_conv_w[0].reshape(CONV_K, n_conv_tiles, CONV_TILE).transpose(1, 0, 2)
    cw = jnp.pad(cw, ((0, 0), (0, 8 - CONV_K), (0, 0)))
    qkvg, conv = _even_in(x2d, ev_norm_w[0][None, :], w_even, jnp.cos(ang), jnp.sin(ang), cw, seq)
    dmat, qdec, kdec, cdec = _retention_tables()
    x2d = _even_out(qkvg, conv, x2d, ev_w_out[0].astype(BF16), dmat, qdec, kdec, cdec, batch, seq)

    w_in = od_w_in[0]

    def dup_heads(wc):
        wc = wc.reshape(d, ATTN_KV_HEADS, 1, ATTN_HEAD_DIM)
        return jnp.broadcast_to(wc, (d, ATTN_KV_HEADS, 2, ATTN_HEAD_DIM)).reshape(d, 2 * KV_WIDTH)

    w_odd = jnp.concatenate([
        w_in[:, :ATTN_WIDTH],
        dup_heads(w_in[:, ATTN_WIDTH:ATTN_WIDTH + KV_WIDTH]),
        dup_heads(w_in[:, ATTN_WIDTH + KV_WIDTH:ATTN_WIDTH + 2 * KV_WIDTH]),
        w_in[:, ATTN_WIDTH + 2 * KV_WIDTH:],
    ], axis=1).astype(BF16)
    qa, qb = _head_rope_tables(od_q_norm_w[0], seq, ATTN_HEAD_DIM ** -0.5)
    ka, kb = _head_rope_tables(od_k_norm_w[0], seq, 1.0)
    head_id = np.arange(MXU_N) // ATTN_HEAD_DIM
    ones_bd = jnp.asarray(head_id[:, None] == head_id[None, :], BF16)
    q, kv, g = _odd_in(x2d, od_norm_w[0][None, :], w_odd, qa, qb, ka, kb, ones_bd, seq)
    x2d = _odd_out(od_sinks[0].astype(F32), q, kv, g, x2d, od_w_out[0].astype(BF16), batch, seq)
    return x2d.reshape(batch, seq, d)
```

```python
import functools

import numpy as np
import jax
import jax.numpy as jnp
from jax import lax
from jax.experimental import pallas as pl
from jax.experimental.pallas import tpu as pltpu

D_MODEL = 2048
RET_HEADS = 4
RET_HEAD_DIM = 256
RET_WIDTH = RET_HEADS * RET_HEAD_DIM
CONV_WIDTH = D_MODEL - RET_WIDTH
CONV_K = 3
RET_CHUNK = 128
EVEN_IN = 4 * RET_WIDTH + 4 * CONV_WIDTH
ATTN_HEAD_DIM = 64
ATTN_Q_HEADS = 32
ATTN_KV_HEADS = 4
ATTN_GROUP = ATTN_Q_HEADS // ATTN_KV_HEADS
ATTN_WIDTH = ATTN_Q_HEADS * ATTN_HEAD_DIM
KV_WIDTH = ATTN_KV_HEADS * ATTN_HEAD_DIM
BLOCK = 128
ROPE_THETA = 10000.0
EPS = 1e-6
NEG = -1e30

LANES = 128
MXU_N = 256
SUB_N = 512
IN_TM = 512
IN_TN = 1024
OUT_TM = 512
CONV_TILE = 256
HALO = 8
VMEM_LIMIT = 56 * 1024 * 1024

F32 = jnp.float32
BF16 = jnp.bfloat16


def _dot(a, b):
    return jnp.dot(a, b, preferred_element_type=F32)


def _dot_nt(a, b):
    return lax.dot_general(a, b, (((1,), (1,)), ((), ())), preferred_element_type=F32)


def _dot_tn(a, b):
    return lax.dot_general(a, b, (((0,), (0,)), ((), ())), preferred_element_type=F32)


def _silu(g):
    return g / (1.0 + jnp.exp(-g))


def _rms_rows(x, nw):
    ms = jnp.mean(x * x, axis=-1, keepdims=True)
    return x * lax.rsqrt(ms + EPS) * nw


def _sub_cols(s):
    return slice(s * SUB_N, (s + 1) * SUB_N)


def _even_in_kernel(x_ref, nw_ref, wa_ref, wb_ref, wgc_ref, wu_ref, wgb_ref, wgs_ref,
                    cos_ref, sin_ref, cw_ref,
                    qkvg_ref, conv_ref, h_scr, cu_scr, *, seq_tiles):
    i = pl.program_id(0)
    j = pl.program_id(1)
    tm = x_ref.shape[0]
    w_refs = (wa_ref, wb_ref)

    @pl.when(j == 0)
    def _():
        h_scr[...] = _rms_rows(x_ref[...], nw_ref[...]).astype(BF16)

    def sub_dot(s):
        return _dot(h_scr[...], w_refs[s][...])

    def pair_dot(left_ref, right_ref):
        return _dot(h_scr[...], jnp.concatenate([left_ref[...], right_ref[...]], axis=1))

    @pl.when(j < 2)
    def _():
        cos = cos_ref[...]
        sin = sin_ref[...]
        for s in range(len(w_refs)):
            acc = sub_dot(s)
            for hd in range(SUB_N // RET_HEAD_DIM):
                base = s * SUB_N + hd * RET_HEAD_DIM
                x1 = acc[:, hd * RET_HEAD_DIM:hd * RET_HEAD_DIM + LANES]
                x2 = acc[:, hd * RET_HEAD_DIM + LANES:(hd + 1) * RET_HEAD_DIM]
                qkvg_ref[:, base:base + LANES] = (x1 * cos - x2 * sin).astype(BF16)
                qkvg_ref[:, base + LANES:base + RET_HEAD_DIM] = (x2 * cos + x1 * sin).astype(BF16)

    @pl.when(j == 2)
    def _():
        for s in range(len(w_refs)):
            qkvg_ref[:, _sub_cols(s)] = sub_dot(s).astype(BF16)

    @pl.when(j == 3)
    def _():
        for s in range(len(w_refs)):
            qkvg_ref[:, _sub_cols(s)] = _silu(sub_dot(s)).astype(BF16)

    @pl.when(j >= 4)
    def _():
        cs = j - 4
        cu_pair = pair_dot(wgc_ref, wu_ref)
        cu = cu_pair[:, :CONV_TILE] * cu_pair[:, CONV_TILE:]
        at_seq_start = (i % seq_tiles) == 0

        @pl.when(at_seq_start)
        def _():
            cu_scr[cs, HALO - (CONV_K - 1):HALO, :] = jnp.zeros((CONV_K - 1, CONV_TILE), F32)

        @pl.when(jnp.logical_not(at_seq_start))
        def _():
            cu_scr[cs, HALO - (CONV_K - 1):HALO, :] = cu_scr[cs, HALO + tm - (CONV_K - 1):HALO + tm, :]

        cu_scr[cs, HALO:HALO + tm, :] = cu
        cw = cw_ref[...]
        conv = cw[CONV_K - 1:CONV_K, :] * cu
        for t in range(CONV_K - 1):
            shift = CONV_K - 1 - t
            conv = conv + cw[t:t + 1, :] * cu_scr[cs, HALO - shift:HALO - shift + tm, :]
        gates = pair_dot(wgb_ref, wgs_ref)
        conv_ref[...] = (gates[:, :CONV_TILE] * conv * _silu(gates[:, CONV_TILE:])).astype(BF16)


def _even_in(x2d, nw, w, cos, sin, cw, seq):
    t = x2d.shape[0]
    tm = IN_TM
    seq_tiles = seq // tm
    n_ret_steps = 4 * RET_WIDTH // IN_TN
    n_conv_tiles = CONV_WIDTH // CONV_TILE
    nj = n_ret_steps + n_conv_tiles
    sub_per_step = IN_TN // SUB_N
    conv_col0 = 4 * RET_WIDTH // CONV_TILE

    def wide_spec(s):
        return pl.BlockSpec(
            (D_MODEL, SUB_N),
            lambda i, j: (0, sub_per_step * jnp.minimum(j, n_ret_steps - 1) + s))

    def conv_w_spec(branch):
        return pl.BlockSpec(
            (D_MODEL, CONV_TILE),
            lambda i, j: (0, conv_col0 + branch * n_conv_tiles + jnp.maximum(j - n_ret_steps, 0)))

    kern = functools.partial(_even_in_kernel, seq_tiles=seq_tiles)
    return pl.pallas_call(
        kern,
        grid=(t // tm, nj),
        in_specs=[
            pl.BlockSpec((tm, D_MODEL), lambda i, j: (i, 0)),
            pl.BlockSpec((1, D_MODEL), lambda i, j: (0, 0)),
            wide_spec(0), wide_spec(1),
            conv_w_spec(1), conv_w_spec(2), conv_w_spec(0), conv_w_spec(3),
            pl.BlockSpec((tm, LANES), lambda i, j: (i % seq_tiles, 0)),
            pl.BlockSpec((tm, LANES), lambda i, j: (i % seq_tiles, 0)),
            pl.BlockSpec((8, CONV_TILE), lambda i, j: (0, jnp.maximum(j - n_ret_steps, 0))),
        ],
        out_specs=[
            pl.BlockSpec((tm, IN_TN), lambda i, j: (i, jnp.minimum(j, n_ret_steps - 1))),
            pl.BlockSpec((tm, CONV_TILE), lambda i, j: (i, jnp.maximum(j - n_ret_steps, 0))),
        ],
        out_shape=[
            jax.ShapeDtypeStruct((t, 4 * RET_WIDTH), BF16),
            jax.ShapeDtypeStruct((t, CONV_WIDTH), BF16),
        ],
        scratch_shapes=[
            pltpu.VMEM((tm, D_MODEL), BF16),
            pltpu.VMEM((n_conv_tiles, HALO + tm, CONV_TILE), F32),
        ],
        compiler_params=pltpu.CompilerParams(
            dimension_semantics=("arbitrary", "arbitrary"),
            vmem_limit_bytes=VMEM_LIMIT),
        name="even_in",
    )(x2d, nw, w, w, w, w, w, w, cos, sin, cw)


def _even_out_kernel(q_ref, k_ref, v_ref, g_ref, conv_ref, x_ref, w_ref,
                     dmat_ref, qdec_ref, kdec_ref, cdec_ref,
                     o_ref, state_scr, y_scr):
    tm = x_ref.shape[0]

    @pl.when(pl.program_id(1) == 0)
    def _():
        state_scr[...] = jnp.zeros_like(state_scr)

    for c in range(tm // RET_CHUNK):
        rows = slice(c * RET_CHUNK, (c + 1) * RET_CHUNK)
        for h in range(RET_HEADS):
            cols = slice(h * RET_HEAD_DIM, (h + 1) * RET_HEAD_DIM)
            qh = q_ref[rows, cols]
            kh = k_ref[rows, cols]
            vh = v_ref[rows, cols]
            scores = _dot_nt(qh, kh) * dmat_ref[h]
            inner = _dot(scores.astype(BF16), vh)
            state = state_scr[h]
            cross = _dot(qh, state.astype(BF16)) * qdec_ref[h]
            o = inner + cross
            kd = (kh.astype(F32) * kdec_ref[h]).astype(BF16)
            state_scr[h] = cdec_ref[h] * state + _dot_tn(kd, vh)
            ms = jnp.mean(o * o, axis=-1, keepdims=True)
            ro = o * lax.rsqrt(ms + EPS) * g_ref[rows, cols].astype(F32)
            y_scr[rows, cols] = ro.astype(BF16)
    y_scr[:, RET_WIDTH:] = conv_ref[...]

    for n in range(D_MODEL // SUB_N):
        o_ref[:, _sub_cols(n)] = x_ref[:, _sub_cols(n)] + _dot(y_scr[...], w_ref[:, _sub_cols(n)])


def _even_out(qkvg, conv, x2d, w_out, dmat, qdec, kdec, cdec, batch, seq):
    t = x2d.shape[0]
    tm = OUT_TM
    ns = seq // tm

    def row(b, i):
        return b * ns + i

    def qkvg_spec(col):
        return pl.BlockSpec((tm, RET_WIDTH), lambda b, i: (row(b, i), col))

    def const_spec(shape):
        return pl.BlockSpec(shape, lambda b, i: (0,) * len(shape))

    return pl.pallas_call(
        _even_out_kernel,
        grid=(batch, ns),
        in_specs=[
            qkvg_spec(0), qkvg_spec(1), qkvg_spec(2), qkvg_spec(3),
            pl.BlockSpec((tm, CONV_WIDTH), lambda b, i: (row(b, i), 0)),
            pl.BlockSpec((tm, D_MODEL), lambda b, i: (row(b, i), 0)),
            const_spec((D_MODEL, D_MODEL)),
            const_spec(dmat.shape), const_spec(qdec.shape),
            const_spec(kdec.shape), const_spec(cdec.shape),
        ],
        out_specs=pl.BlockSpec((tm, D_MODEL), lambda b, i: (row(b, i), 0)),
        out_shape=jax.ShapeDtypeStruct((t, D_MODEL), F32),
        scratch_shapes=[
            pltpu.VMEM((RET_HEADS, RET_HEAD_DIM, RET_HEAD_DIM), F32),
            pltpu.VMEM((tm, D_MODEL), BF16),
        ],
        compiler_params=pltpu.CompilerParams(
            dimension_semantics=("arbitrary", "arbitrary"),
            vmem_limit_bytes=VMEM_LIMIT),
        name="even_out",
    )(qkvg, qkvg, qkvg, qkvg, conv, x2d, w_out, dmat, qdec, kdec, cdec)


def _odd_in_kernel(x_ref, nw_ref, wa_ref, wb_ref, wk_ref, wv_ref,
                   qa_ref, qb_ref, ka_ref, kb_ref, ones_ref,
                   q_ref, kv_ref, g_ref, h_scr):
    j = pl.program_id(1)
    w_refs = (wa_ref, wb_ref)

    @pl.when(j == 0)
    def _():
        h_scr[...] = _rms_rows(x_ref[...], nw_ref[...]).astype(BF16)

    def sub_dot(w_ref):
        return _dot(h_scr[...], w_ref[...])

    def norm_rope(acc, ta, tb):
        width = acc.shape[1]
        sq = (acc * acc).astype(BF16)
        ss = jnp.concatenate(
            [_dot(sq[:, c * MXU_N:(c + 1) * MXU_N], ones_ref[...]) for c in range(width // MXU_N)],
            axis=1)
        r = lax.rsqrt(ss * (1.0 / ATTN_HEAD_DIM) + EPS)
        half = ATTN_HEAD_DIM // 2
        lane = lax.broadcasted_iota(jnp.int32, acc.shape, 1)
        first_half = (lane % ATTN_HEAD_DIM) < half
        partner = jnp.where(first_half,
                            pltpu.roll(acc, width - half, axis=1),
                            pltpu.roll(acc, half, axis=1))
        reps = width // LANES
        ta_w = jnp.concatenate([ta] * reps, axis=1)
        tb_w = jnp.concatenate([tb] * reps, axis=1)
        return (r * (acc * ta_w + partner * tb_w)).astype(BF16)

    @pl.when(j < 2)
    def _():
        for s in range(len(w_refs)):
            q_ref[:, _sub_cols(s)] = norm_rope(sub_dot(w_refs[s]), qa_ref[...], qb_ref[...])

    @pl.when(j == 2)
    def _():
        kv_ref[:, _sub_cols(0)] = norm_rope(sub_dot(wk_ref), ka_ref[...], kb_ref[...])
        kv_ref[:, _sub_cols(1)] = sub_dot(wv_ref).astype(BF16)

    @pl.when(j >= 3)
    def _():
        for s in range(len(w_refs)):
            g_ref[:, _sub_cols(s)] = _silu(sub_dot(w_refs[s])).astype(BF16)


def _odd_in(x2d, nw, w, w_kv, qa, qb, ka, kb, ones_bd, seq):
    t = x2d.shape[0]
    tm = IN_TM
    seq_tiles = seq // tm
    q_steps = ATTN_WIDTH // IN_TN
    nj = 2 * q_steps + 1
    gate_col0 = (ATTN_WIDTH + 2 * KV_WIDTH) // SUB_N
    sub_per_step = IN_TN // SUB_N

    def wide_spec(s):
        def index(i, j):
            q_idx = sub_per_step * jnp.minimum(j, q_steps - 1) + s
            g_idx = gate_col0 + sub_per_step * (j - q_steps - 1) + s
            return 0, jnp.where(j <= q_steps, q_idx, g_idx)
        return pl.BlockSpec((D_MODEL, SUB_N), index)

    def tab_spec():
        return pl.BlockSpec((tm, LANES), lambda i, j: (i % seq_tiles, 0))

    return pl.pallas_call(
        _odd_in_kernel,
        grid=(t // tm, nj),
        in_specs=[
            pl.BlockSpec((tm, D_MODEL), lambda i, j: (i, 0)),
            pl.BlockSpec((1, D_MODEL), lambda i, j: (0, 0)),
            wide_spec(0), wide_spec(1),
            pl.BlockSpec((D_MODEL, SUB_N), lambda i, j: (0, 0)),
            pl.BlockSpec((D_MODEL, SUB_N), lambda i, j: (0, 1)),
            tab_spec(), tab_spec(), tab_spec(), tab_spec(),
            pl.BlockSpec((MXU_N, MXU_N), lambda i, j: (0, 0)),
        ],
        out_specs=[
            pl.BlockSpec((tm, IN_TN), lambda i, j: (i, jnp.minimum(j, q_steps - 1))),
            pl.BlockSpec((tm, IN_TN), lambda i, j: (i, 0)),
            pl.BlockSpec((tm, IN_TN), lambda i, j: (i, jnp.clip(j - q_steps - 1, 0, q_steps - 1))),
        ],
        out_shape=[
            jax.ShapeDtypeStruct((t, ATTN_WIDTH), BF16),
            jax.ShapeDtypeStruct((t, 4 * KV_WIDTH), BF16),
            jax.ShapeDtypeStruct((t, ATTN_WIDTH), BF16),
        ],
        scratch_shapes=[pltpu.VMEM((tm, D_MODEL), BF16)],
        compiler_params=pltpu.CompilerParams(
            dimension_semantics=("arbitrary", "arbitrary"),
            vmem_limit_bytes=VMEM_LIMIT),
        name="odd_in",
    )(x2d, nw, w, w, w_kv, w_kv, qa, qb, ka, kb, ones_bd)


def _odd_out_kernel(sink_ref, q_ref, kv_ref, kvp_ref, g_ref, x_ref, w_ref,
                    o_ref, y_scr):
    tm = x_ref.shape[0]
    at_seq_start = pl.program_id(1) == 0
    pairs = ATTN_GROUP // 2
    half_w = 2 * KV_WIDTH

    lane_kv = lax.broadcasted_iota(jnp.int32, (2 * BLOCK, LANES), 1)
    low_kv = lane_kv < ATTN_HEAD_DIM
    rr = lax.broadcasted_iota(jnp.int32, (BLOCK, BLOCK), 0)
    cc = lax.broadcasted_iota(jnp.int32, (BLOCK, BLOCK), 1)
    tri = cc <= rr
    low_o = cc < ATTN_HEAD_DIM

    for bl in range(tm // BLOCK):
        rows = slice(bl * BLOCK, (bl + 1) * BLOCK)
        for c in range(ATTN_KV_HEADS):
            kcols = slice(c * LANES, (c + 1) * LANES)
            vcols = slice(half_w + c * LANES, half_w + (c + 1) * LANES)
            if bl == 0:
                k_prev = kvp_ref[:, kcols]
                v_prev = kvp_ref[:, vcols]
            else:
                prows = slice((bl - 1) * BLOCK, bl * BLOCK)
                k_prev = kv_ref[prows, kcols]
                v_prev = kv_ref[prows, vcols]
            kk = jnp.concatenate([k_prev, kv_ref[rows, kcols]], axis=0)
            vv = jnp.concatenate([v_prev, kv_ref[rows, vcols]], axis=0)
            zero = jnp.zeros_like(kk)
            k_par = (jnp.where(low_kv, kk, zero), jnp.where(low_kv, zero, kk))
            qs = jnp.concatenate(
                [q_ref[rows, (pairs * c + a) * LANES:(pairs * c + a + 1) * LANES]
                 for a in range(pairs)], axis=0)
            probs = []
            denoms = []
            for par in range(2):
                s = _dot_nt(qs, k_par[par])
                for a in range(pairs):
                    sa = s[a * BLOCK:(a + 1) * BLOCK]
                    s_prev = sa[:, :BLOCK]
                    if bl == 0:
                        s_prev = jnp.where(at_seq_start, NEG, s_prev)
                    f = jnp.where(tri, sa[:, BLOCK:], s_prev)
                    sink = sink_ref[ATTN_GROUP * c + 2 * a + par]
                    m = jnp.maximum(jnp.max(f, axis=1, keepdims=True), sink)
                    p = jnp.exp(f - m)
                    denoms.append(jnp.sum(p, axis=1, keepdims=True) + jnp.exp(sink - m))
                    probs.append(jnp.concatenate(
                        [jnp.where(tri, 0.0, p), jnp.where(tri, p, 0.0)], axis=1).astype(BF16))
            o = _dot(jnp.concatenate(probs, axis=0), vv)
            for a in range(pairs):
                o_even = o[a * BLOCK:(a + 1) * BLOCK] / denoms[a]
                o_odd = o[(pairs + a) * BLOCK:(pairs + a + 1) * BLOCK] / denoms[pairs + a]
                ocols = slice((pairs * c + a) * LANES, (pairs * c + a + 1) * LANES)
                gated = jnp.where(low_o, o_even, o_odd) * g_ref[rows, ocols].astype(F32)
                y_scr[rows, ocols] = gated.astype(BF16)

    for n in range(D_MODEL // SUB_N):
        o_ref[:, _sub_cols(n)] = x_ref[:, _sub_cols(n)] + _dot(y_scr[...], w_ref[:, _sub_cols(n)])


def _odd_out(sinks, q, kv, g, x2d, w_out, batch, seq):
    t = x2d.shape[0]
    tm = OUT_TM
    ns = seq // tm
    bpt = tm // BLOCK

    def row(b, i):
        return b * ns + i

    return pl.pallas_call(
        _odd_out_kernel,
        grid=(batch, ns),
        in_specs=[
            pl.BlockSpec(memory_space=pltpu.SMEM),
            pl.BlockSpec((tm, ATTN_WIDTH), lambda b, i: (row(b, i), 0)),
            pl.BlockSpec((tm, 4 * KV_WIDTH), lambda b, i: (row(b, i), 0)),
            pl.BlockSpec((BLOCK, 4 * KV_WIDTH),
                         lambda b, i: (jnp.maximum(row(b, i) * bpt - 1, 0), 0)),
            pl.BlockSpec((tm, ATTN_WIDTH), lambda b, i: (row(b, i), 0)),
            pl.BlockSpec((tm, D_MODEL), lambda b, i: (row(b, i), 0)),
            pl.BlockSpec((ATTN_WIDTH, D_MODEL), lambda b, i: (0, 0)),
        ],
        out_specs=pl.BlockSpec((tm, D_MODEL), lambda b, i: (row(b, i), 0)),
        out_shape=jax.ShapeDtypeStruct((t, D_MODEL), F32),
        scratch_shapes=[pltpu.VMEM((tm, ATTN_WIDTH), BF16)],
        compiler_params=pltpu.CompilerParams(
            dimension_semantics=("arbitrary", "arbitrary"),
            vmem_limit_bytes=VMEM_LIMIT),
        name="odd_out",
    )(sinks, q, kv, kv, g, x2d, w_out)


def _rope_angles(seq, dim):
    inv = 1.0 / (ROPE_THETA ** (jnp.arange(0, dim, 2, dtype=F32) / dim))
    return jnp.arange(seq).astype(F32)[:, None] * inv[None, :]


def _retention_tables():
    h = np.arange(RET_HEADS, dtype=np.float64)
    log_g = np.log(1.0 - 2.0 ** (-5.0 - h))
    idx = np.arange(RET_CHUNK, dtype=np.float64)
    diff = idx[:, None] - idx[None, :]
    scale = RET_HEAD_DIM ** -0.5
    intra = np.where(diff >= 0, np.exp(log_g[:, None, None] * np.maximum(diff, 0.0)), 0.0)
    q_dec = np.exp(log_g[:, None] * (idx[None, :] + 1.0))
    k_dec = np.exp(log_g[:, None] * (RET_CHUNK - 1.0 - idx[None, :]))
    c_dec = np.exp(log_g * RET_CHUNK)
    wide = (RET_HEADS, RET_CHUNK, RET_HEAD_DIM)
    dmat = jnp.asarray(intra * scale, F32)
    qdec = jnp.asarray(np.broadcast_to(q_dec[:, :, None], wide), F32)
    kdec = jnp.asarray(np.broadcast_to(k_dec[:, :, None] * scale, wide), F32)
    cdec = jnp.asarray(np.broadcast_to(c_dec[:, None, None], (RET_HEADS, 1, RET_HEAD_DIM)), F32)
    return dmat, qdec, kdec, cdec


def _head_rope_tables(norm_w, seq, scale):
    half = ATTN_HEAD_DIM // 2
    ang = _rope_angles(seq, ATTN_HEAD_DIM)
    cos = jnp.cos(ang)
    sin = jnp.sin(ang)
    cos_h = jnp.concatenate([cos, cos], axis=1)
    sin_h = jnp.concatenate([-sin, sin], axis=1)
    w = norm_w.astype(F32)
    w_partner = jnp.concatenate([w[half:], w[:half]])
    ta = cos_h * (w * scale)[None, :]
    tb = sin_h * (w_partner * scale)[None, :]
    reps = LANES // ATTN_HEAD_DIM
    return jnp.tile(ta, (1, reps)), jnp.tile(tb, (1, reps))


def kernel(x, ev_norm_w, ev_w_in, ev_conv_w, ev_w_out, od_norm_w, od_w_in,
           od_q_norm_w, od_k_norm_w, od_sinks, od_w_out):
    batch, seq, d = x.shape
    assert d == D_MODEL and seq % IN_TM == 0 and seq % OUT_TM == 0
    x2d = x.reshape(batch * seq, d)

    ang = _rope_angles(seq, RET_HEAD_DIM)
    cw = jnp.pad(ev_conv_w[0], ((0, 8 - CONV_K), (0, 0)))
    qkvg, conv = _even_in(x2d, ev_norm_w[0][None, :], ev_w_in[0].astype(BF16),
                          jnp.cos(ang), jnp.sin(ang), cw, seq)
    dmat, qdec, kdec, cdec = _retention_tables()
    x2d = _even_out(qkvg, conv, x2d, ev_w_out[0].astype(BF16), dmat, qdec, kdec, cdec, batch, seq)

    w_in = od_w_in[0]

    def dup_heads(wc):
        wc = wc.reshape(d, ATTN_KV_HEADS, 1, ATTN_HEAD_DIM)
        return jnp.broadcast_to(wc, (d, ATTN_KV_HEADS, 2, ATTN_HEAD_DIM)).reshape(d, 2 * KV_WIDTH)

    w_kv = jnp.concatenate([
        dup_heads(w_in[:, ATTN_WIDTH:ATTN_WIDTH + KV_WIDTH]),
        dup_heads(w_in[:, ATTN_WIDTH + KV_WIDTH:ATTN_WIDTH + 2 * KV_WIDTH]),
    ], axis=1).astype(BF16)
    qa, qb = _head_rope_tables(od_q_norm_w[0], seq, ATTN_HEAD_DIM ** -0.5)
    ka, kb = _head_rope_tables(od_k_norm_w[0], seq, 1.0)
    head_id = np.arange(MXU_N) // ATTN_HEAD_DIM
    ones_bd = jnp.asarray(head_id[:, None] == head_id[None, :], BF16)
    q, kv, g = _odd_in(x2d, od_norm_w[0][None, :], w_in.astype(BF16), w_kv,
                       qa, qb, ka, kb, ones_bd, seq)
    x2d = _odd_out(od_sinks[0].astype(F32), q, kv, g, x2d, od_w_out[0].astype(BF16), batch, seq)
    return x2d.reshape(batch, seq, d)
```

```python
import functools

import numpy as np
import jax
import jax.numpy as jnp
from jax import lax
from jax.experimental import pallas as pl
from jax.experimental.pallas import tpu as pltpu

D_MODEL = 2048
RET_HEADS = 4
RET_HEAD_DIM = 256
RET_WIDTH = RET_HEADS * RET_HEAD_DIM
CONV_WIDTH = D_MODEL - RET_WIDTH
CONV_K = 3
RET_CHUNK = 128
EVEN_IN = 4 * RET_WIDTH + 4 * CONV_WIDTH
ATTN_HEAD_DIM = 64
ATTN_Q_HEADS = 32
ATTN_KV_HEADS = 4
ATTN_GROUP = ATTN_Q_HEADS // ATTN_KV_HEADS
ATTN_WIDTH = ATTN_Q_HEADS * ATTN_HEAD_DIM
KV_WIDTH = ATTN_KV_HEADS * ATTN_HEAD_DIM
BLOCK = 128
ROPE_THETA = 10000.0
EPS = 1e-6
NEG = -1e30

LANES = 128
MXU_N = 256
SUB_N = 512
IN_TM = 1024
IN_TN = 1024
OUT_TM = 512
CONV_TILE = 256
HALO = 8
VMEM_LIMIT = 56 * 1024 * 1024

F32 = jnp.float32
BF16 = jnp.bfloat16


def _dot(a, b):
    return jnp.dot(a, b, preferred_element_type=F32)


def _dot_nt(a, b):
    return lax.dot_general(a, b, (((1,), (1,)), ((), ())), preferred_element_type=F32)


def _dot_tn(a, b):
    return lax.dot_general(a, b, (((0,), (0,)), ((), ())), preferred_element_type=F32)


def _silu(g):
    return g / (1.0 + jnp.exp(-g))


def _rms_rows(x, nw):
    ms = jnp.mean(x * x, axis=-1, keepdims=True)
    return x * lax.rsqrt(ms + EPS) * nw


def _sub_cols(s):
    return slice(s * SUB_N, (s + 1) * SUB_N)


def _even_in_kernel(x_ref, nw_ref, wa_ref, wb_ref, wgc_ref, wu_ref, wgb_ref, wgs_ref,
                    cos_ref, sin_ref, cw_ref,
                    qkvg_ref, conv_ref, h_scr, cu_scr, tail_scr, *, seq_tiles):
    i = pl.program_id(0)
    j = pl.program_id(1)
    tm = x_ref.shape[0]
    w_refs = (wa_ref, wb_ref)

    @pl.when(j == 0)
    def _():
        h_scr[...] = _rms_rows(x_ref[...], nw_ref[...]).astype(BF16)

    def sub_dot(s):
        return _dot(h_scr[...], w_refs[s][...])

    def pair_dot(left_ref, right_ref):
        return _dot(h_scr[...], jnp.concatenate([left_ref[...], right_ref[...]], axis=1))

    @pl.when(j < 2)
    def _():
        cos = cos_ref[...]
        sin = sin_ref[...]
        for s in range(len(w_refs)):
            acc = sub_dot(s)
            for hd in range(SUB_N // RET_HEAD_DIM):
                base = s * SUB_N + hd * RET_HEAD_DIM
                x1 = acc[:, hd * RET_HEAD_DIM:hd * RET_HEAD_DIM + LANES]
                x2 = acc[:, hd * RET_HEAD_DIM + LANES:(hd + 1) * RET_HEAD_DIM]
                qkvg_ref[:, base:base + LANES] = (x1 * cos - x2 * sin).astype(BF16)
                qkvg_ref[:, base + LANES:base + RET_HEAD_DIM] = (x2 * cos + x1 * sin).astype(BF16)

    @pl.when(j == 2)
    def _():
        for s in range(len(w_refs)):
            qkvg_ref[:, _sub_cols(s)] = sub_dot(s).astype(BF16)

    @pl.when(j == 3)
    def _():
        for s in range(len(w_refs)):
            qkvg_ref[:, _sub_cols(s)] = _silu(sub_dot(s)).astype(BF16)

    @pl.when(j >= 4)
    def _():
        cs = j - 4
        cu_pair = pair_dot(wgc_ref, wu_ref)
        cu = cu_pair[:, :CONV_TILE] * cu_pair[:, CONV_TILE:]
        at_seq_start = (i % seq_tiles) == 0

        halo_rows = slice(HALO - (CONV_K - 1), HALO)

        @pl.when(at_seq_start)
        def _():
            cu_scr[halo_rows, :] = jnp.zeros((CONV_K - 1, CONV_TILE), F32)

        @pl.when(jnp.logical_not(at_seq_start))
        def _():
            cu_scr[halo_rows, :] = tail_scr[cs, halo_rows, :]

        cu_scr[HALO:HALO + tm, :] = cu
        tail_scr[cs, halo_rows, :] = cu_scr[HALO + tm - (CONV_K - 1):HALO + tm, :]
        cw = cw_ref[...]
        conv = cw[CONV_K - 1:CONV_K, :] * cu
        for t in range(CONV_K - 1):
            shift = CONV_K - 1 - t
            conv = conv + cw[t:t + 1, :] * cu_scr[HALO - shift:HALO - shift + tm, :]
        gates = pair_dot(wgb_ref, wgs_ref)
        conv_ref[...] = (gates[:, :CONV_TILE] * conv * _silu(gates[:, CONV_TILE:])).astype(BF16)


def _even_in(x2d, nw, w, cos, sin, cw, seq):
    t = x2d.shape[0]
    tm = IN_TM
    seq_tiles = seq // tm
    n_ret_steps = 4 * RET_WIDTH // IN_TN
    n_conv_tiles = CONV_WIDTH // CONV_TILE
    nj = n_ret_steps + n_conv_tiles
    sub_per_step = IN_TN // SUB_N
    conv_col0 = 4 * RET_WIDTH // CONV_TILE

    def wide_spec(s):
        return pl.BlockSpec(
            (D_MODEL, SUB_N),
            lambda i, j: (0, sub_per_step * jnp.minimum(j, n_ret_steps - 1) + s))

    def conv_w_spec(branch):
        return pl.BlockSpec(
            (D_MODEL, CONV_TILE),
            lambda i, j: (0, conv_col0 + branch * n_conv_tiles + jnp.maximum(j - n_ret_steps, 0)))

    kern = functools.partial(_even_in_kernel, seq_tiles=seq_tiles)
    return pl.pallas_call(
        kern,
        grid=(t // tm, nj),
        in_specs=[
            pl.BlockSpec((tm, D_MODEL), lambda i, j: (i, 0)),
            pl.BlockSpec((1, D_MODEL), lambda i, j: (0, 0), pipeline_mode=pl.Buffered(1)),
            wide_spec(0), wide_spec(1),
            conv_w_spec(1), conv_w_spec(2), conv_w_spec(0), conv_w_spec(3),
            pl.BlockSpec((tm, LANES), lambda i, j: (i % seq_tiles, 0)),
            pl.BlockSpec((tm, LANES), lambda i, j: (i % seq_tiles, 0)),
            pl.BlockSpec((8, CONV_TILE), lambda i, j: (0, jnp.maximum(j - n_ret_steps, 0))),
        ],
        out_specs=[
            pl.BlockSpec((tm, IN_TN), lambda i, j: (i, jnp.minimum(j, n_ret_steps - 1))),
            pl.BlockSpec((tm, CONV_TILE), lambda i, j: (i, jnp.maximum(j - n_ret_steps, 0))),
        ],
        out_shape=[
            jax.ShapeDtypeStruct((t, 4 * RET_WIDTH), BF16),
            jax.ShapeDtypeStruct((t, CONV_WIDTH), BF16),
        ],
        scratch_shapes=[
            pltpu.VMEM((tm, D_MODEL), BF16),
            pltpu.VMEM((HALO + tm, CONV_TILE), F32),
            pltpu.VMEM((n_conv_tiles, HALO, CONV_TILE), F32),
        ],
        compiler_params=pltpu.CompilerParams(
            dimension_semantics=("arbitrary", "arbitrary"),
            vmem_limit_bytes=VMEM_LIMIT),
        name="even_in",
    )(x2d, nw, w, w, w, w, w, w, cos, sin, cw)


def _even_out_kernel(q_ref, k_ref, v_ref, g_ref, conv_ref, x_ref, w_ref,
                     dmat_ref, qdec_ref, kdec_ref, cdec_ref,
                     o_ref, state_scr, y_scr):
    tm = x_ref.shape[0]

    @pl.when(pl.program_id(1) == 0)
    def _():
        state_scr[...] = jnp.zeros_like(state_scr)

    for c in range(tm // RET_CHUNK):
        rows = slice(c * RET_CHUNK, (c + 1) * RET_CHUNK)
        for h in range(RET_HEADS):
            cols = slice(h * RET_HEAD_DIM, (h + 1) * RET_HEAD_DIM)
            qh = q_ref[rows, cols]
            kh = k_ref[rows, cols]
            vh = v_ref[rows, cols]
            scores = _dot_nt(qh, kh) * dmat_ref[h]
            inner = _dot(scores.astype(BF16), vh)
            state = state_scr[h]
            cross = _dot(qh, state.astype(BF16)) * qdec_ref[h]
            o = inner + cross
            kd = (kh.astype(F32) * kdec_ref[h]).astype(BF16)
            state_scr[h] = cdec_ref[h] * state + _dot_tn(kd, vh)
            ms = jnp.mean(o * o, axis=-1, keepdims=True)
            ro = o * lax.rsqrt(ms + EPS) * g_ref[rows, cols].astype(F32)
            y_scr[rows, cols] = ro.astype(BF16)
    y_scr[:, RET_WIDTH:] = conv_ref[...]

    for n in range(D_MODEL // SUB_N):
        o_ref[:, _sub_cols(n)] = x_ref[:, _sub_cols(n)] + _dot(y_scr[...], w_ref[:, _sub_cols(n)])


def _even_out(qkvg, conv, x2d, w_out, dmat, qdec, kdec, cdec, batch, seq):
    t = x2d.shape[0]
    tm = OUT_TM
    ns = seq // tm

    def row(b, i):
        return b * ns + i

    def qkvg_spec(col):
        return pl.BlockSpec((tm, RET_WIDTH), lambda b, i: (row(b, i), col))

    def const_spec(shape):
        return pl.BlockSpec(shape, lambda b, i: (0,) * len(shape), pipeline_mode=pl.Buffered(1))

    return pl.pallas_call(
        _even_out_kernel,
        grid=(batch, ns),
        in_specs=[
            qkvg_spec(0), qkvg_spec(1), qkvg_spec(2), qkvg_spec(3),
            pl.BlockSpec((tm, CONV_WIDTH), lambda b, i: (row(b, i), 0)),
            pl.BlockSpec((tm, D_MODEL), lambda b, i: (row(b, i), 0)),
            const_spec((D_MODEL, D_MODEL)),
            const_spec(dmat.shape), const_spec(qdec.shape),
            const_spec(kdec.shape), const_spec(cdec.shape),
        ],
        out_specs=pl.BlockSpec((tm, D_MODEL), lambda b, i: (row(b, i), 0)),
        out_shape=jax.ShapeDtypeStruct((t, D_MODEL), F32),
        scratch_shapes=[
            pltpu.VMEM((RET_HEADS, RET_HEAD_DIM, RET_HEAD_DIM), F32),
            pltpu.VMEM((tm, D_MODEL), BF16),
        ],
        compiler_params=pltpu.CompilerParams(
            dimension_semantics=("arbitrary", "arbitrary"),
            vmem_limit_bytes=VMEM_LIMIT),
        name="even_out",
    )(qkvg, qkvg, qkvg, qkvg, conv, x2d, w_out, dmat, qdec, kdec, cdec)


def _odd_in_kernel(x_ref, nw_ref, wa_ref, wb_ref, wk_ref, wv_ref,
                   qa_ref, qb_ref, ka_ref, kb_ref, ones_ref,
                   out_ref, h_scr, *, q_steps):
    j = pl.program_id(1)
    w_refs = (wa_ref, wb_ref)

    @pl.when(j == 0)
    def _():
        h_scr[...] = _rms_rows(x_ref[...], nw_ref[...]).astype(BF16)

    def sub_dot(w_ref):
        return _dot(h_scr[...], w_ref[...])

    def norm_rope(acc, ta, tb):
        width = acc.shape[1]
        sq = (acc * acc).astype(BF16)
        ss = jnp.concatenate(
            [_dot(sq[:, c * MXU_N:(c + 1) * MXU_N], ones_ref[...]) for c in range(width // MXU_N)],
            axis=1)
        r = lax.rsqrt(ss * (1.0 / ATTN_HEAD_DIM) + EPS)
        half = ATTN_HEAD_DIM // 2
        lane = lax.broadcasted_iota(jnp.int32, acc.shape, 1)
        first_half = (lane % ATTN_HEAD_DIM) < half
        partner = jnp.where(first_half,
                            pltpu.roll(acc, width - half, axis=1),
                            pltpu.roll(acc, half, axis=1))
        reps = width // LANES
        ta_w = jnp.concatenate([ta] * reps, axis=1)
        tb_w = jnp.concatenate([tb] * reps, axis=1)
        return (r * (acc * ta_w + partner * tb_w)).astype(BF16)

    @pl.when(j < q_steps)
    def _():
        for s in range(len(w_refs)):
            out_ref[:, _sub_cols(s)] = norm_rope(sub_dot(w_refs[s]), qa_ref[...], qb_ref[...])

    @pl.when(jnp.logical_and(j >= q_steps, j < 2 * q_steps))
    def _():
        for s in range(len(w_refs)):
            out_ref[:, _sub_cols(s)] = _silu(sub_dot(w_refs[s])).astype(BF16)

    @pl.when(j == 2 * q_steps)
    def _():
        out_ref[:, _sub_cols(0)] = norm_rope(sub_dot(wk_ref), ka_ref[...], kb_ref[...])
        out_ref[:, _sub_cols(1)] = sub_dot(wv_ref).astype(BF16)


def _odd_in(x2d, nw, w, w_kv, qa, qb, ka, kb, ones_bd, seq):
    t = x2d.shape[0]
    tm = IN_TM
    seq_tiles = seq // tm
    q_steps = ATTN_WIDTH // IN_TN
    nj = 2 * q_steps + 1
    gate_col0 = (ATTN_WIDTH + 2 * KV_WIDTH) // SUB_N
    sub_per_step = IN_TN // SUB_N

    def wide_spec(s):
        def index(i, j):
            q_idx = sub_per_step * j + s
            g_idx = gate_col0 + sub_per_step * (jnp.minimum(j, 2 * q_steps - 1) - q_steps) + s
            return 0, jnp.where(j < q_steps, q_idx, g_idx)
        return pl.BlockSpec((D_MODEL, SUB_N), index)

    def tab_spec():
        return pl.BlockSpec((tm, LANES), lambda i, j: (i % seq_tiles, 0))

    def resident_spec(shape, col):
        return pl.BlockSpec(shape, lambda i, j: (0, col), pipeline_mode=pl.Buffered(1))

    kern = functools.partial(_odd_in_kernel, q_steps=q_steps)
    return pl.pallas_call(
        kern,
        grid=(t // tm, nj),
        in_specs=[
            pl.BlockSpec((tm, D_MODEL), lambda i, j: (i, 0)),
            resident_spec((1, D_MODEL), 0),
            wide_spec(0), wide_spec(1),
            resident_spec((D_MODEL, SUB_N), 0),
            resident_spec((D_MODEL, SUB_N), 1),
            tab_spec(), tab_spec(), tab_spec(), tab_spec(),
            resident_spec((MXU_N, MXU_N), 0),
        ],
        out_specs=pl.BlockSpec((tm, IN_TN), lambda i, j: (i, j)),
        out_shape=jax.ShapeDtypeStruct((t, nj * IN_TN), BF16),
        scratch_shapes=[pltpu.VMEM((tm, D_MODEL), BF16)],
        compiler_params=pltpu.CompilerParams(
            dimension_semantics=("arbitrary", "arbitrary"),
            vmem_limit_bytes=VMEM_LIMIT),
        name="odd_in",
    )(x2d, nw, w, w, w_kv, w_kv, qa, qb, ka, kb, ones_bd)


def _odd_out_kernel(sink_ref, q_ref, kv_ref, kvp_ref, g_ref, x_ref, w_ref,
                    o_ref, y_scr):
    tm = x_ref.shape[0]
    at_seq_start = pl.program_id(1) == 0
    pairs = ATTN_GROUP // 2
    half_w = 2 * KV_WIDTH

    lane_kv = lax.broadcasted_iota(jnp.int32, (2 * BLOCK, LANES), 1)
    low_kv = lane_kv < ATTN_HEAD_DIM
    rr = lax.broadcasted_iota(jnp.int32, (BLOCK, BLOCK), 0)
    cc = lax.broadcasted_iota(jnp.int32, (BLOCK, BLOCK), 1)
    tri = cc <= rr
    low_o = cc < ATTN_HEAD_DIM

    for bl in range(tm // BLOCK):
        rows = slice(bl * BLOCK, (bl + 1) * BLOCK)
        for c in range(ATTN_KV_HEADS):
            kcols = slice(c * LANES, (c + 1) * LANES)
            vcols = slice(half_w + c * LANES, half_w + (c + 1) * LANES)
            if bl == 0:
                k_prev = kvp_ref[:, kcols]
                v_prev = kvp_ref[:, vcols]
            else:
                prows = slice((bl - 1) * BLOCK, bl * BLOCK)
                k_prev = kv_ref[prows, kcols]
                v_prev = kv_ref[prows, vcols]
            kk = jnp.concatenate([k_prev, kv_ref[rows, kcols]], axis=0)
            vv = jnp.concatenate([v_prev, kv_ref[rows, vcols]], axis=0)
            zero = jnp.zeros_like(kk)
            k_par = (jnp.where(low_kv, kk, zero), jnp.where(low_kv, zero, kk))
            qs = jnp.concatenate(
                [q_ref[rows, (pairs * c + a) * LANES:(pairs * c + a + 1) * LANES]
                 for a in range(pairs)], axis=0)
            probs = []
            denoms = []
            for par in range(2):
                s = _dot_nt(qs, k_par[par])
                for a in range(pairs):
                    sa = s[a * BLOCK:(a + 1) * BLOCK]
                    s_prev = sa[:, :BLOCK]
                    if bl == 0:
                        s_prev = jnp.where(at_seq_start, NEG, s_prev)
                    f = jnp.where(tri, sa[:, BLOCK:], s_prev)
                    sink = sink_ref[ATTN_GROUP * c + 2 * a + par]
                    m = jnp.maximum(jnp.max(f, axis=1, keepdims=True), sink)
                    p = jnp.exp(f - m)
                    denoms.append(jnp.sum(p, axis=1, keepdims=True) + jnp.exp(sink - m))
                    probs.append(jnp.concatenate(
                        [jnp.where(tri, 0.0, p), jnp.where(tri, p, 0.0)], axis=1).astype(BF16))
            o = _dot(jnp.concatenate(probs, axis=0), vv)
            for a in range(pairs):
                o_even = o[a * BLOCK:(a + 1) * BLOCK] / denoms[a]
                o_odd = o[(pairs + a) * BLOCK:(pairs + a + 1) * BLOCK] / denoms[pairs + a]
                ocols = slice((pairs * c + a) * LANES, (pairs * c + a + 1) * LANES)
                gated = jnp.where(low_o, o_even, o_odd) * g_ref[rows, ocols].astype(F32)
                y_scr[rows, ocols] = gated.astype(BF16)

    for n in range(D_MODEL // SUB_N):
        o_ref[:, _sub_cols(n)] = x_ref[:, _sub_cols(n)] + _dot(y_scr[...], w_ref[:, _sub_cols(n)])


def _odd_out(sinks, qgkv, x2d, w_out, batch, seq):
    t = x2d.shape[0]
    tm = OUT_TM
    ns = seq // tm
    bpt = tm // BLOCK
    kv_col = 2 * ATTN_WIDTH // (4 * KV_WIDTH)

    def row(b, i):
        return b * ns + i

    return pl.pallas_call(
        _odd_out_kernel,
        grid=(batch, ns),
        in_specs=[
            pl.BlockSpec(memory_space=pltpu.SMEM),
            pl.BlockSpec((tm, ATTN_WIDTH), lambda b, i: (row(b, i), 0)),
            pl.BlockSpec((tm, 4 * KV_WIDTH), lambda b, i: (row(b, i), kv_col)),
            pl.BlockSpec((BLOCK, 4 * KV_WIDTH),
                         lambda b, i: (jnp.maximum(row(b, i) * bpt - 1, 0), kv_col)),
            pl.BlockSpec((tm, ATTN_WIDTH), lambda b, i: (row(b, i), 1)),
            pl.BlockSpec((tm, D_MODEL), lambda b, i: (row(b, i), 0)),
            pl.BlockSpec((ATTN_WIDTH, D_MODEL), lambda b, i: (0, 0), pipeline_mode=pl.Buffered(1)),
        ],
        out_specs=pl.BlockSpec((tm, D_MODEL), lambda b, i: (row(b, i), 0)),
        out_shape=jax.ShapeDtypeStruct((t, D_MODEL), F32),
        scratch_shapes=[pltpu.VMEM((tm, ATTN_WIDTH), BF16)],
        compiler_params=pltpu.CompilerParams(
            dimension_semantics=("arbitrary", "arbitrary"),
            vmem_limit_bytes=VMEM_LIMIT),
        name="odd_out",
    )(sinks, qgkv, qgkv, qgkv, qgkv, x2d, w_out)


def _rope_angles(seq, dim):
    inv = 1.0 / (ROPE_THETA ** (jnp.arange(0, dim, 2, dtype=F32) / dim))
    return jnp.arange(seq).astype(F32)[:, None] * inv[None, :]


def _retention_tables():
    h = np.arange(RET_HEADS, dtype=np.float64)
    log_g = np.log(1.0 - 2.0 ** (-5.0 - h))
    idx = np.arange(RET_CHUNK, dtype=np.float64)
    diff = idx[:, None] - idx[None, :]
    scale = RET_HEAD_DIM ** -0.5
    intra = np.where(diff >= 0, np.exp(log_g[:, None, None] * np.maximum(diff, 0.0)), 0.0)
    q_dec = np.exp(log_g[:, None] * (idx[None, :] + 1.0))
    k_dec = np.exp(log_g[:, None] * (RET_CHUNK - 1.0 - idx[None, :]))
    c_dec = np.exp(log_g * RET_CHUNK)
    wide = (RET_HEADS, RET_CHUNK, RET_HEAD_DIM)
    dmat = jnp.asarray(intra * scale, F32)
    qdec = jnp.asarray(np.broadcast_to(q_dec[:, :, None], wide), F32)
    kdec = jnp.asarray(np.broadcast_to(k_dec[:, :, None] * scale, wide), F32)
    cdec = jnp.asarray(np.broadcast_to(c_dec[:, None, None], (RET_HEADS, 1, RET_HEAD_DIM)), F32)
    return dmat, qdec, kdec, cdec


def _head_rope_tables(norm_w, seq, scale):
    half = ATTN_HEAD_DIM // 2
    ang = _rope_angles(seq, ATTN_HEAD_DIM)
    cos = jnp.cos(ang)
    sin = jnp.sin(ang)
    cos_h = jnp.concatenate([cos, cos], axis=1)
    sin_h = jnp.concatenate([-sin, sin], axis=1)
    w = norm_w.astype(F32)
    w_partner = jnp.concatenate([w[half:], w[:half]])
    ta = cos_h * (w * scale)[None, :]
    tb = sin_h * (w_partner * scale)[None, :]
    reps = LANES // ATTN_HEAD_DIM
    return jnp.tile(ta, (1, reps)), jnp.tile(tb, (1, reps))


def kernel(x, ev_norm_w, ev_w_in, ev_conv_w, ev_w_out, od_norm_w, od_w_in,
           od_q_norm_w, od_k_norm_w, od_sinks, od_w_out):
    batch, seq, d = x.shape
    assert d == D_MODEL and seq % IN_TM == 0 and seq % OUT_TM == 0
    x2d = x.reshape(batch * seq, d)

    ang = _rope_angles(seq, RET_HEAD_DIM)
    cw = jnp.pad(ev_conv_w[0], ((0, 8 - CONV_K), (0, 0)))
    qkvg, conv = _even_in(x2d, ev_norm_w[0][None, :], ev_w_in[0].astype(BF16),
                          jnp.cos(ang), jnp.sin(ang), cw, seq)
    dmat, qdec, kdec, cdec = _retention_tables()
    x2d = _even_out(qkvg, conv, x2d, ev_w_out[0].astype(BF16), dmat, qdec, kdec, cdec, batch, seq)

    w_in = od_w_in[0]

    def dup_heads(wc):
        wc = wc.reshape(d, ATTN_KV_HEADS, 1, ATTN_HEAD_DIM)
        return jnp.broadcast_to(wc, (d, ATTN_KV_HEADS, 2, ATTN_HEAD_DIM)).reshape(d, 2 * KV_WIDTH)

    w_kv = jnp.concatenate([
        dup_heads(w_in[:, ATTN_WIDTH:ATTN_WIDTH + KV_WIDTH]),
        dup_heads(w_in[:, ATTN_WIDTH + KV_WIDTH:ATTN_WIDTH + 2 * KV_WIDTH]),
    ], axis=1).astype(BF16)
    qa, qb = _head_rope_tables(od_q_norm_w[0], seq, ATTN_HEAD_DIM ** -0.5)
    ka, kb = _head_rope_tables(od_k_norm_w[0], seq, 1.0)
    head_id = np.arange(MXU_N) // ATTN_HEAD_DIM
    ones_bd = jnp.asarray(head_id[:, None] == head_id[None, :], BF16)
    qgkv = _odd_in(x2d, od_norm_w[0][None, :], w_in.astype(BF16), w_kv,
                   qa, qb, ka, kb, ones_bd, seq)
    x2d = _odd_out(od_sinks[0].astype(F32), qgkv, x2d, od_w_out[0].astype(BF16), batch, seq)
    return x2d.reshape(batch, seq, d)
```

```python
import functools

import numpy as np
import jax
import jax.numpy as jnp
from jax import lax
from jax.experimental import pallas as pl
from jax.experimental.pallas import tpu as pltpu

D_MODEL = 2048
RET_HEADS = 4
RET_HEAD_DIM = 256
RET_WIDTH = RET_HEADS * RET_HEAD_DIM
CONV_WIDTH = D_MODEL - RET_WIDTH
CONV_K = 3
RET_CHUNK = 128
EVEN_IN = 4 * RET_WIDTH + 4 * CONV_WIDTH
ATTN_HEAD_DIM = 64
ATTN_Q_HEADS = 32
ATTN_KV_HEADS = 4
ATTN_GROUP = ATTN_Q_HEADS // ATTN_KV_HEADS
ATTN_WIDTH = ATTN_Q_HEADS * ATTN_HEAD_DIM
KV_WIDTH = ATTN_KV_HEADS * ATTN_HEAD_DIM
BLOCK = 128
ROPE_THETA = 10000.0
EPS = 1e-6
NEG = -1e30

LANES = 128
MXU_N = 256
SUB_N = 512
IN_TM = 1024
IN_TN = 1024
OUT_TM = 512
CONV_TILE = 256
HALO = 8
VMEM_LIMIT = 56 * 1024 * 1024

F32 = jnp.float32
BF16 = jnp.bfloat16


def _dot(a, b):
    return jnp.dot(a, b, preferred_element_type=F32)


def _dot_nt(a, b):
    return lax.dot_general(a, b, (((1,), (1,)), ((), ())), preferred_element_type=F32)


def _dot_tn(a, b):
    return lax.dot_general(a, b, (((0,), (0,)), ((), ())), preferred_element_type=F32)


def _silu(g):
    return (0.5 * g) * (1.0 + jnp.tanh(0.5 * g))


def _rms_rows(x, nw):
    ms = jnp.mean(x * x, axis=-1, keepdims=True)
    return x * lax.rsqrt(ms + EPS) * nw


def _sub_cols(s):
    return slice(s * SUB_N, (s + 1) * SUB_N)


def _even_in_kernel(x_ref, nw_ref, wa_ref, wb_ref, wgc_ref, wu_ref, wgb_ref, wgs_ref,
                    cos_ref, sin_ref, cw_ref,
                    qkvg_ref, conv_ref, h_scr, cu_scr, tail_scr, *, seq_tiles):
    i = pl.program_id(0)
    j = pl.program_id(1)
    tm = x_ref.shape[0]
    w_refs = (wa_ref, wb_ref)

    @pl.when(j == 0)
    def _():
        h_scr[...] = _rms_rows(x_ref[...], nw_ref[...]).astype(BF16)

    def sub_dot(s):
        return _dot(h_scr[...], w_refs[s][...])

    def pair_dot(left_ref, right_ref):
        return _dot(h_scr[...], jnp.concatenate([left_ref[...], right_ref[...]], axis=1))

    @pl.when(j < 2)
    def _():
        cos = cos_ref[...]
        sin = sin_ref[...]
        for s in range(len(w_refs)):
            acc = sub_dot(s)
            for hd in range(SUB_N // RET_HEAD_DIM):
                base = s * SUB_N + hd * RET_HEAD_DIM
                x1 = acc[:, hd * RET_HEAD_DIM:hd * RET_HEAD_DIM + LANES]
                x2 = acc[:, hd * RET_HEAD_DIM + LANES:(hd + 1) * RET_HEAD_DIM]
                qkvg_ref[:, base:base + LANES] = (x1 * cos - x2 * sin).astype(BF16)
                qkvg_ref[:, base + LANES:base + RET_HEAD_DIM] = (x2 * cos + x1 * sin).astype(BF16)

    @pl.when(j == 2)
    def _():
        for s in range(len(w_refs)):
            qkvg_ref[:, _sub_cols(s)] = sub_dot(s).astype(BF16)

    @pl.when(j == 3)
    def _():
        for s in range(len(w_refs)):
            qkvg_ref[:, _sub_cols(s)] = _silu(sub_dot(s)).astype(BF16)

    @pl.when(j >= 4)
    def _():
        cs = j - 4
        cu_pair = pair_dot(wgc_ref, wu_ref)
        cu = cu_pair[:, :CONV_TILE] * cu_pair[:, CONV_TILE:]
        at_seq_start = (i % seq_tiles) == 0

        halo_rows = slice(HALO - (CONV_K - 1), HALO)

        @pl.when(at_seq_start)
        def _():
            cu_scr[halo_rows, :] = jnp.zeros((CONV_K - 1, CONV_TILE), F32)

        @pl.when(jnp.logical_not(at_seq_start))
        def _():
            cu_scr[halo_rows, :] = tail_scr[cs, halo_rows, :]

        cu_scr[HALO:HALO + tm, :] = cu
        tail_scr[cs, halo_rows, :] = cu_scr[HALO + tm - (CONV_K - 1):HALO + tm, :]
        cw = cw_ref[...]
        conv = cw[CONV_K - 1:CONV_K, :] * cu
        for t in range(CONV_K - 1):
            shift = CONV_K - 1 - t
            conv = conv + cw[t:t + 1, :] * cu_scr[HALO - shift:HALO - shift + tm, :]
        gates = pair_dot(wgb_ref, wgs_ref)
        conv_ref[...] = (gates[:, :CONV_TILE] * conv * _silu(gates[:, CONV_TILE:])).astype(BF16)


def _even_in(x2d, nw, w, cos, sin, cw, seq):
    t = x2d.shape[0]
    tm = IN_TM
    seq_tiles = seq // tm
    n_ret_steps = 4 * RET_WIDTH // IN_TN
    n_conv_tiles = CONV_WIDTH // CONV_TILE
    nj = n_ret_steps + n_conv_tiles
    sub_per_step = IN_TN // SUB_N
    conv_col0 = 4 * RET_WIDTH // CONV_TILE

    def wide_spec(s):
        return pl.BlockSpec(
            (D_MODEL, SUB_N),
            lambda i, j: (0, sub_per_step * jnp.minimum(j, n_ret_steps - 1) + s))

    def conv_w_spec(branch):
        return pl.BlockSpec(
            (D_MODEL, CONV_TILE),
            lambda i, j: (0, conv_col0 + branch * n_conv_tiles + jnp.maximum(j - n_ret_steps, 0)))

    kern = functools.partial(_even_in_kernel, seq_tiles=seq_tiles)
    return pl.pallas_call(
        kern,
        grid=(t // tm, nj),
        in_specs=[
            pl.BlockSpec((tm, D_MODEL), lambda i, j: (i, 0)),
            pl.BlockSpec((1, D_MODEL), lambda i, j: (0, 0), pipeline_mode=pl.Buffered(1)),
            wide_spec(0), wide_spec(1),
            conv_w_spec(1), conv_w_spec(2), conv_w_spec(0), conv_w_spec(3),
            pl.BlockSpec((tm, LANES), lambda i, j: (i % seq_tiles, 0)),
            pl.BlockSpec((tm, LANES), lambda i, j: (i % seq_tiles, 0)),
            pl.BlockSpec((8, CONV_TILE), lambda i, j: (0, jnp.maximum(j - n_ret_steps, 0))),
        ],
        out_specs=[
            pl.BlockSpec((tm, IN_TN), lambda i, j: (i, jnp.minimum(j, n_ret_steps - 1))),
            pl.BlockSpec((tm, CONV_TILE), lambda i, j: (i, jnp.maximum(j - n_ret_steps, 0))),
        ],
        out_shape=[
            jax.ShapeDtypeStruct((t, 4 * RET_WIDTH), BF16),
            jax.ShapeDtypeStruct((t, CONV_WIDTH), BF16),
        ],
        scratch_shapes=[
            pltpu.VMEM((tm, D_MODEL), BF16),
            pltpu.VMEM((HALO + tm, CONV_TILE), F32),
            pltpu.VMEM((n_conv_tiles, HALO, CONV_TILE), F32),
        ],
        compiler_params=pltpu.CompilerParams(
            dimension_semantics=("arbitrary", "arbitrary"),
            vmem_limit_bytes=VMEM_LIMIT),
        name="even_in",
    )(x2d, nw, w, w, w, w, w, w, cos, sin, cw)


def _even_out_kernel(q_ref, k_ref, v_ref, g_ref, conv_ref, x_ref, w_ref,
                     dmat_ref, qdec_ref, kdec_ref, cdec_ref, nw_next_ref,
                     o_ref, h_next_ref, state_scr, y_scr):
    tm = x_ref.shape[0]

    @pl.when(pl.program_id(1) == 0)
    def _():
        state_scr[...] = jnp.zeros_like(state_scr)

    for c in range(tm // RET_CHUNK):
        rows = slice(c * RET_CHUNK, (c + 1) * RET_CHUNK)
        for h in range(RET_HEADS):
            cols = slice(h * RET_HEAD_DIM, (h + 1) * RET_HEAD_DIM)
            qh = q_ref[rows, cols]
            kh = k_ref[rows, cols]
            vh = v_ref[rows, cols]
            scores = _dot_nt(qh, kh) * dmat_ref[h]
            inner = _dot(scores.astype(BF16), vh)
            state = state_scr[h]
            cross = _dot(qh, state.astype(BF16)) * qdec_ref[h]
            o = inner + cross
            kd = (kh.astype(F32) * kdec_ref[h]).astype(BF16)
            state_scr[h] = cdec_ref[h] * state + _dot_tn(kd, vh)
            ms = jnp.mean(o * o, axis=-1, keepdims=True)
            ro = o * lax.rsqrt(ms + EPS) * g_ref[rows, cols].astype(F32)
            y_scr[rows, cols] = ro.astype(BF16)
    y_scr[:, RET_WIDTH:] = conv_ref[...]

    ssq = jnp.zeros((tm, 1), F32)
    for n in range(D_MODEL // SUB_N):
        xn = x_ref[:, _sub_cols(n)] + _dot(y_scr[...], w_ref[:, _sub_cols(n)])
        o_ref[:, _sub_cols(n)] = xn
        ssq = ssq + jnp.sum(xn * xn, axis=-1, keepdims=True)
    r = lax.rsqrt(ssq * (1.0 / D_MODEL) + EPS)
    for n in range(D_MODEL // SUB_N):
        h_next_ref[:, _sub_cols(n)] = (o_ref[:, _sub_cols(n)] * r * nw_next_ref[:, _sub_cols(n)]).astype(BF16)


def _even_out(qkvg, conv, x2d, w_out, dmat, qdec, kdec, cdec, nw_next, batch, seq):
    t = x2d.shape[0]
    tm = OUT_TM
    ns = seq // tm

    def row(b, i):
        return b * ns + i

    def qkvg_spec(col):
        return pl.BlockSpec((tm, RET_WIDTH), lambda b, i: (row(b, i), col))

    def const_spec(shape):
        return pl.BlockSpec(shape, lambda b, i: (0,) * len(shape), pipeline_mode=pl.Buffered(1))

    return pl.pallas_call(
        _even_out_kernel,
        grid=(batch, ns),
        in_specs=[
            qkvg_spec(0), qkvg_spec(1), qkvg_spec(2), qkvg_spec(3),
            pl.BlockSpec((tm, CONV_WIDTH), lambda b, i: (row(b, i), 0)),
            pl.BlockSpec((tm, D_MODEL), lambda b, i: (row(b, i), 0)),
            const_spec((D_MODEL, D_MODEL)),
            const_spec(dmat.shape), const_spec(qdec.shape),
            const_spec(kdec.shape), const_spec(cdec.shape), const_spec(nw_next.shape),
        ],
        out_specs=[
            pl.BlockSpec((tm, D_MODEL), lambda b, i: (row(b, i), 0)),
            pl.BlockSpec((tm, D_MODEL), lambda b, i: (row(b, i), 0)),
        ],
        out_shape=[
            jax.ShapeDtypeStruct((t, D_MODEL), F32),
            jax.ShapeDtypeStruct((t, D_MODEL), BF16),
        ],
        scratch_shapes=[
            pltpu.VMEM((RET_HEADS, RET_HEAD_DIM, RET_HEAD_DIM), F32),
            pltpu.VMEM((tm, D_MODEL), BF16),
        ],
        compiler_params=pltpu.CompilerParams(
            dimension_semantics=("arbitrary", "arbitrary"),
            vmem_limit_bytes=VMEM_LIMIT),
        name="even_out",
    )(qkvg, qkvg, qkvg, qkvg, conv, x2d, w_out, dmat, qdec, kdec, cdec, nw_next)


def _odd_in_kernel(h_ref, wa_ref, wb_ref, wkv_ref,
                   qa_ref, qb_ref, ka_ref, kb_ref, ones_ref,
                   out_ref, *, q_steps):
    j = pl.program_id(1)
    w_refs = (wa_ref, wb_ref)

    def sub_dot(w_ref):
        return _dot(h_ref[...], w_ref[...])

    def dup_heads(x):
        lane = lax.broadcasted_iota(jnp.int32, (x.shape[0], LANES), 1)
        low = lane < ATTN_HEAD_DIM
        blocks = []
        for b in range(x.shape[1] // LANES):
            xb = x[:, b * LANES:(b + 1) * LANES]
            swapped = pltpu.roll(xb, ATTN_HEAD_DIM, axis=1)
            blocks += [jnp.where(low, xb, swapped), jnp.where(low, swapped, xb)]
        return jnp.concatenate(blocks, axis=1)

    def norm_rope(acc, ta, tb):
        width = acc.shape[1]
        sq = (acc * acc).astype(BF16)
        ss = jnp.concatenate(
            [_dot(sq[:, c * MXU_N:(c + 1) * MXU_N], ones_ref[...]) for c in range(width // MXU_N)],
            axis=1)
        r = lax.rsqrt(ss * (1.0 / ATTN_HEAD_DIM) + EPS)
        half = ATTN_HEAD_DIM // 2
        lane = lax.broadcasted_iota(jnp.int32, acc.shape, 1)
        first_half = (lane % ATTN_HEAD_DIM) < half
        partner = jnp.where(first_half,
                            pltpu.roll(acc, width - half, axis=1),
                            pltpu.roll(acc, half, axis=1))
        reps = width // LANES
        ta_w = jnp.concatenate([ta] * reps, axis=1)
        tb_w = jnp.concatenate([tb] * reps, axis=1)
        return r * (acc * ta_w + partner * tb_w)

    @pl.when(j < q_steps)
    def _():
        for s in range(len(w_refs)):
            out_ref[:, _sub_cols(s)] = norm_rope(sub_dot(w_refs[s]), qa_ref[...], qb_ref[...]).astype(BF16)

    @pl.when(jnp.logical_and(j >= q_steps, j < 2 * q_steps))
    def _():
        for s in range(len(w_refs)):
            out_ref[:, _sub_cols(s)] = _silu(sub_dot(w_refs[s])).astype(BF16)

    @pl.when(j == 2 * q_steps)
    def _():
        acc = sub_dot(wkv_ref)
        k = norm_rope(acc[:, :KV_WIDTH], ka_ref[...], kb_ref[...])
        out_ref[:, :2 * KV_WIDTH] = dup_heads(k).astype(BF16)
        out_ref[:, 2 * KV_WIDTH:] = dup_heads(acc[:, KV_WIDTH:]).astype(BF16)


def _odd_in(h2d, w, qa, qb, ka, kb, ones_bd, seq):
    t = h2d.shape[0]
    tm = IN_TM
    seq_tiles = seq // tm
    q_steps = ATTN_WIDTH // IN_TN
    nj = 2 * q_steps + 1
    gate_col0 = (ATTN_WIDTH + 2 * KV_WIDTH) // SUB_N
    sub_per_step = IN_TN // SUB_N

    def wide_spec(s):
        def index(i, j):
            q_idx = sub_per_step * j + s
            g_idx = gate_col0 + sub_per_step * (jnp.minimum(j, 2 * q_steps - 1) - q_steps) + s
            return 0, jnp.where(j < q_steps, q_idx, g_idx)
        return pl.BlockSpec((D_MODEL, SUB_N), index)

    def tab_spec():
        return pl.BlockSpec((tm, LANES), lambda i, j: (i % seq_tiles, 0))

    def resident_spec(shape, col):
        return pl.BlockSpec(shape, lambda i, j: (0, col), pipeline_mode=pl.Buffered(1))

    kern = functools.partial(_odd_in_kernel, q_steps=q_steps)
    return pl.pallas_call(
        kern,
        grid=(t // tm, nj),
        in_specs=[
            pl.BlockSpec((tm, D_MODEL), lambda i, j: (i, 0)),
            wide_spec(0), wide_spec(1),
            resident_spec((D_MODEL, SUB_N), ATTN_WIDTH // SUB_N),
            tab_spec(), tab_spec(), tab_spec(), tab_spec(),
            resident_spec((MXU_N, MXU_N), 0),
        ],
        out_specs=pl.BlockSpec((tm, IN_TN), lambda i, j: (i, j)),
        out_shape=jax.ShapeDtypeStruct((t, nj * IN_TN), BF16),
        compiler_params=pltpu.CompilerParams(
            dimension_semantics=("arbitrary", "arbitrary"),
            vmem_limit_bytes=VMEM_LIMIT),
        name="odd_in",
    )(h2d, w, w, w, qa, qb, ka, kb, ones_bd)


def _odd_out_kernel(sink_ref, q_ref, kv_ref, kvp_ref, g_ref, x_ref, w_ref,
                    o_ref, y_scr):
    tm = x_ref.shape[0]
    at_seq_start = pl.program_id(1) == 0
    pairs = ATTN_GROUP // 2
    half_w = 2 * KV_WIDTH

    lane_kv = lax.broadcasted_iota(jnp.int32, (2 * BLOCK, LANES), 1)
    low_kv = lane_kv < ATTN_HEAD_DIM
    rr = lax.broadcasted_iota(jnp.int32, (BLOCK, BLOCK), 0)
    cc = lax.broadcasted_iota(jnp.int32, (BLOCK, BLOCK), 1)
    tri = cc <= rr
    low_o = cc < ATTN_HEAD_DIM

    for bl in range(tm // BLOCK):
        rows = slice(bl * BLOCK, (bl + 1) * BLOCK)
        for c in range(ATTN_KV_HEADS):
            kcols = slice(c * LANES, (c + 1) * LANES)
            vcols = slice(half_w + c * LANES, half_w + (c + 1) * LANES)
            if bl == 0:
                k_prev = kvp_ref[:, kcols]
                v_prev = kvp_ref[:, vcols]
            else:
                prows = slice((bl - 1) * BLOCK, bl * BLOCK)
                k_prev = kv_ref[prows, kcols]
                v_prev = kv_ref[prows, vcols]
            kk = jnp.concatenate([k_prev, kv_ref[rows, kcols]], axis=0)
            vv = jnp.concatenate([v_prev, kv_ref[rows, vcols]], axis=0)
            zero = jnp.zeros_like(kk)
            k_par = (jnp.where(low_kv, kk, zero), jnp.where(low_kv, zero, kk))
            qs = jnp.concatenate(
                [q_ref[rows, (pairs * c + a) * LANES:(pairs * c + a + 1) * LANES]
                 for a in range(pairs)], axis=0)
            probs = []
            denoms = []
            for par in range(2):
                s = _dot_nt(qs, k_par[par])
                for a in range(pairs):
                    sa = s[a * BLOCK:(a + 1) * BLOCK]
                    s_prev = sa[:, :BLOCK]
                    if bl == 0:
                        s_prev = jnp.where(at_seq_start, NEG, s_prev)
                    f = jnp.where(tri, sa[:, BLOCK:], s_prev)
                    sink = sink_ref[ATTN_GROUP * c + 2 * a + par]
                    m = jnp.maximum(jnp.max(f, axis=1, keepdims=True), sink)
                    p = jnp.exp(f - m)
                    denoms.append(jnp.sum(p, axis=1, keepdims=True) + jnp.exp(sink - m))
                    probs.append(jnp.concatenate(
                        [jnp.where(tri, 0.0, p), jnp.where(tri, p, 0.0)], axis=1).astype(BF16))
            o = _dot(jnp.concatenate(probs, axis=0), vv)
            for a in range(pairs):
                o_even = o[a * BLOCK:(a + 1) * BLOCK] / denoms[a]
                o_odd = o[(pairs + a) * BLOCK:(pairs + a + 1) * BLOCK] / denoms[pairs + a]
                ocols = slice((pairs * c + a) * LANES, (pairs * c + a + 1) * LANES)
                gated = jnp.where(low_o, o_even, o_odd) * g_ref[rows, ocols].astype(F32)
                y_scr[rows, ocols] = gated.astype(BF16)

    for n in range(D_MODEL // SUB_N):
        o_ref[:, _sub_cols(n)] = x_ref[:, _sub_cols(n)] + _dot(y_scr[...], w_ref[:, _sub_cols(n)])


def _odd_out(sinks, qgkv, x2d, w_out, batch, seq):
    t = x2d.shape[0]
    tm = OUT_TM
    ns = seq // tm
    bpt = tm // BLOCK
    kv_col = 2 * ATTN_WIDTH // (4 * KV_WIDTH)

    def row(b, i):
        return b * ns + i

    return pl.pallas_call(
        _odd_out_kernel,
        grid=(batch, ns),
        in_specs=[
            pl.BlockSpec(memory_space=pltpu.SMEM),
            pl.BlockSpec((tm, ATTN_WIDTH), lambda b, i: (row(b, i), 0)),
            pl.BlockSpec((tm, 4 * KV_WIDTH), lambda b, i: (row(b, i), kv_col)),
            pl.BlockSpec((BLOCK, 4 * KV_WIDTH),
                         lambda b, i: (jnp.maximum(row(b, i) * bpt - 1, 0), kv_col)),
            pl.BlockSpec((tm, ATTN_WIDTH), lambda b, i: (row(b, i), 1)),
            pl.BlockSpec((tm, D_MODEL), lambda b, i: (row(b, i), 0)),
            pl.BlockSpec((ATTN_WIDTH, D_MODEL), lambda b, i: (0, 0), pipeline_mode=pl.Buffered(1)),
        ],
        out_specs=pl.BlockSpec((tm, D_MODEL), lambda b, i: (row(b, i), 0)),
        out_shape=jax.ShapeDtypeStruct((t, D_MODEL), F32),
        scratch_shapes=[pltpu.VMEM((tm, ATTN_WIDTH), BF16)],
        compiler_params=pltpu.CompilerParams(
            dimension_semantics=("arbitrary", "arbitrary"),
            vmem_limit_bytes=VMEM_LIMIT),
        name="odd_out",
    )(sinks, qgkv, qgkv, qgkv, qgkv, x2d, w_out)


def _rope_angles(seq, dim):
    inv = 1.0 / (ROPE_THETA ** (jnp.arange(0, dim, 2, dtype=F32) / dim))
    return jnp.arange(seq).astype(F32)[:, None] * inv[None, :]


def _retention_tables():
    h = np.arange(RET_HEADS, dtype=np.float64)
    log_g = np.log(1.0 - 2.0 ** (-5.0 - h))
    idx = np.arange(RET_CHUNK, dtype=np.float64)
    diff = idx[:, None] - idx[None, :]
    scale = RET_HEAD_DIM ** -0.5
    intra = np.where(diff >= 0, np.exp(log_g[:, None, None] * np.maximum(diff, 0.0)), 0.0)
    q_dec = np.exp(log_g[:, None] * (idx[None, :] + 1.0))
    k_dec = np.exp(log_g[:, None] * (RET_CHUNK - 1.0 - idx[None, :]))
    c_dec = np.exp(log_g * RET_CHUNK)
    wide = (RET_HEADS, RET_CHUNK, RET_HEAD_DIM)
    dmat = jnp.asarray(intra * scale, F32)
    qdec = jnp.asarray(np.broadcast_to(q_dec[:, :, None], wide), F32)
    kdec = jnp.asarray(np.broadcast_to(k_dec[:, :, None] * scale, wide), F32)
    cdec = jnp.asarray(np.broadcast_to(c_dec[:, None, None], (RET_HEADS, 1, RET_HEAD_DIM)), F32)
    return dmat, qdec, kdec, cdec


def _head_rope_tables(norm_w, seq, scale):
    half = ATTN_HEAD_DIM // 2
    ang = _rope_angles(seq, ATTN_HEAD_DIM)
    cos = jnp.cos(ang)
    sin = jnp.sin(ang)
    cos_h = jnp.concatenate([cos, cos], axis=1)
    sin_h = jnp.concatenate([-sin, sin], axis=1)
    w = norm_w.astype(F32)
    w_partner = jnp.concatenate([w[half:], w[:half]])
    ta = cos_h * (w * scale)[None, :]
    tb = sin_h * (w_partner * scale)[None, :]
    reps = LANES // ATTN_HEAD_DIM
    return jnp.tile(ta, (1, reps)), jnp.tile(tb, (1, reps))


def kernel(x, ev_norm_w, ev_w_in, ev_conv_w, ev_w_out, od_norm_w, od_w_in,
           od_q_norm_w, od_k_norm_w, od_sinks, od_w_out):
    batch, seq, d = x.shape
    assert d == D_MODEL and seq % IN_TM == 0 and seq % OUT_TM == 0
    x2d = x.reshape(batch * seq, d)

    ang = _rope_angles(seq, RET_HEAD_DIM)
    cw = jnp.pad(ev_conv_w[0], ((0, 8 - CONV_K), (0, 0)))
    qkvg, conv = _even_in(x2d, ev_norm_w[0][None, :], ev_w_in[0].astype(BF16),
                          jnp.cos(ang), jnp.sin(ang), cw, seq)
    dmat, qdec, kdec, cdec = _retention_tables()
    x2d, h2d = _even_out(qkvg, conv, x2d, ev_w_out[0].astype(BF16), dmat, qdec, kdec, cdec,
                         od_norm_w[0][None, :], batch, seq)

    qa, qb = _head_rope_tables(od_q_norm_w[0], seq, ATTN_HEAD_DIM ** -0.5)
    ka, kb = _head_rope_tables(od_k_norm_w[0], seq, 1.0)
    head_id = np.arange(MXU_N) // ATTN_HEAD_DIM
    ones_bd = jnp.asarray(head_id[:, None] == head_id[None, :], BF16)
    qgkv = _odd_in(h2d, od_w_in[0].astype(BF16), qa, qb, ka, kb, ones_bd, seq)
    x2d = _odd_out(od_sinks[0].astype(F32), qgkv, x2d, od_w_out[0].astype(BF16), batch, seq)
    return x2d.reshape(batch, seq, d)
```

```python
import functools

import numpy as np
import jax
import jax.numpy as jnp
from jax import lax
from jax.experimental import pallas as pl
from jax.experimental.pallas import tpu as pltpu

D_MODEL = 2048
RET_HEADS = 4
RET_HEAD_DIM = 256
RET_WIDTH = RET_HEADS * RET_HEAD_DIM
CONV_WIDTH = D_MODEL - RET_WIDTH
CONV_K = 3
RET_CHUNK = 256
EVEN_IN = 4 * RET_WIDTH + 4 * CONV_WIDTH
ATTN_HEAD_DIM = 64
ATTN_Q_HEADS = 32
ATTN_KV_HEADS = 4
ATTN_GROUP = ATTN_Q_HEADS // ATTN_KV_HEADS
ATTN_WIDTH = ATTN_Q_HEADS * ATTN_HEAD_DIM
KV_WIDTH = ATTN_KV_HEADS * ATTN_HEAD_DIM
BLOCK = 128
ROPE_THETA = 10000.0
EPS = 1e-6
NEG = -1e30

LANES = 128
MXU_N = 256
SUB_N = 512
IN_TM = 1024
IN_TN = 1024
OUT_TM = 512
CONV_TILE = 256
HALO = 8
VMEM_LIMIT = 56 * 1024 * 1024

F32 = jnp.float32
BF16 = jnp.bfloat16


def _dot(a, b):
    return jnp.dot(a, b, preferred_element_type=F32)


def _dot_nt(a, b):
    return lax.dot_general(a, b, (((1,), (1,)), ((), ())), preferred_element_type=F32)


def _dot_tn(a, b):
    return lax.dot_general(a, b, (((0,), (0,)), ((), ())), preferred_element_type=F32)


def _silu(g):
    return (0.5 * g) * (1.0 + jnp.tanh(0.5 * g))


def _rms_rows(x, nw):
    ms = jnp.mean(x * x, axis=-1, keepdims=True)
    return x * lax.rsqrt(ms + EPS) * nw


def _sub_cols(s):
    return slice(s * SUB_N, (s + 1) * SUB_N)


def _even_in_kernel(x_ref, nw_ref, wa_ref, wb_ref, wgc_ref, wu_ref, wgb_ref, wgs_ref,
                    cos_ref, sin_ref, cw_ref, c0_ref, c1_ref, c2_ref,
                    qkvg_ref, conv_ref, c0_out, c1_out, c2_out,
                    h_scr, cu_scr, tail_scr, *, seq_tiles):
    i = pl.program_id(0)
    j = pl.program_id(1)
    tm = x_ref.shape[0]
    w_refs = (wa_ref, wb_ref)

    for src, dst in ((c0_ref, c0_out), (c1_ref, c1_out), (c2_ref, c2_out)):
        dst[...] = src[...].astype(BF16)

    @pl.when(j == 0)
    def _():
        h_scr[...] = _rms_rows(x_ref[...], nw_ref[...]).astype(BF16)

    def sub_dot(s):
        return _dot(h_scr[...], w_refs[s][...])

    def pair_dot(left_ref, right_ref):
        return _dot(h_scr[...], jnp.concatenate([left_ref[...], right_ref[...]], axis=1))

    @pl.when(j < 2)
    def _():
        cos = cos_ref[...]
        sin = sin_ref[...]
        for s in range(len(w_refs)):
            acc = sub_dot(s)
            for hd in range(SUB_N // RET_HEAD_DIM):
                base = s * SUB_N + hd * RET_HEAD_DIM
                x1 = acc[:, hd * RET_HEAD_DIM:hd * RET_HEAD_DIM + LANES]
                x2 = acc[:, hd * RET_HEAD_DIM + LANES:(hd + 1) * RET_HEAD_DIM]
                qkvg_ref[:, base:base + LANES] = (x1 * cos - x2 * sin).astype(BF16)
                qkvg_ref[:, base + LANES:base + RET_HEAD_DIM] = (x2 * cos + x1 * sin).astype(BF16)

    @pl.when(j == 2)
    def _():
        for s in range(len(w_refs)):
            qkvg_ref[:, _sub_cols(s)] = sub_dot(s).astype(BF16)

    @pl.when(j == 3)
    def _():
        for s in range(len(w_refs)):
            qkvg_ref[:, _sub_cols(s)] = _silu(sub_dot(s)).astype(BF16)

    @pl.when(j >= 4)
    def _():
        cs = j - 4
        cu_pair = pair_dot(wgc_ref, wu_ref)
        cu = cu_pair[:, :CONV_TILE] * cu_pair[:, CONV_TILE:]
        at_seq_start = (i % seq_tiles) == 0

        halo_rows = slice(HALO - (CONV_K - 1), HALO)

        @pl.when(at_seq_start)
        def _():
            cu_scr[halo_rows, :] = jnp.zeros((CONV_K - 1, CONV_TILE), F32)

        @pl.when(jnp.logical_not(at_seq_start))
        def _():
            cu_scr[halo_rows, :] = tail_scr[cs, halo_rows, :]

        cu_scr[HALO:HALO + tm, :] = cu
        tail_scr[cs, halo_rows, :] = cu_scr[HALO + tm - (CONV_K - 1):HALO + tm, :]
        cw = cw_ref[...]
        conv = cw[CONV_K - 1:CONV_K, :] * cu
        for t in range(CONV_K - 1):
            shift = CONV_K - 1 - t
            conv = conv + cw[t:t + 1, :] * cu_scr[HALO - shift:HALO - shift + tm, :]
        gates = pair_dot(wgb_ref, wgs_ref)
        conv_ref[...] = (gates[:, :CONV_TILE] * conv * _silu(gates[:, CONV_TILE:])).astype(BF16)


def _even_in(x2d, nw, w, cos, sin, cw, to_cast, seq):
    t = x2d.shape[0]
    tm = IN_TM
    seq_tiles = seq // tm
    n_ret_steps = 4 * RET_WIDTH // IN_TN
    n_conv_tiles = CONV_WIDTH // CONV_TILE
    nj = n_ret_steps + n_conv_tiles
    sub_per_step = IN_TN // SUB_N
    conv_col0 = 4 * RET_WIDTH // CONV_TILE
    n_steps = (t // tm) * nj

    def cast_spec(a):
        rows = a.shape[0] // n_steps
        assert rows * n_steps == a.shape[0] and rows % 16 == 0
        return pl.BlockSpec((rows, a.shape[1]), lambda i, j: (i * nj + j, 0))

    def wide_spec(s):
        return pl.BlockSpec(
            (D_MODEL, SUB_N),
            lambda i, j: (0, sub_per_step * jnp.minimum(j, n_ret_steps - 1) + s))

    def conv_w_spec(branch):
        return pl.BlockSpec(
            (D_MODEL, CONV_TILE),
            lambda i, j: (0, conv_col0 + branch * n_conv_tiles + jnp.maximum(j - n_ret_steps, 0)))

    kern = functools.partial(_even_in_kernel, seq_tiles=seq_tiles)
    return pl.pallas_call(
        kern,
        grid=(t // tm, nj),
        in_specs=[
            pl.BlockSpec((tm, D_MODEL), lambda i, j: (i, 0)),
            pl.BlockSpec((1, D_MODEL), lambda i, j: (0, 0), pipeline_mode=pl.Buffered(1)),
            wide_spec(0), wide_spec(1),
            conv_w_spec(1), conv_w_spec(2), conv_w_spec(0), conv_w_spec(3),
            pl.BlockSpec((tm, LANES), lambda i, j: (i % seq_tiles, 0)),
            pl.BlockSpec((tm, LANES), lambda i, j: (i % seq_tiles, 0)),
            pl.BlockSpec((8, CONV_TILE), lambda i, j: (0, jnp.maximum(j - n_ret_steps, 0))),
        ] + [cast_spec(a) for a in to_cast],
        out_specs=[
            pl.BlockSpec((tm, IN_TN), lambda i, j: (i, jnp.minimum(j, n_ret_steps - 1))),
            pl.BlockSpec((tm, CONV_TILE), lambda i, j: (i, jnp.maximum(j - n_ret_steps, 0))),
        ] + [cast_spec(a) for a in to_cast],
        out_shape=[
            jax.ShapeDtypeStruct((t, 4 * RET_WIDTH), BF16),
            jax.ShapeDtypeStruct((t, CONV_WIDTH), BF16),
        ] + [jax.ShapeDtypeStruct(a.shape, BF16) for a in to_cast],
        scratch_shapes=[
            pltpu.VMEM((tm, D_MODEL), BF16),
            pltpu.VMEM((HALO + tm, CONV_TILE), F32),
            pltpu.VMEM((n_conv_tiles, HALO, CONV_TILE), F32),
        ],
        compiler_params=pltpu.CompilerParams(
            dimension_semantics=("arbitrary", "arbitrary"),
            vmem_limit_bytes=VMEM_LIMIT),
        name="even_in",
    )(x2d, nw, w, w, w, w, w, w, cos, sin, cw, *to_cast)


def _even_out_kernel(q_ref, k_ref, v_ref, g_ref, conv_ref, x_ref, w_ref,
                     dmat_ref, qdec_ref, kdec_ref, cdec_ref, nw_next_ref,
                     o_ref, h_next_ref, state_scr, y_scr):
    tm = x_ref.shape[0]

    @pl.when(pl.program_id(1) == 0)
    def _():
        state_scr[...] = jnp.zeros_like(state_scr)

    for c in range(tm // RET_CHUNK):
        rows = slice(c * RET_CHUNK, (c + 1) * RET_CHUNK)
        for h in range(RET_HEADS):
            cols = slice(h * RET_HEAD_DIM, (h + 1) * RET_HEAD_DIM)
            qh = q_ref[rows, cols]
            kh = k_ref[rows, cols]
            vh = v_ref[rows, cols]
            scores = _dot_nt(qh, kh) * dmat_ref[h]
            inner = _dot(scores.astype(BF16), vh)
            state = state_scr[h]
            cross = _dot(qh, state.astype(BF16)) * qdec_ref[h]
            o = inner + cross
            kd = (kh.astype(F32) * kdec_ref[h]).astype(BF16)
            state_scr[h] = cdec_ref[h] * state + _dot_tn(kd, vh)
            ms = jnp.mean(o * o, axis=-1, keepdims=True)
            ro = o * lax.rsqrt(ms + EPS) * g_ref[rows, cols].astype(F32)
            y_scr[rows, cols] = ro.astype(BF16)
    y_scr[:, RET_WIDTH:] = conv_ref[...]

    ssq = jnp.zeros((tm, 1), F32)
    for n in range(D_MODEL // SUB_N):
        xn = x_ref[:, _sub_cols(n)] + _dot(y_scr[...], w_ref[:, _sub_cols(n)])
        o_ref[:, _sub_cols(n)] = xn
        ssq = ssq + jnp.sum(xn * xn, axis=-1, keepdims=True)
    r = lax.rsqrt(ssq * (1.0 / D_MODEL) + EPS)
    for n in range(D_MODEL // SUB_N):
        h_next_ref[:, _sub_cols(n)] = (o_ref[:, _sub_cols(n)] * r * nw_next_ref[:, _sub_cols(n)]).astype(BF16)


def _even_out(qkvg, conv, x2d, w_out, dmat, qdec, kdec, cdec, nw_next, batch, seq):
    t = x2d.shape[0]
    tm = OUT_TM
    ns = seq // tm

    def row(b, i):
        return b * ns + i

    def qkvg_spec(col):
        return pl.BlockSpec((tm, RET_WIDTH), lambda b, i: (row(b, i), col))

    def const_spec(shape):
        return pl.BlockSpec(shape, lambda b, i: (0,) * len(shape), pipeline_mode=pl.Buffered(1))

    return pl.pallas_call(
        _even_out_kernel,
        grid=(batch, ns),
        in_specs=[
            qkvg_spec(0), qkvg_spec(1), qkvg_spec(2), qkvg_spec(3),
            pl.BlockSpec((tm, CONV_WIDTH), lambda b, i: (row(b, i), 0)),
            pl.BlockSpec((tm, D_MODEL), lambda b, i: (row(b, i), 0)),
            const_spec((D_MODEL, D_MODEL)),
            const_spec(dmat.shape), const_spec(qdec.shape),
            const_spec(kdec.shape), const_spec(cdec.shape), const_spec(nw_next.shape),
        ],
        out_specs=[
            pl.BlockSpec((tm, D_MODEL), lambda b, i: (row(b, i), 0)),
            pl.BlockSpec((tm, D_MODEL), lambda b, i: (row(b, i), 0)),
        ],
        out_shape=[
            jax.ShapeDtypeStruct((t, D_MODEL), F32),
            jax.ShapeDtypeStruct((t, D_MODEL), BF16),
        ],
        scratch_shapes=[
            pltpu.VMEM((RET_HEADS, RET_HEAD_DIM, RET_HEAD_DIM), F32),
            pltpu.VMEM((tm, D_MODEL), BF16),
        ],
        compiler_params=pltpu.CompilerParams(
            dimension_semantics=("arbitrary", "arbitrary"),
            vmem_limit_bytes=VMEM_LIMIT),
        name="even_out",
    )(qkvg, qkvg, qkvg, qkvg, conv, x2d, w_out, dmat, qdec, kdec, cdec, nw_next)


def _odd_in_kernel(h_ref, wa_ref, wb_ref, wkv_ref,
                   qa_ref, qb_ref, ka_ref, kb_ref, ones_ref,
                   out_ref, *, q_steps):
    j = pl.program_id(1)
    w_refs = (wa_ref, wb_ref)

    def sub_dot(w_ref):
        return _dot(h_ref[...], w_ref[...])

    def dup_heads(x):
        lane = lax.broadcasted_iota(jnp.int32, (x.shape[0], LANES), 1)
        low = lane < ATTN_HEAD_DIM
        blocks = []
        for b in range(x.shape[1] // LANES):
            xb = x[:, b * LANES:(b + 1) * LANES]
            swapped = pltpu.roll(xb, ATTN_HEAD_DIM, axis=1)
            blocks += [jnp.where(low, xb, swapped), jnp.where(low, swapped, xb)]
        return jnp.concatenate(blocks, axis=1)

    def norm_rope(acc, ta, tb):
        width = acc.shape[1]
        sq = (acc * acc).astype(BF16)
        ss = jnp.concatenate(
            [_dot(sq[:, c * MXU_N:(c + 1) * MXU_N], ones_ref[...]) for c in range(width // MXU_N)],
            axis=1)
        r = lax.rsqrt(ss * (1.0 / ATTN_HEAD_DIM) + EPS)
        half = ATTN_HEAD_DIM // 2
        lane = lax.broadcasted_iota(jnp.int32, acc.shape, 1)
        first_half = (lane % ATTN_HEAD_DIM) < half
        partner = jnp.where(first_half,
                            pltpu.roll(acc, width - half, axis=1),
                            pltpu.roll(acc, half, axis=1))
        reps = width // LANES
        ta_w = jnp.concatenate([ta] * reps, axis=1)
        tb_w = jnp.concatenate([tb] * reps, axis=1)
        return r * (acc * ta_w + partner * tb_w)

    @pl.when(j < q_steps)
    def _():
        for s in range(len(w_refs)):
            out_ref[:, _sub_cols(s)] = norm_rope(sub_dot(w_refs[s]), qa_ref[...], qb_ref[...]).astype(BF16)

    @pl.when(jnp.logical_and(j >= q_steps, j < 2 * q_steps))
    def _():
        for s in range(len(w_refs)):
            out_ref[:, _sub_cols(s)] = _silu(sub_dot(w_refs[s])).astype(BF16)

    @pl.when(j == 2 * q_steps)
    def _():
        acc = sub_dot(wkv_ref)
        k = norm_rope(acc[:, :KV_WIDTH], ka_ref[...], kb_ref[...])
        out_ref[:, :2 * KV_WIDTH] = dup_heads(k).astype(BF16)
        out_ref[:, 2 * KV_WIDTH:] = dup_heads(acc[:, KV_WIDTH:]).astype(BF16)


def _odd_in(h2d, w, qa, qb, ka, kb, ones_bd, seq):
    t = h2d.shape[0]
    tm = IN_TM
    seq_tiles = seq // tm
    q_steps = ATTN_WIDTH // IN_TN
    nj = 2 * q_steps + 1
    gate_col0 = (ATTN_WIDTH + 2 * KV_WIDTH) // SUB_N
    sub_per_step = IN_TN // SUB_N

    def wide_spec(s):
        def index(i, j):
            q_idx = sub_per_step * j + s
            g_idx = gate_col0 + sub_per_step * (jnp.minimum(j, 2 * q_steps - 1) - q_steps) + s
            return 0, jnp.where(j < q_steps, q_idx, g_idx)
        return pl.BlockSpec((D_MODEL, SUB_N), index)

    def tab_spec():
        return pl.BlockSpec((tm, LANES), lambda i, j: (i % seq_tiles, 0))

    def resident_spec(shape, col):
        return pl.BlockSpec(shape, lambda i, j: (0, col), pipeline_mode=pl.Buffered(1))

    kern = functools.partial(_odd_in_kernel, q_steps=q_steps)
    return pl.pallas_call(
        kern,
        grid=(t // tm, nj),
        in_specs=[
            pl.BlockSpec((tm, D_MODEL), lambda i, j: (i, 0)),
            wide_spec(0), wide_spec(1),
            resident_spec((D_MODEL, SUB_N), ATTN_WIDTH // SUB_N),
            tab_spec(), tab_spec(), tab_spec(), tab_spec(),
            resident_spec((MXU_N, MXU_N), 0),
        ],
        out_specs=pl.BlockSpec((tm, IN_TN), lambda i, j: (i, j)),
        out_shape=jax.ShapeDtypeStruct((t, nj * IN_TN), BF16),
        compiler_params=pltpu.CompilerParams(
            dimension_semantics=("arbitrary", "arbitrary"),
            vmem_limit_bytes=VMEM_LIMIT),
        name="odd_in",
    )(h2d, w, w, w, qa, qb, ka, kb, ones_bd)


def _odd_out_kernel(sink_ref, q_ref, kv_ref, kvp_ref, g_ref, x_ref, w_ref,
                    o_ref, y_scr):
    tm = x_ref.shape[0]
    at_seq_start = pl.program_id(1) == 0
    pairs = ATTN_GROUP // 2
    half_w = 2 * KV_WIDTH

    lane_kv = lax.broadcasted_iota(jnp.int32, (2 * BLOCK, LANES), 1)
    low_kv = lane_kv < ATTN_HEAD_DIM
    rr = lax.broadcasted_iota(jnp.int32, (BLOCK, BLOCK), 0)
    cc = lax.broadcasted_iota(jnp.int32, (BLOCK, BLOCK), 1)
    tri = cc <= rr
    low_o = cc < ATTN_HEAD_DIM

    for bl in range(tm // BLOCK):
        rows = slice(bl * BLOCK, (bl + 1) * BLOCK)
        for c in range(ATTN_KV_HEADS):
            kcols = slice(c * LANES, (c + 1) * LANES)
            vcols = slice(half_w + c * LANES, half_w + (c + 1) * LANES)
            if bl == 0:
                k_prev = kvp_ref[:, kcols]
                v_prev = kvp_ref[:, vcols]
            else:
                prows = slice((bl - 1) * BLOCK, bl * BLOCK)
                k_prev = kv_ref[prows, kcols]
                v_prev = kv_ref[prows, vcols]
            kk = jnp.concatenate([k_prev, kv_ref[rows, kcols]], axis=0)
            vv = jnp.concatenate([v_prev, kv_ref[rows, vcols]], axis=0)
            zero = jnp.zeros_like(kk)
            k_par = (jnp.where(low_kv, kk, zero), jnp.where(low_kv, zero, kk))
            qs = jnp.concatenate(
                [q_ref[rows, (pairs * c + a) * LANES:(pairs * c + a + 1) * LANES]
                 for a in range(pairs)], axis=0)
            probs = []
            denoms = []
            for par in range(2):
                s = _dot_nt(qs, k_par[par])
                for a in range(pairs):
                    sa = s[a * BLOCK:(a + 1) * BLOCK]
                    s_prev = sa[:, :BLOCK]
                    if bl == 0:
                        s_prev = jnp.where(at_seq_start, NEG, s_prev)
                    f = jnp.where(tri, sa[:, BLOCK:], s_prev)
                    sink = sink_ref[ATTN_GROUP * c + 2 * a + par]
                    m = jnp.maximum(jnp.max(f, axis=1, keepdims=True), sink)
                    p = jnp.exp(f - m)
                    denoms.append(jnp.sum(p, axis=1, keepdims=True) + jnp.exp(sink - m))
                    probs.append(jnp.concatenate(
                        [jnp.where(tri, 0.0, p), jnp.where(tri, p, 0.0)], axis=1).astype(BF16))
            o = _dot(jnp.concatenate(probs, axis=0), vv)
            for a in range(pairs):
                o_even = o[a * BLOCK:(a + 1) * BLOCK] / denoms[a]
                o_odd = o[(pairs + a) * BLOCK:(pairs + a + 1) * BLOCK] / denoms[pairs + a]
                ocols = slice((pairs * c + a) * LANES, (pairs * c + a + 1) * LANES)
                gated = jnp.where(low_o, o_even, o_odd) * g_ref[rows, ocols].astype(F32)
                y_scr[rows, ocols] = gated.astype(BF16)

    for n in range(D_MODEL // SUB_N):
        o_ref[:, _sub_cols(n)] = x_ref[:, _sub_cols(n)] + _dot(y_scr[...], w_ref[:, _sub_cols(n)])


def _odd_out(sinks, qgkv, x2d, w_out, batch, seq):
    t = x2d.shape[0]
    tm = OUT_TM
    ns = seq // tm
    bpt = tm // BLOCK
    kv_col = 2 * ATTN_WIDTH // (4 * KV_WIDTH)

    def row(b, i):
        return b * ns + i

    return pl.pallas_call(
        _odd_out_kernel,
        grid=(batch, ns),
        in_specs=[
            pl.BlockSpec(memory_space=pltpu.SMEM),
            pl.BlockSpec((tm, ATTN_WIDTH), lambda b, i: (row(b, i), 0)),
            pl.BlockSpec((tm, 4 * KV_WIDTH), lambda b, i: (row(b, i), kv_col)),
            pl.BlockSpec((BLOCK, 4 * KV_WIDTH),
                         lambda b, i: (jnp.maximum(row(b, i) * bpt - 1, 0), kv_col)),
            pl.BlockSpec((tm, ATTN_WIDTH), lambda b, i: (row(b, i), 1)),
            pl.BlockSpec((tm, D_MODEL), lambda b, i: (row(b, i), 0)),
            pl.BlockSpec((ATTN_WIDTH, D_MODEL), lambda b, i: (0, 0), pipeline_mode=pl.Buffered(1)),
        ],
        out_specs=pl.BlockSpec((tm, D_MODEL), lambda b, i: (row(b, i), 0)),
        out_shape=jax.ShapeDtypeStruct((t, D_MODEL), F32),
        scratch_shapes=[pltpu.VMEM((tm, ATTN_WIDTH), BF16)],
        compiler_params=pltpu.CompilerParams(
            dimension_semantics=("arbitrary", "arbitrary"),
            vmem_limit_bytes=VMEM_LIMIT),
        name="odd_out",
    )(sinks, qgkv, qgkv, qgkv, qgkv, x2d, w_out)


def _rope_cos_sin(seq, dim):
    inv = 1.0 / (ROPE_THETA ** (np.arange(0, dim, 2, dtype=np.float64) / dim))
    ang = np.arange(seq, dtype=np.float64)[:, None] * inv[None, :]
    return np.cos(ang), np.sin(ang)


def _retention_tables():
    h = np.arange(RET_HEADS, dtype=np.float64)
    log_g = np.log(1.0 - 2.0 ** (-5.0 - h))
    idx = np.arange(RET_CHUNK, dtype=np.float64)
    diff = idx[:, None] - idx[None, :]
    scale = RET_HEAD_DIM ** -0.5
    intra = np.where(diff >= 0, np.exp(log_g[:, None, None] * np.maximum(diff, 0.0)), 0.0)
    q_dec = np.exp(log_g[:, None] * (idx[None, :] + 1.0))
    k_dec = np.exp(log_g[:, None] * (RET_CHUNK - 1.0 - idx[None, :]))
    c_dec = np.exp(log_g * RET_CHUNK)
    wide = (RET_HEADS, RET_CHUNK, RET_HEAD_DIM)
    dmat = jnp.asarray(intra * scale, F32)
    qdec = jnp.asarray(np.broadcast_to(q_dec[:, :, None], wide), F32)
    kdec = jnp.asarray(np.broadcast_to(k_dec[:, :, None] * scale, wide), F32)
    cdec = jnp.asarray(np.broadcast_to(c_dec[:, None, None], (RET_HEADS, 1, RET_HEAD_DIM)), F32)
    return dmat, qdec, kdec, cdec


def _head_rope_tables(norm_w, seq, scale):
    half = ATTN_HEAD_DIM // 2
    reps = LANES // ATTN_HEAD_DIM
    cos, sin = _rope_cos_sin(seq, ATTN_HEAD_DIM)
    cos_l = jnp.asarray(np.tile(np.concatenate([cos, cos], axis=1), (1, reps)), F32)
    sin_l = jnp.asarray(np.tile(np.concatenate([-sin, sin], axis=1), (1, reps)), F32)
    w = norm_w.astype(F32)
    w_partner = jnp.concatenate([w[half:], w[:half]])
    ta = cos_l * jnp.tile(w * scale, reps)[None, :]
    tb = sin_l * jnp.tile(w_partner * scale, reps)[None, :]
    return ta, tb


def kernel(x, ev_norm_w, ev_w_in, ev_conv_w, ev_w_out, od_norm_w, od_w_in,
           od_q_norm_w, od_k_norm_w, od_sinks, od_w_out):
    batch, seq, d = x.shape
    assert d == D_MODEL and seq % IN_TM == 0 and seq % OUT_TM == 0
    x2d = x.reshape(batch * seq, d)

    cos, sin = _rope_cos_sin(seq, RET_HEAD_DIM)
    cw = jnp.pad(ev_conv_w[0], ((0, 8 - CONV_K), (0, 0)))
    qkvg, conv, w_out_e, w_in_o, w_out_o = _even_in(
        x2d, ev_norm_w[0][None, :], ev_w_in[0].astype(BF16),
        jnp.asarray(cos, F32), jnp.asarray(sin, F32), cw,
        (ev_w_out[0], od_w_in[0], od_w_out[0]), seq)
    dmat, qdec, kdec, cdec = _retention_tables()
    x2d, h2d = _even_out(qkvg, conv, x2d, w_out_e, dmat, qdec, kdec, cdec,
                         od_norm_w[0][None, :], batch, seq)

    qa, qb = _head_rope_tables(od_q_norm_w[0], seq, ATTN_HEAD_DIM ** -0.5)
    ka, kb = _head_rope_tables(od_k_norm_w[0], seq, 1.0)
    head_id = np.arange(MXU_N) // ATTN_HEAD_DIM
    ones_bd = jnp.asarray(head_id[:, None] == head_id[None, :], BF16)
    qgkv = _odd_in(h2d, w_in_o, qa, qb, ka, kb, ones_bd, seq)
    x2d = _odd_out(od_sinks[0].astype(F32), qgkv, x2d, w_out_o, batch, seq)
    return x2d.reshape(batch, seq, d)
```

```python
import functools

import numpy as np
import jax
import jax.numpy as jnp
from jax import lax
from jax.experimental import pallas as pl
from jax.experimental.pallas import tpu as pltpu

D_MODEL = 2048
RET_HEADS = 4
RET_HEAD_DIM = 256
RET_WIDTH = RET_HEADS * RET_HEAD_DIM
CONV_WIDTH = D_MODEL - RET_WIDTH
CONV_K = 3
RET_CHUNK = 256
EVEN_IN = 4 * RET_WIDTH + 4 * CONV_WIDTH
ATTN_HEAD_DIM = 64
ATTN_Q_HEADS = 32
ATTN_KV_HEADS = 4
ATTN_GROUP = ATTN_Q_HEADS // ATTN_KV_HEADS
ATTN_WIDTH = ATTN_Q_HEADS * ATTN_HEAD_DIM
KV_WIDTH = ATTN_KV_HEADS * ATTN_HEAD_DIM
BLOCK = 128
ROPE_THETA = 10000.0
EPS = 1e-6
NEG = -1e30

LANES = 128
MXU_N = 256
SUB_N = 512
N_SUB = 4
Q_STEPS = 2
IN_TM = 1024
IN_TN = N_SUB * SUB_N
OUT_TM = 512
HALO = 8
VMEM_LIMIT = 56 * 1024 * 1024

F32 = jnp.float32
BF16 = jnp.bfloat16


def _dot(a, b):
    return jnp.dot(a, b, preferred_element_type=F32)


def _dot_nt(a, b):
    return lax.dot_general(a, b, (((1,), (1,)), ((), ())), preferred_element_type=F32)


def _dot_tn(a, b):
    return lax.dot_general(a, b, (((0,), (0,)), ((), ())), preferred_element_type=F32)


def _silu(g):
    return (0.5 * g) * (1.0 + jnp.tanh(0.5 * g))


def _rms_rows(x, nw):
    ms = jnp.mean(x * x, axis=-1, keepdims=True)
    return x * lax.rsqrt(ms + EPS) * nw


def _sub_cols(s):
    return slice(s * SUB_N, (s + 1) * SUB_N)


def _even_in_kernel(x_ref, nw_ref, w0_ref, w1_ref, w2_ref, w3_ref,
                    cos_ref, sin_ref, cw_ref, c0_ref, c1_ref, c2_ref,
                    qkvg_ref, conv_ref, c0_out, c1_out, c2_out,
                    h_scr, cu_scr, tail_scr, *, seq_tiles):
    i = pl.program_id(0)
    j = pl.program_id(1)
    tm = h_scr.shape[1]
    chunk = x_ref.shape[0]
    w_refs = (w0_ref, w1_ref, w2_ref, w3_ref)

    for src, dst in ((c0_ref, c0_out), (c1_ref, c1_out), (c2_ref, c2_out)):
        dst[...] = src[...].astype(BF16)

    def norm_chunk():
        rows = pl.ds(pl.multiple_of(j * chunk, chunk), chunk)
        h_scr[i % 2, rows, :] = _rms_rows(x_ref[...], nw_ref[...]).astype(BF16)

    def sub_dot(s):
        return _dot(h_scr[(i + 1) % 2], w_refs[s][...])

    @pl.when(i == 0)
    def _():
        norm_chunk()
        tail_scr[...] = jnp.zeros_like(tail_scr)

    projecting = i >= 1

    def rope_then(tail_fn):
        norm_chunk()
        cos = cos_ref[...]
        sin = sin_ref[...]
        for s in range(N_SUB // 2):
            acc = sub_dot(s)
            for hd in range(SUB_N // RET_HEAD_DIM):
                base = s * SUB_N + hd * RET_HEAD_DIM
                x1 = acc[:, hd * RET_HEAD_DIM:hd * RET_HEAD_DIM + LANES]
                x2 = acc[:, hd * RET_HEAD_DIM + LANES:(hd + 1) * RET_HEAD_DIM]
                qkvg_ref[:, base:base + LANES] = (x1 * cos - x2 * sin).astype(BF16)
                qkvg_ref[:, base + LANES:base + RET_HEAD_DIM] = (x2 * cos + x1 * sin).astype(BF16)
        for s in range(N_SUB // 2, N_SUB):
            qkvg_ref[:, _sub_cols(s)] = tail_fn(sub_dot(s)).astype(BF16)

    @pl.when(jnp.logical_and(projecting, j == 0))
    def _():
        rope_then(lambda v: v)

    @pl.when(jnp.logical_and(projecting, j == 1))
    def _():
        rope_then(_silu)

    @pl.when(jnp.logical_and(projecting, j >= 2))
    def _():
        norm_chunk()
        cs = j - 2
        cu = sub_dot(0) * sub_dot(1)
        at_seq_start = ((i - 1) % seq_tiles) == 0
        halo_rows = slice(HALO - (CONV_K - 1), HALO)
        cu_scr[halo_rows, :] = jnp.where(at_seq_start, 0.0, tail_scr[cs, halo_rows, :])
        cu_scr[HALO:HALO + tm, :] = cu
        tail_scr[cs, halo_rows, :] = cu_scr[HALO + tm - (CONV_K - 1):HALO + tm, :]
        cw = cw_ref[...]
        conv = cw[CONV_K - 1:CONV_K, :] * cu
        for t in range(CONV_K - 1):
            shift = CONV_K - 1 - t
            conv = conv + cw[t:t + 1, :] * cu_scr[HALO - shift:HALO - shift + tm, :]
        gated = sub_dot(2) * conv
        conv_ref[...] = (gated * _silu(sub_dot(3))).astype(BF16)


def _even_in(x2d, nw, w, cos, sin, cw, to_cast, seq):
    t = x2d.shape[0]
    tm = IN_TM
    n_tiles = t // tm
    seq_tiles = seq // tm
    n_ret_steps = 4 * RET_WIDTH // IN_TN
    n_conv_steps = CONV_WIDTH // SUB_N
    nj = n_ret_steps + n_conv_steps
    chunk = tm // nj
    n_proj_steps = n_tiles * nj
    conv_first = tuple((4 * RET_WIDTH + b * CONV_WIDTH) // SUB_N for b in (1, 2, 0, 3))

    def tile(i):
        return jnp.maximum(i - 1, 0)

    def step(i, j):
        return jnp.where(i == 0, 0, j)

    def w_spec(s):
        def index(i, j):
            jj = step(i, j)
            half = N_SUB // 2
            ret_idx = half * jnp.minimum(jj, n_ret_steps - 1) + (s % half) + (s // half) * N_SUB
            conv_idx = conv_first[s] + jj - n_ret_steps
            return 0, jnp.where(jj < n_ret_steps, ret_idx, conv_idx)
        return pl.BlockSpec((D_MODEL, SUB_N), index)

    def cast_spec(a):
        rows = a.shape[0] // n_proj_steps
        assert rows * n_proj_steps == a.shape[0] and rows % 16 == 0
        return pl.BlockSpec((rows, a.shape[1]), lambda i, j: (tile(i) * nj + step(i, j), 0))

    def table_spec():
        return pl.BlockSpec((tm, LANES), lambda i, j: (tile(i) % seq_tiles, 0))

    kern = functools.partial(_even_in_kernel, seq_tiles=seq_tiles)
    return pl.pallas_call(
        kern,
        grid=(n_tiles + 1, nj),
        in_specs=[
            pl.BlockSpec((chunk, D_MODEL), lambda i, j: (jnp.minimum(i, n_tiles - 1) * nj + j, 0)),
            pl.BlockSpec((1, D_MODEL), lambda i, j: (0, 0), pipeline_mode=pl.Buffered(1)),
            w_spec(0), w_spec(1), w_spec(2), w_spec(3),
            table_spec(), table_spec(),
            pl.BlockSpec((8, SUB_N), lambda i, j: (0, jnp.maximum(step(i, j) - n_ret_steps, 0))),
        ] + [cast_spec(a) for a in to_cast],
        out_specs=[
            pl.BlockSpec((tm, IN_TN),
                         lambda i, j: (tile(i), jnp.minimum(step(i, j), n_ret_steps - 1))),
            pl.BlockSpec((tm, SUB_N),
                         lambda i, j: (tile(i), jnp.maximum(step(i, j) - n_ret_steps, 0))),
        ] + [cast_spec(a) for a in to_cast],
        out_shape=[
            jax.ShapeDtypeStruct((t, 4 * RET_WIDTH), BF16),
            jax.ShapeDtypeStruct((t, CONV_WIDTH), BF16),
        ] + [jax.ShapeDtypeStruct(a.shape, BF16) for a in to_cast],
        scratch_shapes=[
            pltpu.VMEM((2, tm, D_MODEL), BF16),
            pltpu.VMEM((HALO + tm, SUB_N), F32),
            pltpu.VMEM((n_conv_steps, HALO, SUB_N), F32),
        ],
        compiler_params=pltpu.CompilerParams(
            dimension_semantics=("arbitrary", "arbitrary"),
            vmem_limit_bytes=VMEM_LIMIT),
        name="even_in",
    )(x2d, nw, w, w, w, w, cos, sin, cw, *to_cast)


def _even_out_kernel(q_ref, k_ref, v_ref, g_ref, conv_ref, x_ref, w_ref,
                     dmat_ref, qdec_ref, kdec_ref, cdec_ref, nw_next_ref,
                     o_ref, h_next_ref, state_scr, y_scr):
    tm = x_ref.shape[0]

    @pl.when(pl.program_id(1) == 0)
    def _():
        state_scr[...] = jnp.zeros_like(state_scr)

    for c in range(tm // RET_CHUNK):
        rows = slice(c * RET_CHUNK, (c + 1) * RET_CHUNK)
        for h in range(RET_HEADS):
            cols = slice(h * RET_HEAD_DIM, (h + 1) * RET_HEAD_DIM)
            qh = q_ref[rows, cols]
            kh = k_ref[rows, cols]
            vh = v_ref[rows, cols]
            scores = _dot_nt(qh, kh) * dmat_ref[h]
            inner = _dot(scores.astype(BF16), vh)
            state = state_scr[h]
            cross = _dot(qh, state.astype(BF16)) * qdec_ref[h]
            o = inner + cross
            kd = (kh.astype(F32) * kdec_ref[h]).astype(BF16)
            state_scr[h] = cdec_ref[h] * state + _dot_tn(kd, vh)
            ms = jnp.mean(o * o, axis=-1, keepdims=True)
            ro = o * lax.rsqrt(ms + EPS) * g_ref[rows, cols].astype(F32)
            y_scr[rows, cols] = ro.astype(BF16)
    y_scr[:, RET_WIDTH:] = conv_ref[...]

    ssq = jnp.zeros((tm, 1), F32)
    for n in range(D_MODEL // SUB_N):
        xn = x_ref[:, _sub_cols(n)] + _dot(y_scr[...], w_ref[:, _sub_cols(n)])
        o_ref[:, _sub_cols(n)] = xn
        ssq = ssq + jnp.sum(xn * xn, axis=-1, keepdims=True)
    r = lax.rsqrt(ssq * (1.0 / D_MODEL) + EPS)
    for n in range(D_MODEL // SUB_N):
        h_next_ref[:, _sub_cols(n)] = (o_ref[:, _sub_cols(n)] * r * nw_next_ref[:, _sub_cols(n)]).astype(BF16)


def _even_out(qkvg, conv, x2d, w_out, dmat, qdec, kdec, cdec, nw_next, batch, seq):
    t = x2d.shape[0]
    tm = OUT_TM
    ns = seq // tm

    def row(b, i):
        return b * ns + i

    def qkvg_spec(col):
        return pl.BlockSpec((tm, RET_WIDTH), lambda b, i: (row(b, i), col))

    def const_spec(shape):
        return pl.BlockSpec(shape, lambda b, i: (0,) * len(shape), pipeline_mode=pl.Buffered(1))

    return pl.pallas_call(
        _even_out_kernel,
        grid=(batch, ns),
        in_specs=[
            qkvg_spec(0), qkvg_spec(2), qkvg_spec(1), qkvg_spec(3),
            pl.BlockSpec((tm, CONV_WIDTH), lambda b, i: (row(b, i), 0)),
            pl.BlockSpec((tm, D_MODEL), lambda b, i: (row(b, i), 0)),
            const_spec((D_MODEL, D_MODEL)),
            const_spec(dmat.shape), const_spec(qdec.shape),
            const_spec(kdec.shape), const_spec(cdec.shape), const_spec(nw_next.shape),
        ],
        out_specs=[
            pl.BlockSpec((tm, D_MODEL), lambda b, i: (row(b, i), 0)),
            pl.BlockSpec((tm, D_MODEL), lambda b, i: (row(b, i), 0)),
        ],
        out_shape=[
            jax.ShapeDtypeStruct((t, D_MODEL), F32),
            jax.ShapeDtypeStruct((t, D_MODEL), BF16),
        ],
        scratch_shapes=[
            pltpu.VMEM((RET_HEADS, RET_HEAD_DIM, RET_HEAD_DIM), F32),
            pltpu.VMEM((tm, D_MODEL), BF16),
        ],
        compiler_params=pltpu.CompilerParams(
            dimension_semantics=("arbitrary", "arbitrary"),
            vmem_limit_bytes=VMEM_LIMIT),
        name="even_out",
    )(qkvg, qkvg, qkvg, qkvg, conv, x2d, w_out, dmat, qdec, kdec, cdec, nw_next)


def _odd_in_kernel(h_ref, w0_ref, w1_ref, w2_ref, w3_ref, wkv_ref,
                   qa_ref, qb_ref, ka_ref, kb_ref, ones_ref,
                   qg_ref, kv_ref):
    j = pl.program_id(1)
    w_refs = (w0_ref, w1_ref, w2_ref, w3_ref)

    def sub_dot(w_ref):
        return _dot(h_ref[...], w_ref[...])

    def dup_heads(x):
        lane = lax.broadcasted_iota(jnp.int32, (x.shape[0], LANES), 1)
        low = lane < ATTN_HEAD_DIM
        blocks = []
        for b in range(x.shape[1] // LANES):
            xb = x[:, b * LANES:(b + 1) * LANES]
            swapped = pltpu.roll(xb, ATTN_HEAD_DIM, axis=1)
            blocks += [jnp.where(low, xb, swapped), jnp.where(low, swapped, xb)]
        return jnp.concatenate(blocks, axis=1)

    def norm_rope(acc, ta, tb):
        width = acc.shape[1]
        sq = (acc * acc).astype(BF16)
        ss = jnp.concatenate(
            [_dot(sq[:, c * MXU_N:(c + 1) * MXU_N], ones_ref[...]) for c in range(width // MXU_N)],
            axis=1)
        r = lax.rsqrt(ss * (1.0 / ATTN_HEAD_DIM) + EPS)
        half = ATTN_HEAD_DIM // 2
        lane = lax.broadcasted_iota(jnp.int32, acc.shape, 1)
        first_half = (lane % ATTN_HEAD_DIM) < half
        partner = jnp.where(first_half,
                            pltpu.roll(acc, width - half, axis=1),
                            pltpu.roll(acc, half, axis=1))
        reps = width // LANES
        ta_w = jnp.concatenate([ta] * reps, axis=1)
        tb_w = jnp.concatenate([tb] * reps, axis=1)
        return r * (acc * ta_w + partner * tb_w)

    @pl.when(j < Q_STEPS)
    def _():
        for s in range(N_SUB // 2):
            qg_ref[:, _sub_cols(s)] = norm_rope(sub_dot(w_refs[s]), qa_ref[...], qb_ref[...]).astype(BF16)
        for s in range(N_SUB // 2, N_SUB):
            qg_ref[:, _sub_cols(s)] = _silu(sub_dot(w_refs[s])).astype(BF16)

    @pl.when(j == Q_STEPS)
    def _():
        acc = sub_dot(wkv_ref)
        k = norm_rope(acc[:, :KV_WIDTH], ka_ref[...], kb_ref[...])
        kv_ref[:, :2 * KV_WIDTH] = dup_heads(k).astype(BF16)
        kv_ref[:, 2 * KV_WIDTH:] = dup_heads(acc[:, KV_WIDTH:]).astype(BF16)


def _odd_in(h2d, w, qa, qb, ka, kb, ones_bd, seq):
    t = h2d.shape[0]
    tm = IN_TM
    seq_tiles = seq // tm
    half = N_SUB // 2
    assert ATTN_WIDTH == Q_STEPS * half * SUB_N
    gate_col0 = (ATTN_WIDTH + 2 * KV_WIDTH) // SUB_N

    def w_spec(s):
        def index(i, j):
            jj = jnp.minimum(j, Q_STEPS - 1)
            return 0, half * jj + (s % half) + (s // half) * gate_col0
        return pl.BlockSpec((D_MODEL, SUB_N), index)

    def tab_spec():
        return pl.BlockSpec((tm, LANES), lambda i, j: (i % seq_tiles, 0))

    def resident_spec(shape, col):
        return pl.BlockSpec(shape, lambda i, j: (0, col), pipeline_mode=pl.Buffered(1))

    return pl.pallas_call(
        _odd_in_kernel,
        grid=(t // tm, 3),
        in_specs=[
            pl.BlockSpec((tm, D_MODEL), lambda i, j: (i, 0)),
            w_spec(0), w_spec(1), w_spec(2), w_spec(3),
            resident_spec((D_MODEL, SUB_N), ATTN_WIDTH // SUB_N),
            tab_spec(), tab_spec(), tab_spec(), tab_spec(),
            resident_spec((MXU_N, MXU_N), 0),
        ],
        out_specs=[
            pl.BlockSpec((tm, IN_TN), lambda i, j: (i, jnp.minimum(j, 1))),
            pl.BlockSpec((tm, 4 * KV_WIDTH), lambda i, j: (i, 0)),
        ],
        out_shape=[
            jax.ShapeDtypeStruct((t, 2 * ATTN_WIDTH), BF16),
            jax.ShapeDtypeStruct((t, 4 * KV_WIDTH), BF16),
        ],
        compiler_params=pltpu.CompilerParams(
            dimension_semantics=("arbitrary", "arbitrary"),
            vmem_limit_bytes=VMEM_LIMIT),
        name="odd_in",
    )(h2d, w, w, w, w, w, qa, qb, ka, kb, ones_bd)


def _odd_out_kernel(sink_ref, q0_ref, g0_ref, q1_ref, g1_ref, kv_ref, kvp_ref, x_ref, w_ref,
                    o_ref, y_scr):
    tm = x_ref.shape[0]
    q_refs = (q0_ref, q1_ref)
    g_refs = (g0_ref, g1_ref)
    kv_per_group = ATTN_KV_HEADS // Q_STEPS
    at_seq_start = pl.program_id(1) == 0
    pairs = ATTN_GROUP // 2
    half_w = 2 * KV_WIDTH

    lane_kv = lax.broadcasted_iota(jnp.int32, (2 * BLOCK, LANES), 1)
    low_kv = lane_kv < ATTN_HEAD_DIM
    rr = lax.broadcasted_iota(jnp.int32, (BLOCK, BLOCK), 0)
    cc = lax.broadcasted_iota(jnp.int32, (BLOCK, BLOCK), 1)
    tri = cc <= rr
    low_o = cc < ATTN_HEAD_DIM

    for bl in range(tm // BLOCK):
        rows = slice(bl * BLOCK, (bl + 1) * BLOCK)
        for c in range(ATTN_KV_HEADS):
            kcols = slice(c * LANES, (c + 1) * LANES)
            vcols = slice(half_w + c * LANES, half_w + (c + 1) * LANES)
            if bl == 0:
                k_prev = kvp_ref[:, kcols]
                v_prev = kvp_ref[:, vcols]
            else:
                prows = slice((bl - 1) * BLOCK, bl * BLOCK)
                k_prev = kv_ref[prows, kcols]
                v_prev = kv_ref[prows, vcols]
            kk = jnp.concatenate([k_prev, kv_ref[rows, kcols]], axis=0)
            vv = jnp.concatenate([v_prev, kv_ref[rows, vcols]], axis=0)
            zero = jnp.zeros_like(kk)
            k_par = (jnp.where(low_kv, kk, zero), jnp.where(low_kv, zero, kk))
            q_ref = q_refs[c // kv_per_group]
            g_ref = g_refs[c // kv_per_group]
            c_in = c % kv_per_group
            qs = jnp.concatenate(
                [q_ref[rows, (pairs * c_in + a) * LANES:(pairs * c_in + a + 1) * LANES]
                 for a in range(pairs)], axis=0)
            probs = []
            denoms = []
            for par in range(2):
                s = _dot_nt(qs, k_par[par])
                for a in range(pairs):
                    sa = s[a * BLOCK:(a + 1) * BLOCK]
                    s_prev = sa[:, :BLOCK]
                    if bl == 0:
                        s_prev = jnp.where(at_seq_start, NEG, s_prev)
                    f = jnp.where(tri, sa[:, BLOCK:], s_prev)
                    sink = sink_ref[ATTN_GROUP * c + 2 * a + par]
                    m = jnp.maximum(jnp.max(f, axis=1, keepdims=True), sink)
                    p = jnp.exp(f - m)
                    denoms.append(jnp.sum(p, axis=1, keepdims=True) + jnp.exp(sink - m))
                    probs.append(jnp.concatenate(
                        [jnp.where(tri, 0.0, p), jnp.where(tri, p, 0.0)], axis=1).astype(BF16))
            o = _dot(jnp.concatenate(probs, axis=0), vv)
            for a in range(pairs):
                o_even = o[a * BLOCK:(a + 1) * BLOCK] / denoms[a]
                o_odd = o[(pairs + a) * BLOCK:(pairs + a + 1) * BLOCK] / denoms[pairs + a]
                gcols = slice((pairs * c_in + a) * LANES, (pairs * c_in + a + 1) * LANES)
                ocols = slice((pairs * c + a) * LANES, (pairs * c + a + 1) * LANES)
                gated = jnp.where(low_o, o_even, o_odd) * g_ref[rows, gcols].astype(F32)
                y_scr[rows, ocols] = gated.astype(BF16)

    for n in range(D_MODEL // SUB_N):
        o_ref[:, _sub_cols(n)] = x_ref[:, _sub_cols(n)] + _dot(y_scr[...], w_ref[:, _sub_cols(n)])


def _odd_out(sinks, qg, kv, x2d, w_out, batch, seq):
    t = x2d.shape[0]
    tm = OUT_TM
    ns = seq // tm
    bpt = tm // BLOCK

    def row(b, i):
        return b * ns + i

    def qg_spec(col):
        return pl.BlockSpec((tm, ATTN_WIDTH // Q_STEPS), lambda b, i: (row(b, i), col))

    return pl.pallas_call(
        _odd_out_kernel,
        grid=(batch, ns),
        in_specs=[
            pl.BlockSpec(memory_space=pltpu.SMEM),
            qg_spec(0), qg_spec(1), qg_spec(2), qg_spec(3),
            pl.BlockSpec((tm, 4 * KV_WIDTH), lambda b, i: (row(b, i), 0)),
            pl.BlockSpec((BLOCK, 4 * KV_WIDTH),
                         lambda b, i: (jnp.maximum(row(b, i) * bpt - 1, 0), 0)),
            pl.BlockSpec((tm, D_MODEL), lambda b, i: (row(b, i), 0)),
            pl.BlockSpec((ATTN_WIDTH, D_MODEL), lambda b, i: (0, 0), pipeline_mode=pl.Buffered(1)),
        ],
        out_specs=pl.BlockSpec((tm, D_MODEL), lambda b, i: (row(b, i), 0)),
        out_shape=jax.ShapeDtypeStruct((t, D_MODEL), F32),
        scratch_shapes=[pltpu.VMEM((tm, ATTN_WIDTH), BF16)],
        compiler_params=pltpu.CompilerParams(
            dimension_semantics=("arbitrary", "arbitrary"),
            vmem_limit_bytes=VMEM_LIMIT),
        name="odd_out",
    )(sinks, qg, qg, qg, qg, kv, kv, x2d, w_out)


def _rope_cos_sin(seq, dim):
    inv = 1.0 / (ROPE_THETA ** (np.arange(0, dim, 2, dtype=np.float64) / dim))
    ang = np.arange(seq, dtype=np.float64)[:, None] * inv[None, :]
    return np.cos(ang), np.sin(ang)


def _retention_tables():
    h = np.arange(RET_HEADS, dtype=np.float64)
    log_g = np.log(1.0 - 2.0 ** (-5.0 - h))
    idx = np.arange(RET_CHUNK, dtype=np.float64)
    diff = idx[:, None] - idx[None, :]
    scale = RET_HEAD_DIM ** -0.5
    intra = np.where(diff >= 0, np.exp(log_g[:, None, None] * np.maximum(diff, 0.0)), 0.0)
    q_dec = np.exp(log_g[:, None] * (idx[None, :] + 1.0))
    k_dec = np.exp(log_g[:, None] * (RET_CHUNK - 1.0 - idx[None, :]))
    c_dec = np.exp(log_g * RET_CHUNK)
    wide = (RET_HEADS, RET_CHUNK, RET_HEAD_DIM)
    dmat = jnp.asarray(intra * scale, F32)
    qdec = jnp.asarray(np.broadcast_to(q_dec[:, :, None], wide), F32)
    kdec = jnp.asarray(np.broadcast_to(k_dec[:, :, None] * scale, wide), F32)
    cdec = jnp.asarray(np.broadcast_to(c_dec[:, None, None], (RET_HEADS, 1, RET_HEAD_DIM)), F32)
    return dmat, qdec, kdec, cdec


def _head_rope_tables(norm_w, seq, scale):
    half = ATTN_HEAD_DIM // 2
    reps = LANES // ATTN_HEAD_DIM
    cos, sin = _rope_cos_sin(seq, ATTN_HEAD_DIM)
    cos_l = jnp.asarray(np.tile(np.concatenate([cos, cos], axis=1), (1, reps)), F32)
    sin_l = jnp.asarray(np.tile(np.concatenate([-sin, sin], axis=1), (1, reps)), F32)
    w = norm_w.astype(F32)
    w_partner = jnp.concatenate([w[half:], w[:half]])
    ta = cos_l * jnp.tile(w * scale, reps)[None, :]
    tb = sin_l * jnp.tile(w_partner * scale, reps)[None, :]
    return ta, tb


def kernel(x, ev_norm_w, ev_w_in, ev_conv_w, ev_w_out, od_norm_w, od_w_in,
           od_q_norm_w, od_k_norm_w, od_sinks, od_w_out):
    batch, seq, d = x.shape
    assert d == D_MODEL and seq % IN_TM == 0 and seq % OUT_TM == 0
    x2d = x.reshape(batch * seq, d)

    cos, sin = _rope_cos_sin(seq, RET_HEAD_DIM)
    cw = jnp.pad(ev_conv_w[0], ((0, 8 - CONV_K), (0, 0)))
    qkvg, conv, w_out_e, w_in_o, w_out_o = _even_in(
        x2d, ev_norm_w[0][None, :], ev_w_in[0].astype(BF16),
        jnp.asarray(cos, F32), jnp.asarray(sin, F32), cw,
        (ev_w_out[0], od_w_in[0], od_w_out[0]), seq)
    dmat, qdec, kdec, cdec = _retention_tables()
    x2d, h2d = _even_out(qkvg, conv, x2d, w_out_e, dmat, qdec, kdec, cdec,
                         od_norm_w[0][None, :], batch, seq)

    qa, qb = _head_rope_tables(od_q_norm_w[0], seq, ATTN_HEAD_DIM ** -0.5)
    ka, kb = _head_rope_tables(od_k_norm_w[0], seq, 1.0)
    head_id = np.arange(MXU_N) // ATTN_HEAD_DIM
    ones_bd = jnp.asarray(head_id[:, None] == head_id[None, :], BF16)
    qg, kv = _odd_in(h2d, w_in_o, qa, qb, ka, kb, ones_bd, seq)
    x2d = _odd_out(od_sinks[0].astype(F32), qg, kv, x2d, w_out_o, batch, seq)
    return x2d.reshape(batch, seq, d)
```

```python
import functools

import numpy as np
import jax
import jax.numpy as jnp
from jax import lax
from jax.experimental import pallas as pl
from jax.experimental.pallas import tpu as pltpu

D_MODEL = 2048
RET_HEADS = 4
RET_HEAD_DIM = 256
RET_WIDTH = RET_HEADS * RET_HEAD_DIM
CONV_WIDTH = D_MODEL - RET_WIDTH
CONV_K = 3
RET_CHUNK = 256
EVEN_IN = 4 * RET_WIDTH + 4 * CONV_WIDTH
ATTN_HEAD_DIM = 64
ATTN_Q_HEADS = 32
ATTN_KV_HEADS = 4
ATTN_GROUP = ATTN_Q_HEADS // ATTN_KV_HEADS
ATTN_WIDTH = ATTN_Q_HEADS * ATTN_HEAD_DIM
KV_WIDTH = ATTN_KV_HEADS * ATTN_HEAD_DIM
BLOCK = 128
ROPE_THETA = 10000.0
EPS = 1e-6
NEG = -1e30

LANES = 128
MXU_N = 256
SUB_N = 512
N_SUB = 4
Q_STEPS = 2
KV_STEP = 1
IN_TM = 1024
IN_TN = N_SUB * SUB_N
OUT_TM = 512
HALO = 8
VMEM_LIMIT = 56 * 1024 * 1024

F32 = jnp.float32
BF16 = jnp.bfloat16


def _dot(a, b):
    return jnp.dot(a, b, preferred_element_type=F32)


def _dot_nt(a, b):
    return lax.dot_general(a, b, (((1,), (1,)), ((), ())), preferred_element_type=F32)


def _dot_tn(a, b):
    return lax.dot_general(a, b, (((0,), (0,)), ((), ())), preferred_element_type=F32)


def _silu(g):
    return (0.5 * g) * (1.0 + jnp.tanh(0.5 * g))


def _rms_rows(x, nw):
    ms = jnp.mean(x * x, axis=-1, keepdims=True)
    return x * lax.rsqrt(ms + EPS) * nw


def _sub_cols(s):
    return slice(s * SUB_N, (s + 1) * SUB_N)


def _even_in_kernel(x_ref, nw_ref, w0_ref, w1_ref, w2_ref, w3_ref,
                    cos_ref, sin_ref, cw_ref, c0_ref, c1_ref, c2_ref,
                    qkvg_ref, conv_ref, c0_out, c1_out, c2_out,
                    h_scr, cu_scr, tail_scr, *, seq_tiles):
    i = pl.program_id(0)
    j = pl.program_id(1)
    tm = h_scr.shape[1]
    chunk = x_ref.shape[0]
    w_refs = (w0_ref, w1_ref, w2_ref, w3_ref)

    def side_work():
        for src, dst in ((c0_ref, c0_out), (c1_ref, c1_out), (c2_ref, c2_out)):
            dst[...] = src[...].astype(BF16)
        rows = pl.ds(pl.multiple_of(j * chunk, chunk), chunk)
        h_scr[i % 2, rows, :] = _rms_rows(x_ref[...], nw_ref[...]).astype(BF16)

    def sub_dot(s):
        return _dot(h_scr[(i + 1) % 2], w_refs[s][...])

    @pl.when(i == 0)
    def _():
        side_work()
        tail_scr[...] = jnp.zeros_like(tail_scr)

    projecting = i >= 1

    def rope_then(tail_fn):
        side_work()
        cos = cos_ref[...]
        sin = sin_ref[...]
        for s in range(N_SUB // 2):
            acc = sub_dot(s)
            for hd in range(SUB_N // RET_HEAD_DIM):
                base = s * SUB_N + hd * RET_HEAD_DIM
                x1 = acc[:, hd * RET_HEAD_DIM:hd * RET_HEAD_DIM + LANES]
                x2 = acc[:, hd * RET_HEAD_DIM + LANES:(hd + 1) * RET_HEAD_DIM]
                qkvg_ref[:, base:base + LANES] = (x1 * cos - x2 * sin).astype(BF16)
                qkvg_ref[:, base + LANES:base + RET_HEAD_DIM] = (x2 * cos + x1 * sin).astype(BF16)
        for s in range(N_SUB // 2, N_SUB):
            qkvg_ref[:, _sub_cols(s)] = tail_fn(sub_dot(s)).astype(BF16)

    @pl.when(jnp.logical_and(projecting, j == 0))
    def _():
        rope_then(lambda v: v)

    @pl.when(jnp.logical_and(projecting, j == 1))
    def _():
        rope_then(_silu)

    @pl.when(jnp.logical_and(projecting, j >= 2))
    def _():
        side_work()
        cs = j - 2
        cu = sub_dot(0) * sub_dot(1)
        at_seq_start = ((i - 1) % seq_tiles) == 0
        halo_rows = slice(HALO - (CONV_K - 1), HALO)
        cu_scr[halo_rows, :] = jnp.where(at_seq_start, 0.0, tail_scr[cs, halo_rows, :])
        cu_scr[HALO:HALO + tm, :] = cu
        tail_scr[cs, halo_rows, :] = cu_scr[HALO + tm - (CONV_K - 1):HALO + tm, :]
        cw = cw_ref[...]
        conv = cw[CONV_K - 1:CONV_K, :] * cu
        for t in range(CONV_K - 1):
            shift = CONV_K - 1 - t
            conv = conv + cw[t:t + 1, :] * cu_scr[HALO - shift:HALO - shift + tm, :]
        gated = sub_dot(2) * conv
        conv_ref[...] = (gated * _silu(sub_dot(3))).astype(BF16)


def _even_in(x2d, nw, w, cos, sin, cw, to_cast, seq):
    t = x2d.shape[0]
    tm = IN_TM
    n_tiles = t // tm
    seq_tiles = seq // tm
    n_ret_steps = 4 * RET_WIDTH // IN_TN
    n_conv_steps = CONV_WIDTH // SUB_N
    nj = n_ret_steps + n_conv_steps
    chunk = tm // nj
    n_proj_steps = n_tiles * nj
    conv_first = tuple((4 * RET_WIDTH + b * CONV_WIDTH) // SUB_N for b in (1, 2, 0, 3))

    def tile(i):
        return jnp.maximum(i - 1, 0)

    def step(i, j):
        return jnp.where(i == 0, 0, j)

    def w_spec(s):
        def index(i, j):
            jj = step(i, j)
            half = N_SUB // 2
            ret_idx = half * jnp.minimum(jj, n_ret_steps - 1) + (s % half) + (s // half) * N_SUB
            conv_idx = conv_first[s] + jj - n_ret_steps
            return 0, jnp.where(jj < n_ret_steps, ret_idx, conv_idx)
        return pl.BlockSpec((D_MODEL, SUB_N), index)

    def cast_spec(a):
        rows = a.shape[0] // n_proj_steps
        assert rows * n_proj_steps == a.shape[0] and rows % 16 == 0
        return pl.BlockSpec((rows, a.shape[1]), lambda i, j: (tile(i) * nj + step(i, j), 0))

    def table_spec():
        return pl.BlockSpec((tm, LANES), lambda i, j: (tile(i) % seq_tiles, 0))

    kern = functools.partial(_even_in_kernel, seq_tiles=seq_tiles)
    return pl.pallas_call(
        kern,
        grid=(n_tiles + 1, nj),
        in_specs=[
            pl.BlockSpec((chunk, D_MODEL), lambda i, j: (jnp.minimum(i, n_tiles - 1) * nj + j, 0)),
            pl.BlockSpec((1, D_MODEL), lambda i, j: (0, 0), pipeline_mode=pl.Buffered(1)),
            w_spec(0), w_spec(1), w_spec(2), w_spec(3),
            table_spec(), table_spec(),
            pl.BlockSpec((8, SUB_N), lambda i, j: (0, jnp.maximum(step(i, j) - n_ret_steps, 0))),
        ] + [cast_spec(a) for a in to_cast],
        out_specs=[
            pl.BlockSpec((tm, IN_TN),
                         lambda i, j: (tile(i), jnp.minimum(step(i, j), n_ret_steps - 1))),
            pl.BlockSpec((tm, SUB_N),
                         lambda i, j: (tile(i), jnp.maximum(step(i, j) - n_ret_steps, 0))),
        ] + [cast_spec(a) for a in to_cast],
        out_shape=[
            jax.ShapeDtypeStruct((t, 4 * RET_WIDTH), BF16),
            jax.ShapeDtypeStruct((t, CONV_WIDTH), BF16),
        ] + [jax.ShapeDtypeStruct(a.shape, BF16) for a in to_cast],
        scratch_shapes=[
            pltpu.VMEM((2, tm, D_MODEL), BF16),
            pltpu.VMEM((HALO + tm, SUB_N), F32),
            pltpu.VMEM((n_conv_steps, HALO, SUB_N), F32),
        ],
        compiler_params=pltpu.CompilerParams(
            dimension_semantics=("arbitrary", "arbitrary"),
            vmem_limit_bytes=VMEM_LIMIT),
        name="even_in",
    )(x2d, nw, w, w, w, w, cos, sin, cw, *to_cast)


def _even_out_kernel(q_ref, k_ref, v_ref, g_ref, conv_ref, x_ref, w_ref,
                     dmat_ref, qdec_ref, kdec_ref, cdec_ref, nw_next_ref,
                     o_ref, h_next_ref, state_scr, y_scr):
    tm = x_ref.shape[0]

    @pl.when(pl.program_id(1) == 0)
    def _():
        state_scr[...] = jnp.zeros_like(state_scr)

    for c in range(tm // RET_CHUNK):
        rows = slice(c * RET_CHUNK, (c + 1) * RET_CHUNK)
        for h in range(RET_HEADS):
            cols = slice(h * RET_HEAD_DIM, (h + 1) * RET_HEAD_DIM)
            qh = q_ref[rows, cols]
            kh = k_ref[rows, cols]
            vh = v_ref[rows, cols]
            scores = _dot_nt(qh, kh) * dmat_ref[h]
            inner = _dot(scores.astype(BF16), vh)
            state = state_scr[h]
            cross = _dot(qh, state.astype(BF16)) * qdec_ref[h]
            o = inner + cross
            kd = (kh.astype(F32) * kdec_ref[h]).astype(BF16)
            state_scr[h] = cdec_ref[h] * state + _dot_tn(kd, vh)
            ms = jnp.mean(o * o, axis=-1, keepdims=True)
            ro = o * lax.rsqrt(ms + EPS) * g_ref[rows, cols].astype(F32)
            y_scr[rows, cols] = ro.astype(BF16)
    y_scr[:, RET_WIDTH:] = conv_ref[...]

    ssq = jnp.zeros((tm, 1), F32)
    for n in range(D_MODEL // SUB_N):
        xn = x_ref[:, _sub_cols(n)] + _dot(y_scr[...], w_ref[:, _sub_cols(n)])
        o_ref[:, _sub_cols(n)] = xn
        ssq = ssq + jnp.sum(xn * xn, axis=-1, keepdims=True)
    r = lax.rsqrt(ssq * (1.0 / D_MODEL) + EPS)
    for n in range(D_MODEL // SUB_N):
        h_next_ref[:, _sub_cols(n)] = (o_ref[:, _sub_cols(n)] * r * nw_next_ref[:, _sub_cols(n)]).astype(BF16)


def _even_out(qkvg, conv, x2d, w_out, dmat, qdec, kdec, cdec, nw_next, batch, seq):
    t = x2d.shape[0]
    tm = OUT_TM
    ns = seq // tm

    def row(b, i):
        return b * ns + i

    def qkvg_spec(col):
        return pl.BlockSpec((tm, RET_WIDTH), lambda b, i: (row(b, i), col))

    def const_spec(shape):
        return pl.BlockSpec(shape, lambda b, i: (0,) * len(shape), pipeline_mode=pl.Buffered(1))

    return pl.pallas_call(
        _even_out_kernel,
        grid=(batch, ns),
        in_specs=[
            qkvg_spec(0), qkvg_spec(2), qkvg_spec(1), qkvg_spec(3),
            pl.BlockSpec((tm, CONV_WIDTH), lambda b, i: (row(b, i), 0)),
            pl.BlockSpec((tm, D_MODEL), lambda b, i: (row(b, i), 0)),
            const_spec((D_MODEL, D_MODEL)),
            const_spec(dmat.shape), const_spec(qdec.shape),
            const_spec(kdec.shape), const_spec(cdec.shape), const_spec(nw_next.shape),
        ],
        out_specs=[
            pl.BlockSpec((tm, D_MODEL), lambda b, i: (row(b, i), 0)),
            pl.BlockSpec((tm, D_MODEL), lambda b, i: (row(b, i), 0)),
        ],
        out_shape=[
            jax.ShapeDtypeStruct((t, D_MODEL), F32),
            jax.ShapeDtypeStruct((t, D_MODEL), BF16),
        ],
        scratch_shapes=[
            pltpu.VMEM((RET_HEADS, RET_HEAD_DIM, RET_HEAD_DIM), F32),
            pltpu.VMEM((tm, D_MODEL), BF16),
        ],
        compiler_params=pltpu.CompilerParams(
            dimension_semantics=("arbitrary", "arbitrary"),
            vmem_limit_bytes=VMEM_LIMIT),
        name="even_out",
    )(qkvg, qkvg, qkvg, qkvg, conv, x2d, w_out, dmat, qdec, kdec, cdec, nw_next)


def _odd_in_kernel(h_ref, w0_ref, w1_ref, w2_ref, w3_ref, wkv_ref,
                   qa_ref, qb_ref, ka_ref, kb_ref, ones_ref,
                   qg_ref, kv_ref):
    j = pl.program_id(1)
    w_refs = (w0_ref, w1_ref, w2_ref, w3_ref)

    def sub_dot(w_ref):
        return _dot(h_ref[...], w_ref[...])

    def dup_heads(x):
        lane = lax.broadcasted_iota(jnp.int32, (x.shape[0], LANES), 1)
        low = lane < ATTN_HEAD_DIM
        blocks = []
        for b in range(x.shape[1] // LANES):
            xb = x[:, b * LANES:(b + 1) * LANES]
            swapped = pltpu.roll(xb, ATTN_HEAD_DIM, axis=1)
            blocks += [jnp.where(low, xb, swapped), jnp.where(low, swapped, xb)]
        return jnp.concatenate(blocks, axis=1)

    def norm_rope(acc, ta, tb):
        width = acc.shape[1]
        sq = (acc * acc).astype(BF16)
        ss = jnp.concatenate(
            [_dot(sq[:, c * MXU_N:(c + 1) * MXU_N], ones_ref[...]) for c in range(width // MXU_N)],
            axis=1)
        r = lax.rsqrt(ss * (1.0 / ATTN_HEAD_DIM) + EPS)
        half = ATTN_HEAD_DIM // 2
        lane = lax.broadcasted_iota(jnp.int32, acc.shape, 1)
        first_half = (lane % ATTN_HEAD_DIM) < half
        partner = jnp.where(first_half,
                            pltpu.roll(acc, width - half, axis=1),
                            pltpu.roll(acc, half, axis=1))
        reps = width // LANES
        ta_w = jnp.concatenate([ta] * reps, axis=1)
        tb_w = jnp.concatenate([tb] * reps, axis=1)
        return r * (acc * ta_w + partner * tb_w)

    @pl.when(j != KV_STEP)
    def _():
        for s in range(N_SUB // 2):
            qg_ref[:, _sub_cols(s)] = norm_rope(sub_dot(w_refs[s]), qa_ref[...], qb_ref[...]).astype(BF16)
        for s in range(N_SUB // 2, N_SUB):
            qg_ref[:, _sub_cols(s)] = _silu(sub_dot(w_refs[s])).astype(BF16)

    @pl.when(j == KV_STEP)
    def _():
        acc = sub_dot(wkv_ref)
        k = norm_rope(acc[:, :KV_WIDTH], ka_ref[...], kb_ref[...])
        kv_ref[:, :2 * KV_WIDTH] = dup_heads(k).astype(BF16)
        kv_ref[:, 2 * KV_WIDTH:] = dup_heads(acc[:, KV_WIDTH:]).astype(BF16)


def _odd_in(h2d, w, qa, qb, ka, kb, ones_bd, seq):
    t = h2d.shape[0]
    tm = IN_TM
    seq_tiles = seq // tm
    half = N_SUB // 2
    assert ATTN_WIDTH == Q_STEPS * half * SUB_N
    gate_col0 = (ATTN_WIDTH + 2 * KV_WIDTH) // SUB_N

    def w_spec(s):
        def index(i, j):
            group = jnp.where(j < KV_STEP, 0, 1)
            return 0, half * group + (s % half) + (s // half) * gate_col0
        return pl.BlockSpec((D_MODEL, SUB_N), index)

    def tab_spec():
        return pl.BlockSpec((tm, LANES), lambda i, j: (i % seq_tiles, 0))

    def resident_spec(shape, col):
        return pl.BlockSpec(shape, lambda i, j: (0, col), pipeline_mode=pl.Buffered(1))

    return pl.pallas_call(
        _odd_in_kernel,
        grid=(t // tm, Q_STEPS + 1),
        in_specs=[
            pl.BlockSpec((tm, D_MODEL), lambda i, j: (i, 0)),
            w_spec(0), w_spec(1), w_spec(2), w_spec(3),
            resident_spec((D_MODEL, SUB_N), ATTN_WIDTH // SUB_N),
            tab_spec(), tab_spec(), tab_spec(), tab_spec(),
            resident_spec((MXU_N, MXU_N), 0),
        ],
        out_specs=[
            pl.BlockSpec((tm, IN_TN), lambda i, j: (i, jnp.where(j <= KV_STEP, 0, 1))),
            pl.BlockSpec((tm, 4 * KV_WIDTH), lambda i, j: (i, 0)),
        ],
        out_shape=[
            jax.ShapeDtypeStruct((t, 2 * ATTN_WIDTH), BF16),
            jax.ShapeDtypeStruct((t, 4 * KV_WIDTH), BF16),
        ],
        compiler_params=pltpu.CompilerParams(
            dimension_semantics=("arbitrary", "arbitrary"),
            vmem_limit_bytes=VMEM_LIMIT),
        name="odd_in",
    )(h2d, w, w, w, w, w, qa, qb, ka, kb, ones_bd)


def _odd_out_kernel(sink_ref, q0_ref, g0_ref, q1_ref, g1_ref, kv_ref, kvp_ref, x_ref, w_ref,
                    o_ref, y_scr):
    tm = x_ref.shape[0]
    q_refs = (q0_ref, q1_ref)
    g_refs = (g0_ref, g1_ref)
    kv_per_group = ATTN_KV_HEADS // Q_STEPS
    at_seq_start = pl.program_id(1) == 0
    pairs = ATTN_GROUP // 2
    half_w = 2 * KV_WIDTH

    lane_kv = lax.broadcasted_iota(jnp.int32, (2 * BLOCK, LANES), 1)
    low_kv = lane_kv < ATTN_HEAD_DIM
    rr = lax.broadcasted_iota(jnp.int32, (BLOCK, BLOCK), 0)
    cc = lax.broadcasted_iota(jnp.int32, (BLOCK, BLOCK), 1)
    tri = cc <= rr
    low_o = cc < ATTN_HEAD_DIM

    for bl in range(tm // BLOCK):
        rows = slice(bl * BLOCK, (bl + 1) * BLOCK)
        for c in range(ATTN_KV_HEADS):
            kcols = slice(c * LANES, (c + 1) * LANES)
            vcols = slice(half_w + c * LANES, half_w + (c + 1) * LANES)
            if bl == 0:
                k_prev = kvp_ref[:, kcols]
                v_prev = kvp_ref[:, vcols]
            else:
                prows = slice((bl - 1) * BLOCK, bl * BLOCK)
                k_prev = kv_ref[prows, kcols]
                v_prev = kv_ref[prows, vcols]
            kk = jnp.concatenate([k_prev, kv_ref[rows, kcols]], axis=0)
            vv = jnp.concatenate([v_prev, kv_ref[rows, vcols]], axis=0)
            zero = jnp.zeros_like(kk)
            k_par = (jnp.where(low_kv, kk, zero), jnp.where(low_kv, zero, kk))
            q_ref = q_refs[c // kv_per_group]
            g_ref = g_refs[c // kv_per_group]
            c_in = c % kv_per_group
            qs = jnp.concatenate(
                [q_ref[rows, (pairs * c_in + a) * LANES:(pairs * c_in + a + 1) * LANES]
                 for a in range(pairs)], axis=0)
            probs = []
            denoms = []
            for par in range(2):
                s = _dot_nt(qs, k_par[par])
                for a in range(pairs):
                    sa = s[a * BLOCK:(a + 1) * BLOCK]
                    s_prev = sa[:, :BLOCK]
                    if bl == 0:
                        s_prev = jnp.where(at_seq_start, NEG, s_prev)
                    f = jnp.where(tri, sa[:, BLOCK:], s_prev)
                    sink = sink_ref[ATTN_GROUP * c + 2 * a + par]
                    m = jnp.maximum(jnp.max(f, axis=1, keepdims=True), sink)
                    p = jnp.exp(f - m)
                    denoms.append(jnp.sum(p, axis=1, keepdims=True) + jnp.exp(sink - m))
                    probs.append(jnp.concatenate(
                        [jnp.where(tri, 0.0, p), jnp.where(tri, p, 0.0)], axis=1).astype(BF16))
            o = _dot(jnp.concatenate(probs, axis=0), vv)
            for a in range(pairs):
                o_even = o[a * BLOCK:(a + 1) * BLOCK] / denoms[a]
                o_odd = o[(pairs + a) * BLOCK:(pairs + a + 1) * BLOCK] / denoms[pairs + a]
                gcols = slice((pairs * c_in + a) * LANES, (pairs * c_in + a + 1) * LANES)
                ocols = slice((pairs * c + a) * LANES, (pairs * c + a + 1) * LANES)
                gated = jnp.where(low_o, o_even, o_odd) * g_ref[rows, gcols].astype(F32)
                y_scr[rows, ocols] = gated.astype(BF16)

    for n in range(D_MODEL // SUB_N):
        o_ref[:, _sub_cols(n)] = x_ref[:, _sub_cols(n)] + _dot(y_scr[...], w_ref[:, _sub_cols(n)])


def _odd_out(sinks, qg, kv, x2d, w_out, batch, seq):
    t = x2d.shape[0]
    tm = OUT_TM
    ns = seq // tm
    bpt = tm // BLOCK

    def row(b, i):
        return b * ns + i

    def qg_spec(col):
        return pl.BlockSpec((tm, ATTN_WIDTH // Q_STEPS), lambda b, i: (row(b, i), col))

    return pl.pallas_call(
        _odd_out_kernel,
        grid=(batch, ns),
        in_specs=[
            pl.BlockSpec(memory_space=pltpu.SMEM),
            qg_spec(0), qg_spec(1), qg_spec(2), qg_spec(3),
            pl.BlockSpec((tm, 4 * KV_WIDTH), lambda b, i: (row(b, i), 0)),
            pl.BlockSpec((BLOCK, 4 * KV_WIDTH),
                         lambda b, i: (jnp.maximum(row(b, i) * bpt - 1, 0), 0)),
            pl.BlockSpec((tm, D_MODEL), lambda b, i: (row(b, i), 0)),
            pl.BlockSpec((ATTN_WIDTH, D_MODEL), lambda b, i: (0, 0), pipeline_mode=pl.Buffered(1)),
        ],
        out_specs=pl.BlockSpec((tm, D_MODEL), lambda b, i: (row(b, i), 0)),
        out_shape=jax.ShapeDtypeStruct((t, D_MODEL), F32),
        scratch_shapes=[pltpu.VMEM((tm, ATTN_WIDTH), BF16)],
        compiler_params=pltpu.CompilerParams(
            dimension_semantics=("arbitrary", "arbitrary"),
            vmem_limit_bytes=VMEM_LIMIT),
        name="odd_out",
    )(sinks, qg, qg, qg, qg, kv, kv, x2d, w_out)


def _rope_cos_sin(seq, dim):
    inv = 1.0 / (ROPE_THETA ** (np.arange(0, dim, 2, dtype=np.float64) / dim))
    ang = np.arange(seq, dtype=np.float64)[:, None] * inv[None, :]
    return np.cos(ang), np.sin(ang)


def _retention_tables():
    h = np.arange(RET_HEADS, dtype=np.float64)
    log_g = np.log(1.0 - 2.0 ** (-5.0 - h))
    idx = np.arange(RET_CHUNK, dtype=np.float64)
    diff = idx[:, None] - idx[None, :]
    scale = RET_HEAD_DIM ** -0.5
    intra = np.where(diff >= 0, np.exp(log_g[:, None, None] * np.maximum(diff, 0.0)), 0.0)
    q_dec = np.exp(log_g[:, None] * (idx[None, :] + 1.0))
    k_dec = np.exp(log_g[:, None] * (RET_CHUNK - 1.0 - idx[None, :]))
    c_dec = np.exp(log_g * RET_CHUNK)
    wide = (RET_HEADS, RET_CHUNK, RET_HEAD_DIM)
    dmat = jnp.asarray(intra * scale, F32)
    qdec = jnp.asarray(np.broadcast_to(q_dec[:, :, None], wide), F32)
    kdec = jnp.asarray(np.broadcast_to(k_dec[:, :, None] * scale, wide), F32)
    cdec = jnp.asarray(np.broadcast_to(c_dec[:, None, None], (RET_HEADS, 1, RET_HEAD_DIM)), F32)
    return dmat, qdec, kdec, cdec


def _head_rope_tables(norm_w, seq, scale):
    half = ATTN_HEAD_DIM // 2
    reps = LANES // ATTN_HEAD_DIM
    cos, sin = _rope_cos_sin(seq, ATTN_HEAD_DIM)
    cos_l = jnp.asarray(np.tile(np.concatenate([cos, cos], axis=1), (1, reps)), F32)
    sin_l = jnp.asarray(np.tile(np.concatenate([-sin, sin], axis=1), (1, reps)), F32)
    w = norm_w.astype(F32)
    w_partner = jnp.concatenate([w[half:], w[:half]])
    ta = cos_l * jnp.tile(w * scale, reps)[None, :]
    tb = sin_l * jnp.tile(w_partner * scale, reps)[None, :]
    return ta, tb


def kernel(x, ev_norm_w, ev_w_in, ev_conv_w, ev_w_out, od_norm_w, od_w_in,
           od_q_norm_w, od_k_norm_w, od_sinks, od_w_out):
    batch, seq, d = x.shape
    assert d == D_MODEL and seq % IN_TM == 0 and seq % OUT_TM == 0
    x2d = x.reshape(batch * seq, d)

    cos, sin = _rope_cos_sin(seq, RET_HEAD_DIM)
    cw = jnp.pad(ev_conv_w[0], ((0, 8 - CONV_K), (0, 0)))
    qkvg, conv, w_out_e, w_in_o, w_out_o = _even_in(
        x2d, ev_norm_w[0][None, :], ev_w_in[0].astype(BF16),
        jnp.asarray(cos, F32), jnp.asarray(sin, F32), cw,
        (ev_w_out[0], od_w_in[0], od_w_out[0]), seq)
    dmat, qdec, kdec, cdec = _retention_tables()
    x2d, h2d = _even_out(qkvg, conv, x2d, w_out_e, dmat, qdec, kdec, cdec,
                         od_norm_w[0][None, :], batch, seq)

    qa, qb = _head_rope_tables(od_q_norm_w[0], seq, ATTN_HEAD_DIM ** -0.5)
    ka, kb = _head_rope_tables(od_k_norm_w[0], seq, 1.0)
    head_id = np.arange(MXU_N) // ATTN_HEAD_DIM
    ones_bd = jnp.asarray(head_id[:, None] == head_id[None, :], BF16)
    qg, kv = _odd_in(h2d, w_in_o, qa, qb, ka, kb, ones_bd, seq)
    x2d = _odd_out(od_sinks[0].astype(F32), qg, kv, x2d, w_out_o, batch, seq)
    return x2d.reshape(batch, seq, d)
```

```python
import functools

import numpy as np
import jax
import jax.numpy as jnp
from jax import lax
from jax.experimental import pallas as pl
from jax.experimental.pallas import tpu as pltpu

D_MODEL = 2048
RET_HEADS = 4
RET_HEAD_DIM = 256
RET_WIDTH = RET_HEADS * RET_HEAD_DIM
CONV_WIDTH = D_MODEL - RET_WIDTH
CONV_K = 3
RET_CHUNK = 256
EVEN_IN = 4 * RET_WIDTH + 4 * CONV_WIDTH
ATTN_HEAD_DIM = 64
ATTN_Q_HEADS = 32
ATTN_KV_HEADS = 4
ATTN_GROUP = ATTN_Q_HEADS // ATTN_KV_HEADS
ATTN_WIDTH = ATTN_Q_HEADS * ATTN_HEAD_DIM
KV_WIDTH = ATTN_KV_HEADS * ATTN_HEAD_DIM
BLOCK = 128
ROPE_THETA = 10000.0
EPS = 1e-6
NEG = -1e30

LANES = 128
MXU_N = 256
SUB_N = 512
N_SUB = 4
Q_STEPS = 2
KV_STEP = 1
IN_TM = 1024
IN_TN = N_SUB * SUB_N
OUT_TM = 512
HALO = 8
VMEM_LIMIT = 56 * 1024 * 1024

F32 = jnp.float32
BF16 = jnp.bfloat16


def _dot(a, b):
    return jnp.dot(a, b, preferred_element_type=F32)


def _dot_nt(a, b):
    return lax.dot_general(a, b, (((1,), (1,)), ((), ())), preferred_element_type=F32)


def _dot_tn(a, b):
    return lax.dot_general(a, b, (((0,), (0,)), ((), ())), preferred_element_type=F32)


def _silu(g):
    return (0.5 * g) * (1.0 + jnp.tanh(0.5 * g))


def _rms_rows(x, nw):
    ms = jnp.mean(x * x, axis=-1, keepdims=True)
    return x * lax.rsqrt(ms + EPS) * nw


def _sub_cols(s):
    return slice(s * SUB_N, (s + 1) * SUB_N)


def _even_in_kernel(x_ref, nw_ref, w0_ref, w1_ref, w2_ref, w3_ref,
                    cos_ref, sin_ref, cw_ref, c0_ref, c1_ref, c2_ref,
                    qkvg_ref, conv_ref, c0_out, c1_out, c2_out,
                    h_cur, h_next, cu_scr, tail_scr, *, seq_tiles):
    i = pl.program_id(0)
    j = pl.program_id(1)
    tm = h_cur.shape[0]
    chunk = x_ref.shape[0]
    w_refs = (w0_ref, w1_ref, w2_ref, w3_ref)

    for src, dst in ((c0_ref, c0_out), (c1_ref, c1_out), (c2_ref, c2_out)):
        dst[...] = src[...].astype(BF16)

    def norm_next(step=None):
        if step is None:
            rows = pl.ds(pl.multiple_of(j * chunk, chunk), chunk)
        else:
            rows = slice(step * chunk, (step + 1) * chunk)
        h_next[rows, :] = _rms_rows(x_ref[...], nw_ref[...]).astype(BF16)

    def sub_dot(s):
        return _dot(h_cur[...], w_refs[s][...])

    @pl.when(i == 0)
    def _():
        norm_next()
        tail_scr[...] = jnp.zeros_like(tail_scr)

    projecting = i >= 1

    def rope_then(tail_fn, step):
        if step == 0:
            for kb in range(D_MODEL // MXU_N):
                h_cur[:, kb * MXU_N:(kb + 1) * MXU_N] = h_next[:, kb * MXU_N:(kb + 1) * MXU_N]
        norm_next(step)
        cos = cos_ref[...]
        sin = sin_ref[...]
        for s in range(N_SUB // 2):
            acc = sub_dot(s)
            for hd in range(SUB_N // RET_HEAD_DIM):
                base = s * SUB_N + hd * RET_HEAD_DIM
                x1 = acc[:, hd * RET_HEAD_DIM:hd * RET_HEAD_DIM + LANES]
                x2 = acc[:, hd * RET_HEAD_DIM + LANES:(hd + 1) * RET_HEAD_DIM]
                qkvg_ref[:, base:base + LANES] = (x1 * cos - x2 * sin).astype(BF16)
                qkvg_ref[:, base + LANES:base + RET_HEAD_DIM] = (x2 * cos + x1 * sin).astype(BF16)
        for s in range(N_SUB // 2, N_SUB):
            qkvg_ref[:, _sub_cols(s)] = tail_fn(sub_dot(s)).astype(BF16)

    @pl.when(jnp.logical_and(projecting, j == 0))
    def _():
        rope_then(lambda v: v, 0)

    @pl.when(jnp.logical_and(projecting, j == 1))
    def _():
        rope_then(_silu, 1)

    @pl.when(jnp.logical_and(projecting, j >= 2))
    def _():
        cs = j - 2
        norm_next()
        cu = sub_dot(0) * sub_dot(1)
        at_seq_start = ((i - 1) % seq_tiles) == 0
        halo_rows = slice(HALO - (CONV_K - 1), HALO)
        cu_scr[halo_rows, :] = jnp.where(at_seq_start, 0.0, tail_scr[cs, halo_rows, :])
        cu_scr[HALO:HALO + tm, :] = cu
        tail_scr[cs, halo_rows, :] = cu_scr[HALO + tm - (CONV_K - 1):HALO + tm, :]
        cw = cw_ref[...]
        conv = cw[CONV_K - 1:CONV_K, :] * cu
        for t in range(CONV_K - 1):
            shift = CONV_K - 1 - t
            conv = conv + cw[t:t + 1, :] * cu_scr[HALO - shift:HALO - shift + tm, :]
        gated = sub_dot(2) * conv
        conv_ref[...] = (gated * _silu(sub_dot(3))).astype(BF16)


def _even_in(x2d, nw, w, cos, sin, cw, to_cast, seq):
    t = x2d.shape[0]
    tm = IN_TM
    n_tiles = t // tm
    seq_tiles = seq // tm
    n_ret_steps = 4 * RET_WIDTH // IN_TN
    n_conv_steps = CONV_WIDTH // SUB_N
    nj = n_ret_steps + n_conv_steps
    chunk = tm // nj
    n_proj_steps = n_tiles * nj
    conv_first = tuple((4 * RET_WIDTH + b * CONV_WIDTH) // SUB_N for b in (1, 2, 0, 3))

    def tile(i):
        return jnp.maximum(i - 1, 0)

    def step(i, j):
        return jnp.where(i == 0, 0, j)

    def w_spec(s):
        def index(i, j):
            jj = step(i, j)
            half = N_SUB // 2
            ret_idx = half * jnp.minimum(jj, n_ret_steps - 1) + (s % half) + (s // half) * N_SUB
            conv_idx = conv_first[s] + jj - n_ret_steps
            return 0, jnp.where(jj < n_ret_steps, ret_idx, conv_idx)
        return pl.BlockSpec((D_MODEL, SUB_N), index)

    def cast_spec(a):
        rows = a.shape[0] // n_proj_steps
        assert rows * n_proj_steps == a.shape[0] and rows % 16 == 0
        return pl.BlockSpec((rows, a.shape[1]), lambda i, j: (tile(i) * nj + step(i, j), 0))

    def table_spec():
        return pl.BlockSpec((tm, LANES), lambda i, j: (tile(i) % seq_tiles, 0))

    kern = functools.partial(_even_in_kernel, seq_tiles=seq_tiles)
    return pl.pallas_call(
        kern,
        grid=(n_tiles + 1, nj),
        in_specs=[
            pl.BlockSpec((chunk, D_MODEL), lambda i, j: (jnp.minimum(i, n_tiles - 1) * nj + j, 0)),
            pl.BlockSpec((1, D_MODEL), lambda i, j: (0, 0), pipeline_mode=pl.Buffered(1)),
            w_spec(0), w_spec(1), w_spec(2), w_spec(3),
            table_spec(), table_spec(),
            pl.BlockSpec((8, SUB_N), lambda i, j: (0, jnp.maximum(step(i, j) - n_ret_steps, 0))),
        ] + [cast_spec(a) for a in to_cast],
        out_specs=[
            pl.BlockSpec((tm, IN_TN),
                         lambda i, j: (tile(i), jnp.minimum(step(i, j), n_ret_steps - 1))),
            pl.BlockSpec((tm, SUB_N),
                         lambda i, j: (tile(i), jnp.maximum(step(i, j) - n_ret_steps, 0))),
        ] + [cast_spec(a) for a in to_cast],
        out_shape=[
            jax.ShapeDtypeStruct((t, 4 * RET_WIDTH), BF16),
            jax.ShapeDtypeStruct((t, CONV_WIDTH), BF16),
        ] + [jax.ShapeDtypeStruct(a.shape, BF16) for a in to_cast],
        scratch_shapes=[
            pltpu.VMEM((tm, D_MODEL), BF16),
            pltpu.VMEM((tm, D_MODEL), BF16),
            pltpu.VMEM((HALO + tm, SUB_N), F32),
            pltpu.VMEM((n_conv_steps, HALO, SUB_N), F32),
        ],
        compiler_params=pltpu.CompilerParams(
            dimension_semantics=("arbitrary", "arbitrary"),
            vmem_limit_bytes=VMEM_LIMIT),
        name="even_in",
    )(x2d, nw, w, w, w, w, cos, sin, cw, *to_cast)


def _even_out_kernel(q_ref, k_ref, v_ref, g_ref, conv_ref, x_ref, w_ref,
                     dmat_ref, qdec_ref, kdec_ref, cdec_ref, nw_next_ref,
                     o_ref, h_next_ref, state_scr, y_scr):
    tm = x_ref.shape[0]

    @pl.when(pl.program_id(1) == 0)
    def _():
        state_scr[...] = jnp.zeros_like(state_scr)

    for c in range(tm // RET_CHUNK):
        rows = slice(c * RET_CHUNK, (c + 1) * RET_CHUNK)
        for h in range(RET_HEADS):
            cols = slice(h * RET_HEAD_DIM, (h + 1) * RET_HEAD_DIM)
            qh = q_ref[rows, cols]
            kh = k_ref[rows, cols]
            vh = v_ref[rows, cols]
            scores = _dot_nt(qh, kh) * dmat_ref[h]
            inner = _dot(scores.astype(BF16), vh)
            state = state_scr[h]
            cross = _dot(qh, state.astype(BF16)) * qdec_ref[h]
            o = inner + cross
            kd = (kh.astype(F32) * kdec_ref[h]).astype(BF16)
            state_scr[h] = cdec_ref[h] * state + _dot_tn(kd, vh)
            ms = jnp.mean(o * o, axis=-1, keepdims=True)
            ro = o * lax.rsqrt(ms + EPS) * g_ref[rows, cols].astype(F32)
            y_scr[rows, cols] = ro.astype(BF16)
    y_scr[:, RET_WIDTH:] = conv_ref[...]

    ssq = jnp.zeros((tm, 1), F32)
    for n in range(D_MODEL // SUB_N):
        xn = x_ref[:, _sub_cols(n)] + _dot(y_scr[...], w_ref[:, _sub_cols(n)])
        o_ref[:, _sub_cols(n)] = xn
        ssq = ssq + jnp.sum(xn * xn, axis=-1, keepdims=True)
    r = lax.rsqrt(ssq * (1.0 / D_MODEL) + EPS)
    for n in range(D_MODEL // SUB_N):
        h_next_ref[:, _sub_cols(n)] = (o_ref[:, _sub_cols(n)] * r * nw_next_ref[:, _sub_cols(n)]).astype(BF16)


def _even_out(qkvg, conv, x2d, w_out, dmat, qdec, kdec, cdec, nw_next, batch, seq):
    t = x2d.shape[0]
    tm = OUT_TM
    ns = seq // tm

    def row(b, i):
        return b * ns + i

    def qkvg_spec(col):
        return pl.BlockSpec((tm, RET_WIDTH), lambda b, i: (row(b, i), col))

    def const_spec(shape):
        return pl.BlockSpec(shape, lambda b, i: (0,) * len(shape), pipeline_mode=pl.Buffered(1))

    return pl.pallas_call(
        _even_out_kernel,
        grid=(batch, ns),
        in_specs=[
            qkvg_spec(0), qkvg_spec(2), qkvg_spec(1), qkvg_spec(3),
            pl.BlockSpec((tm, CONV_WIDTH), lambda b, i: (row(b, i), 0)),
            pl.BlockSpec((tm, D_MODEL), lambda b, i: (row(b, i), 0)),
            const_spec((D_MODEL, D_MODEL)),
            const_spec(dmat.shape), const_spec(qdec.shape),
            const_spec(kdec.shape), const_spec(cdec.shape), const_spec(nw_next.shape),
        ],
        out_specs=[
            pl.BlockSpec((tm, D_MODEL), lambda b, i: (row(b, i), 0)),
            pl.BlockSpec((tm, D_MODEL), lambda b, i: (row(b, i), 0)),
        ],
        out_shape=[
            jax.ShapeDtypeStruct((t, D_MODEL), F32),
            jax.ShapeDtypeStruct((t, D_MODEL), BF16),
        ],
        scratch_shapes=[
            pltpu.VMEM((RET_HEADS, RET_HEAD_DIM, RET_HEAD_DIM), F32),
            pltpu.VMEM((tm, D_MODEL), BF16),
        ],
        compiler_params=pltpu.CompilerParams(
            dimension_semantics=("arbitrary", "arbitrary"),
            vmem_limit_bytes=VMEM_LIMIT),
        name="even_out",
    )(qkvg, qkvg, qkvg, qkvg, conv, x2d, w_out, dmat, qdec, kdec, cdec, nw_next)


def _odd_in_kernel(h_ref, w0_ref, w1_ref, w2_ref, w3_ref, wkv_ref,
                   cos_ref, sin_ref, qgain_ref, kgain_ref, ones_ref,
                   qg_ref, kv_ref):
    j = pl.program_id(1)
    w_refs = (w0_ref, w1_ref, w2_ref, w3_ref)

    def sub_dot(w_ref):
        return _dot(h_ref[...], w_ref[...])

    def dup_heads(x):
        lane = lax.broadcasted_iota(jnp.int32, (x.shape[0], LANES), 1)
        low = lane < ATTN_HEAD_DIM
        blocks = []
        for b in range(x.shape[1] // LANES):
            xb = x[:, b * LANES:(b + 1) * LANES]
            swapped = pltpu.roll(xb, ATTN_HEAD_DIM, axis=1)
            blocks += [jnp.where(low, xb, swapped), jnp.where(low, swapped, xb)]
        return jnp.concatenate(blocks, axis=1)

    def norm_rope(acc, gain_ref):
        width = acc.shape[1]
        sq = (acc * acc).astype(BF16)
        ss = jnp.concatenate(
            [_dot(sq[:, c * MXU_N:(c + 1) * MXU_N], ones_ref[...]) for c in range(width // MXU_N)],
            axis=1)
        r = lax.rsqrt(ss * (1.0 / ATTN_HEAD_DIM) + EPS)
        half = ATTN_HEAD_DIM // 2
        lane = lax.broadcasted_iota(jnp.int32, acc.shape, 1)
        first_half = (lane % ATTN_HEAD_DIM) < half
        partner = jnp.where(first_half,
                            pltpu.roll(acc, width - half, axis=1),
                            pltpu.roll(acc, half, axis=1))
        reps = width // LANES
        ta = cos_ref[...] * gain_ref[0:1, :]
        tb = sin_ref[...] * gain_ref[1:2, :]
        ta_w = jnp.concatenate([ta] * reps, axis=1)
        tb_w = jnp.concatenate([tb] * reps, axis=1)
        return r * (acc * ta_w + partner * tb_w)

    @pl.when(j != KV_STEP)
    def _():
        for s in range(N_SUB // 2):
            qg_ref[:, _sub_cols(s)] = norm_rope(sub_dot(w_refs[s]), qgain_ref).astype(BF16)
        for s in range(N_SUB // 2, N_SUB):
            qg_ref[:, _sub_cols(s)] = _silu(sub_dot(w_refs[s])).astype(BF16)

    @pl.when(j == KV_STEP)
    def _():
        acc = sub_dot(wkv_ref)
        k = norm_rope(acc[:, :KV_WIDTH], kgain_ref)
        kv_ref[:, :2 * KV_WIDTH] = dup_heads(k).astype(BF16)
        kv_ref[:, 2 * KV_WIDTH:] = dup_heads(acc[:, KV_WIDTH:]).astype(BF16)


def _odd_in(h2d, w, cos_l, sin_l, q_gain, k_gain, ones_bd, seq):
    t = h2d.shape[0]
    tm = IN_TM
    seq_tiles = seq // tm
    half = N_SUB // 2
    assert ATTN_WIDTH == Q_STEPS * half * SUB_N
    gate_col0 = (ATTN_WIDTH + 2 * KV_WIDTH) // SUB_N

    def w_spec(s):
        def index(i, j):
            group = jnp.where(j < KV_STEP, 0, 1)
            return 0, half * group + (s % half) + (s // half) * gate_col0
        return pl.BlockSpec((D_MODEL, SUB_N), index)

    def tab_spec():
        return pl.BlockSpec((tm, LANES), lambda i, j: (i % seq_tiles, 0))

    def resident_spec(shape, col):
        return pl.BlockSpec(shape, lambda i, j: (0, col), pipeline_mode=pl.Buffered(1))

    return pl.pallas_call(
        _odd_in_kernel,
        grid=(t // tm, Q_STEPS + 1),
        in_specs=[
            pl.BlockSpec((tm, D_MODEL), lambda i, j: (i, 0)),
            w_spec(0), w_spec(1), w_spec(2), w_spec(3),
            resident_spec((D_MODEL, SUB_N), ATTN_WIDTH // SUB_N),
            tab_spec(), tab_spec(),
            resident_spec(q_gain.shape, 0), resident_spec(k_gain.shape, 0),
            resident_spec((MXU_N, MXU_N), 0),
        ],
        out_specs=[
            pl.BlockSpec((tm, IN_TN), lambda i, j: (i, jnp.where(j <= KV_STEP, 0, 1))),
            pl.BlockSpec((tm, 4 * KV_WIDTH), lambda i, j: (i, 0)),
        ],
        out_shape=[
            jax.ShapeDtypeStruct((t, 2 * ATTN_WIDTH), BF16),
            jax.ShapeDtypeStruct((t, 4 * KV_WIDTH), BF16),
        ],
        compiler_params=pltpu.CompilerParams(
            dimension_semantics=("arbitrary", "arbitrary"),
            vmem_limit_bytes=VMEM_LIMIT),
        name="odd_in",
    )(h2d, w, w, w, w, w, cos_l, sin_l, q_gain, k_gain, ones_bd)


def _odd_out_kernel(sink_ref, q0_ref, g0_ref, q1_ref, g1_ref, kv_ref, kvp_ref, x_ref, w_ref,
                    o_ref, y_scr):
    tm = x_ref.shape[0]
    q_refs = (q0_ref, q1_ref)
    g_refs = (g0_ref, g1_ref)
    kv_per_group = ATTN_KV_HEADS // Q_STEPS
    at_seq_start = pl.program_id(1) == 0
    pairs = ATTN_GROUP // 2
    half_w = 2 * KV_WIDTH

    lane_kv = lax.broadcasted_iota(jnp.int32, (2 * BLOCK, LANES), 1)
    low_kv = lane_kv < ATTN_HEAD_DIM
    rr = lax.broadcasted_iota(jnp.int32, (BLOCK, BLOCK), 0)
    cc = lax.broadcasted_iota(jnp.int32, (BLOCK, BLOCK), 1)
    tri = cc <= rr
    low_o = cc < ATTN_HEAD_DIM

    for bl in range(tm // BLOCK):
        rows = slice(bl * BLOCK, (bl + 1) * BLOCK)
        for c in range(ATTN_KV_HEADS):
            kcols = slice(c * LANES, (c + 1) * LANES)
            vcols = slice(half_w + c * LANES, half_w + (c + 1) * LANES)
            if bl == 0:
                k_prev = kvp_ref[:, kcols]
                v_prev = kvp_ref[:, vcols]
            else:
                prows = slice((bl - 1) * BLOCK, bl * BLOCK)
                k_prev = kv_ref[prows, kcols]
                v_prev = kv_ref[prows, vcols]
            kk = jnp.concatenate([k_prev, kv_ref[rows, kcols]], axis=0)
            vv = jnp.concatenate([v_prev, kv_ref[rows, vcols]], axis=0)
            zero = jnp.zeros_like(kk)
            k_par = (jnp.where(low_kv, kk, zero), jnp.where(low_kv, zero, kk))
            q_ref = q_refs[c // kv_per_group]
            g_ref = g_refs[c // kv_per_group]
            c_in = c % kv_per_group
            qs = jnp.concatenate(
                [q_ref[rows, (pairs * c_in + a) * LANES:(pairs * c_in + a + 1) * LANES]
                 for a in range(pairs)], axis=0)
            probs = []
            denoms = []
            for par in range(2):
                s = _dot_nt(qs, k_par[par])
                for a in range(pairs):
                    sa = s[a * BLOCK:(a + 1) * BLOCK]
                    s_prev = sa[:, :BLOCK]
                    if bl == 0:
                        s_prev = jnp.where(at_seq_start, NEG, s_prev)
                    f = jnp.where(tri, sa[:, BLOCK:], s_prev)
                    sink = sink_ref[ATTN_GROUP * c + 2 * a + par]
                    m = jnp.maximum(jnp.max(f, axis=1, keepdims=True), sink)
                    p = jnp.exp(f - m)
                    denoms.append(jnp.sum(p, axis=1, keepdims=True) + jnp.exp(sink - m))
                    probs.append(jnp.concatenate(
                        [jnp.where(tri, 0.0, p), jnp.where(tri, p, 0.0)], axis=1).astype(BF16))
            o = _dot(jnp.concatenate(probs, axis=0), vv)
            for a in range(pairs):
                o_even = o[a * BLOCK:(a + 1) * BLOCK] / denoms[a]
                o_odd = o[(pairs + a) * BLOCK:(pairs + a + 1) * BLOCK] / denoms[pairs + a]
                gcols = slice((pairs * c_in + a) * LANES, (pairs * c_in + a + 1) * LANES)
                ocols = slice((pairs * c + a) * LANES, (pairs * c + a + 1) * LANES)
                gated = jnp.where(low_o, o_even, o_odd) * g_ref[rows, gcols].astype(F32)
                y_scr[rows, ocols] = gated.astype(BF16)

    for n in range(D_MODEL // SUB_N):
        o_ref[:, _sub_cols(n)] = x_ref[:, _sub_cols(n)] + _dot(y_scr[...], w_ref[:, _sub_cols(n)])


def _odd_out(sinks, qg, kv, x2d, w_out, batch, seq):
    t = x2d.shape[0]
    tm = OUT_TM
    ns = seq // tm
    bpt = tm // BLOCK

    def row(b, i):
        return b * ns + i

    def qg_spec(col):
        return pl.BlockSpec((tm, ATTN_WIDTH // Q_STEPS), lambda b, i: (row(b, i), col))

    return pl.pallas_call(
        _odd_out_kernel,
        grid=(batch, ns),
        in_specs=[
            pl.BlockSpec(memory_space=pltpu.SMEM),
            qg_spec(0), qg_spec(1), qg_spec(2), qg_spec(3),
            pl.BlockSpec((tm, 4 * KV_WIDTH), lambda b, i: (row(b, i), 0)),
            pl.BlockSpec((BLOCK, 4 * KV_WIDTH),
                         lambda b, i: (jnp.maximum(row(b, i) * bpt - 1, 0), 0)),
            pl.BlockSpec((tm, D_MODEL), lambda b, i: (row(b, i), 0)),
            pl.BlockSpec((ATTN_WIDTH, D_MODEL), lambda b, i: (0, 0), pipeline_mode=pl.Buffered(1)),
        ],
        out_specs=pl.BlockSpec((tm, D_MODEL), lambda b, i: (row(b, i), 0)),
        out_shape=jax.ShapeDtypeStruct((t, D_MODEL), F32),
        scratch_shapes=[pltpu.VMEM((tm, ATTN_WIDTH), BF16)],
        compiler_params=pltpu.CompilerParams(
            dimension_semantics=("arbitrary", "arbitrary"),
            vmem_limit_bytes=VMEM_LIMIT),
        name="odd_out",
    )(sinks, qg, qg, qg, qg, kv, kv, x2d, w_out)


def _rope_cos_sin(seq, dim):
    inv = 1.0 / (ROPE_THETA ** (np.arange(0, dim, 2, dtype=np.float64) / dim))
    ang = np.arange(seq, dtype=np.float64)[:, None] * inv[None, :]
    return np.cos(ang), np.sin(ang)


def _retention_tables():
    h = np.arange(RET_HEADS, dtype=np.float64)
    log_g = np.log(1.0 - 2.0 ** (-5.0 - h))
    idx = np.arange(RET_CHUNK, dtype=np.float64)
    diff = idx[:, None] - idx[None, :]
    scale = RET_HEAD_DIM ** -0.5
    intra = np.where(diff >= 0, np.exp(log_g[:, None, None] * np.maximum(diff, 0.0)), 0.0)
    q_dec = np.exp(log_g[:, None] * (idx[None, :] + 1.0))
    k_dec = np.exp(log_g[:, None] * (RET_CHUNK - 1.0 - idx[None, :]))
    c_dec = np.exp(log_g * RET_CHUNK)
    wide = (RET_HEADS, RET_CHUNK, RET_HEAD_DIM)
    dmat = jnp.asarray(intra * scale, F32)
    qdec = jnp.asarray(np.broadcast_to(q_dec[:, :, None], wide), F32)
    kdec = jnp.asarray(np.broadcast_to(k_dec[:, :, None] * scale, wide), F32)
    cdec = jnp.asarray(np.broadcast_to(c_dec[:, None, None], (RET_HEADS, 1, RET_HEAD_DIM)), F32)
    return dmat, qdec, kdec, cdec


def _head_rope_tables(seq):
    reps = LANES // ATTN_HEAD_DIM
    cos, sin = _rope_cos_sin(seq, ATTN_HEAD_DIM)
    cos_l = jnp.asarray(np.tile(np.concatenate([cos, cos], axis=1), (1, reps)), F32)
    sin_l = jnp.asarray(np.tile(np.concatenate([-sin, sin], axis=1), (1, reps)), F32)
    return cos_l, sin_l


def _head_gain_rows(norm_w, scale):
    half = ATTN_HEAD_DIM // 2
    reps = LANES // ATTN_HEAD_DIM
    w = norm_w.astype(F32) * scale
    w_partner = jnp.concatenate([w[half:], w[:half]])
    rows = jnp.stack([jnp.tile(w, reps), jnp.tile(w_partner, reps)])
    return jnp.pad(rows, ((0, 8 - rows.shape[0]), (0, 0)))


def kernel(x, ev_norm_w, ev_w_in, ev_conv_w, ev_w_out, od_norm_w, od_w_in,
           od_q_norm_w, od_k_norm_w, od_sinks, od_w_out):
    batch, seq, d = x.shape
    assert d == D_MODEL and seq % IN_TM == 0 and seq % OUT_TM == 0
    x2d = x.reshape(batch * seq, d)

    cos, sin = _rope_cos_sin(seq, RET_HEAD_DIM)
    cw = jnp.pad(ev_conv_w[0], ((0, 8 - CONV_K), (0, 0)))
    qkvg, conv, w_out_e, w_in_o, w_out_o = _even_in(
        x2d, ev_norm_w[0][None, :], ev_w_in[0].astype(BF16),
        jnp.asarray(cos, F32), jnp.asarray(sin, F32), cw,
        (ev_w_out[0], od_w_in[0], od_w_out[0]), seq)
    dmat, qdec, kdec, cdec = _retention_tables()
    x2d, h2d = _even_out(qkvg, conv, x2d, w_out_e, dmat, qdec, kdec, cdec,
                         od_norm_w[0][None, :], batch, seq)

    cos_l, sin_l = _head_rope_tables(seq)
    q_gain = _head_gain_rows(od_q_norm_w[0], ATTN_HEAD_DIM ** -0.5)
    k_gain = _head_gain_rows(od_k_norm_w[0], 1.0)
    head_id = np.arange(MXU_N) // ATTN_HEAD_DIM
    ones_bd = jnp.asarray(head_id[:, None] == head_id[None, :], BF16)
    qg, kv = _odd_in(h2d, w_in_o, cos_l, sin_l, q_gain, k_gain, ones_bd, seq)
    x2d = _odd_out(od_sinks[0].astype(F32), qg, kv, x2d, w_out_o, batch, seq)
    return x2d.reshape(batch, seq, d)
```

```python
import functools

import numpy as np
import jax
import jax.numpy as jnp
from jax import lax
from jax.experimental import pallas as pl
from jax.experimental.pallas import tpu as pltpu

D_MODEL = 2048
RET_HEADS = 4
RET_HEAD_DIM = 256
RET_WIDTH = RET_HEADS * RET_HEAD_DIM
CONV_WIDTH = D_MODEL - RET_WIDTH
CONV_K = 3
RET_CHUNK = 256
EVEN_IN = 4 * RET_WIDTH + 4 * CONV_WIDTH
ATTN_HEAD_DIM = 64
ATTN_Q_HEADS = 32
ATTN_KV_HEADS = 4
ATTN_GROUP = ATTN_Q_HEADS // ATTN_KV_HEADS
ATTN_WIDTH = ATTN_Q_HEADS * ATTN_HEAD_DIM
KV_WIDTH = ATTN_KV_HEADS * ATTN_HEAD_DIM
BLOCK = 128
ROPE_THETA = 10000.0
EPS = 1e-6
NEG = -1e30

LANES = 128
SUBLANES = 8
BF16_ROWS = 16
MXU_N = 256
SUB_N = 512
N_SUB = 4
Q_STEPS = 2
IN_TM = 1024
IN_TN = N_SUB * SUB_N
OUT_TM = 512
HALO = SUBLANES
V7X_VMEM_BYTES = 64 * 1024 * 1024
VMEM_LIMIT = V7X_VMEM_BYTES * 7 // 8

F32 = jnp.float32
BF16 = jnp.bfloat16


def _dot(a, b):
    return jnp.dot(a, b, preferred_element_type=F32)


def _dot_nt(a, b):
    return lax.dot_general(a, b, (((1,), (1,)), ((), ())), preferred_element_type=F32)


def _dot_tn(a, b):
    return lax.dot_general(a, b, (((0,), (0,)), ((), ())), preferred_element_type=F32)


def _silu(g):
    return (0.5 * g) * (1.0 + jnp.tanh(0.5 * g))


def _rms_rows(x, nw):
    ms = jnp.mean(x * x, axis=-1, keepdims=True)
    return x * lax.rsqrt(ms + EPS) * nw


def _sub_cols(s):
    return slice(s * SUB_N, (s + 1) * SUB_N)


def _even_in_kernel(x_ref, nw_ref, w0_ref, w1_ref, w2_ref, w3_ref,
                    cos_ref, sin_ref, cw_ref, c0_ref, c1_ref, c2_ref,
                    qkvg_ref, conv_ref, c0_out, c1_out, c2_out,
                    h_cur, h_next, cu_scr, tail_scr, *, seq_tiles):
    i = pl.program_id(0)
    j = pl.program_id(1)
    tm = h_cur.shape[0]
    chunk = x_ref.shape[0]
    w_refs = (w0_ref, w1_ref, w2_ref, w3_ref)

    for src, dst in ((c0_ref, c0_out), (c1_ref, c1_out), (c2_ref, c2_out)):
        dst[...] = src[...].astype(BF16)

    def norm_next(step=None):
        if step is None:
            rows = pl.ds(pl.multiple_of(j * chunk, chunk), chunk)
        else:
            rows = slice(step * chunk, (step + 1) * chunk)
        h_next[rows, :] = _rms_rows(x_ref[...], nw_ref[...]).astype(BF16)

    def sub_dot(s):
        return _dot(h_cur[...], w_refs[s][...])

    @pl.when(i == 0)
    def _():
        norm_next()
        tail_scr[...] = jnp.zeros_like(tail_scr)

    projecting = i >= 1

    def rope_then(tail_fn, step):
        if step == 0:
            for kb in range(D_MODEL // MXU_N):
                h_cur[:, kb * MXU_N:(kb + 1) * MXU_N] = h_next[:, kb * MXU_N:(kb + 1) * MXU_N]
        norm_next(step)
        cos = cos_ref[...]
        sin = sin_ref[...]
        for s in range(N_SUB // 2):
            acc = sub_dot(s)
            for hd in range(SUB_N // RET_HEAD_DIM):
                base = s * SUB_N + hd * RET_HEAD_DIM
                x1 = acc[:, hd * RET_HEAD_DIM:hd * RET_HEAD_DIM + LANES]
                x2 = acc[:, hd * RET_HEAD_DIM + LANES:(hd + 1) * RET_HEAD_DIM]
                qkvg_ref[:, base:base + LANES] = (x1 * cos - x2 * sin).astype(BF16)
                qkvg_ref[:, base + LANES:base + RET_HEAD_DIM] = (x2 * cos + x1 * sin).astype(BF16)
        for s in range(N_SUB // 2, N_SUB):
            qkvg_ref[:, _sub_cols(s)] = tail_fn(sub_dot(s)).astype(BF16)

    @pl.when(jnp.logical_and(projecting, j == 0))
    def _():
        rope_then(lambda v: v, 0)

    @pl.when(jnp.logical_and(projecting, j == 1))
    def _():
        rope_then(_silu, 1)

    @pl.when(jnp.logical_and(projecting, j >= 2))
    def _():
        cs = j - 2
        norm_next()
        cu = sub_dot(0) * sub_dot(1)
        at_seq_start = ((i - 1) % seq_tiles) == 0
        halo_rows = slice(HALO - (CONV_K - 1), HALO)
        cu_scr[halo_rows, :] = jnp.where(at_seq_start, 0.0, tail_scr[cs, halo_rows, :])
        cu_scr[HALO:HALO + tm, :] = cu
        tail_scr[cs, halo_rows, :] = cu_scr[HALO + tm - (CONV_K - 1):HALO + tm, :]
        cw = cw_ref[...]
        conv = cw[CONV_K - 1:CONV_K, :] * cu
        for t in range(CONV_K - 1):
            shift = CONV_K - 1 - t
            conv = conv + cw[t:t + 1, :] * cu_scr[HALO - shift:HALO - shift + tm, :]
        gated = sub_dot(2) * conv
        conv_ref[...] = (gated * _silu(sub_dot(3))).astype(BF16)


def _even_in(x2d, nw, w, cos, sin, cw, to_cast, seq):
    t = x2d.shape[0]
    tm = IN_TM
    n_tiles = t // tm
    seq_tiles = seq // tm
    n_ret_steps = 4 * RET_WIDTH // IN_TN
    n_conv_steps = CONV_WIDTH // SUB_N
    nj = n_ret_steps + n_conv_steps
    chunk = tm // nj
    n_proj_steps = n_tiles * nj
    conv_first = tuple((4 * RET_WIDTH + b * CONV_WIDTH) // SUB_N for b in (1, 2, 0, 3))

    def tile(i):
        return jnp.maximum(i - 1, 0)

    def step(i, j):
        return jnp.where(i == 0, 0, j)

    def w_spec(s):
        def index(i, j):
            jj = step(i, j)
            half = N_SUB // 2
            ret_idx = half * jnp.minimum(jj, n_ret_steps - 1) + (s % half) + (s // half) * N_SUB
            conv_idx = conv_first[s] + jj - n_ret_steps
            return 0, jnp.where(jj < n_ret_steps, ret_idx, conv_idx)
        return pl.BlockSpec((D_MODEL, SUB_N), index)

    def cast_spec(a):
        rows = a.shape[0] // n_proj_steps
        assert rows * n_proj_steps == a.shape[0] and rows % BF16_ROWS == 0
        return pl.BlockSpec((rows, a.shape[1]), lambda i, j: (tile(i) * nj + step(i, j), 0))

    def table_spec():
        return pl.BlockSpec((tm, LANES), lambda i, j: (tile(i) % seq_tiles, 0))

    kern = functools.partial(_even_in_kernel, seq_tiles=seq_tiles)
    return pl.pallas_call(
        kern,
        grid=(n_tiles + 1, nj),
        in_specs=[
            pl.BlockSpec((chunk, D_MODEL), lambda i, j: (jnp.minimum(i, n_tiles - 1) * nj + j, 0)),
            pl.BlockSpec((1, D_MODEL), lambda i, j: (0, 0), pipeline_mode=pl.Buffered(1)),
            w_spec(0), w_spec(1), w_spec(2), w_spec(3),
            table_spec(), table_spec(),
            pl.BlockSpec((SUBLANES, SUB_N), lambda i, j: (0, jnp.maximum(step(i, j) - n_ret_steps, 0))),
        ] + [cast_spec(a) for a in to_cast],
        out_specs=[
            pl.BlockSpec((tm, IN_TN),
                         lambda i, j: (tile(i), jnp.minimum(step(i, j), n_ret_steps - 1))),
            pl.BlockSpec((tm, SUB_N),
                         lambda i, j: (tile(i), jnp.maximum(step(i, j) - n_ret_steps, 0))),
        ] + [cast_spec(a) for a in to_cast],
        out_shape=[
            jax.ShapeDtypeStruct((t, 4 * RET_WIDTH), BF16),
            jax.ShapeDtypeStruct((t, CONV_WIDTH), BF16),
        ] + [jax.ShapeDtypeStruct(a.shape, BF16) for a in to_cast],
        scratch_shapes=[
            pltpu.VMEM((tm, D_MODEL), BF16),
            pltpu.VMEM((tm, D_MODEL), BF16),
            pltpu.VMEM((HALO + tm, SUB_N), F32),
            pltpu.VMEM((n_conv_steps, HALO, SUB_N), F32),
        ],
        compiler_params=pltpu.CompilerParams(
            dimension_semantics=("arbitrary", "arbitrary"),
            vmem_limit_bytes=VMEM_LIMIT),
        name="even_in",
    )(x2d, nw, w, w, w, w, cos, sin, cw, *to_cast)


def _even_out_kernel(qkvg_ref, conv_ref, x_ref, w_ref,
                     dmat_ref, qdec_ref, kdec_ref, cdec_ref, nw_next_ref,
                     o_ref, h_next_ref, state_scr, y_scr):
    tm = x_ref.shape[0]
    q_ref, v_ref, k_ref, g_ref = (
        qkvg_ref.at[:, n * RET_WIDTH:(n + 1) * RET_WIDTH] for n in range(4))

    @pl.when(pl.program_id(1) == 0)
    def _():
        state_scr[...] = jnp.zeros_like(state_scr)

    for c in range(tm // RET_CHUNK):
        rows = slice(c * RET_CHUNK, (c + 1) * RET_CHUNK)
        for h in range(RET_HEADS):
            cols = slice(h * RET_HEAD_DIM, (h + 1) * RET_HEAD_DIM)
            qh = q_ref[rows, cols]
            kh = k_ref[rows, cols]
            vh = v_ref[rows, cols]
            scores = _dot_nt(qh, kh) * dmat_ref[h]
            inner = _dot(scores.astype(BF16), vh)
            state = state_scr[h]
            cross = _dot(qh, state.astype(BF16)) * qdec_ref[h]
            o = inner + cross
            kd = (kh.astype(F32) * kdec_ref[h]).astype(BF16)
            state_scr[h] = cdec_ref[h] * state + _dot_tn(kd, vh)
            ms = jnp.mean(o * o, axis=-1, keepdims=True)
            ro = o * lax.rsqrt(ms + EPS) * g_ref[rows, cols].astype(F32)
            y_scr[rows, cols] = ro.astype(BF16)
    y_scr[:, RET_WIDTH:] = conv_ref[...]

    ssq = jnp.zeros((tm, 1), F32)
    for n in range(D_MODEL // SUB_N):
        xn = x_ref[:, _sub_cols(n)] + _dot(y_scr[...], w_ref[:, _sub_cols(n)])
        o_ref[:, _sub_cols(n)] = xn
        ssq = ssq + jnp.sum(xn * xn, axis=-1, keepdims=True)
    r = lax.rsqrt(ssq * (1.0 / D_MODEL) + EPS)
    for n in range(D_MODEL // SUB_N):
        h_next_ref[:, _sub_cols(n)] = (o_ref[:, _sub_cols(n)] * r * nw_next_ref[:, _sub_cols(n)]).astype(BF16)


def _even_out(qkvg, conv, x2d, w_out, dmat, qdec, kdec, cdec, nw_next, batch, seq):
    t = x2d.shape[0]
    tm = OUT_TM
    ns = seq // tm

    def row(b, i):
        return b * ns + i

    def const_spec(shape):
        return pl.BlockSpec(shape, lambda b, i: (0,) * len(shape), pipeline_mode=pl.Buffered(1))

    return pl.pallas_call(
        _even_out_kernel,
        grid=(batch, ns),
        in_specs=[
            pl.BlockSpec((tm, 4 * RET_WIDTH), lambda b, i: (row(b, i), 0)),
            pl.BlockSpec((tm, CONV_WIDTH), lambda b, i: (row(b, i), 0)),
            pl.BlockSpec((tm, D_MODEL), lambda b, i: (row(b, i), 0)),
            const_spec((D_MODEL, D_MODEL)),
            const_spec(dmat.shape), const_spec(qdec.shape),
            const_spec(kdec.shape), const_spec(cdec.shape), const_spec(nw_next.shape),
        ],
        out_specs=[
            pl.BlockSpec((tm, D_MODEL), lambda b, i: (row(b, i), 0)),
            pl.BlockSpec((tm, D_MODEL), lambda b, i: (row(b, i), 0)),
        ],
        out_shape=[
            jax.ShapeDtypeStruct((t, D_MODEL), F32),
            jax.ShapeDtypeStruct((t, D_MODEL), BF16),
        ],
        scratch_shapes=[
            pltpu.VMEM((RET_HEADS, RET_HEAD_DIM, RET_HEAD_DIM), F32),
            pltpu.VMEM((tm, D_MODEL), BF16),
        ],
        compiler_params=pltpu.CompilerParams(
            dimension_semantics=("arbitrary", "arbitrary"),
            vmem_limit_bytes=VMEM_LIMIT),
        name="even_out",
    )(qkvg, conv, x2d, w_out, dmat, qdec, kdec, cdec, nw_next)


def _odd_in_kernel(h_ref, w0_ref, w1_ref, w2_ref, w3_ref, wkv_ref,
                   cos_ref, sin_ref, qgain_ref, kgain_ref, ones_ref,
                   qg_ref, kv_ref):
    j = pl.program_id(1)
    w_refs = (w0_ref, w1_ref, w2_ref, w3_ref)

    def sub_dot(w_ref):
        return _dot(h_ref[...], w_ref[...])

    def dup_heads(x):
        lane = lax.broadcasted_iota(jnp.int32, (x.shape[0], LANES), 1)
        low = lane < ATTN_HEAD_DIM
        blocks = []
        for b in range(x.shape[1] // LANES):
            xb = x[:, b * LANES:(b + 1) * LANES]
            swapped = pltpu.roll(xb, ATTN_HEAD_DIM, axis=1)
            blocks += [jnp.where(low, xb, swapped), jnp.where(low, swapped, xb)]
        return jnp.concatenate(blocks, axis=1)

    def norm_rope(acc, gain_ref):
        width = acc.shape[1]
        sq = (acc * acc).astype(BF16)
        ss = jnp.concatenate(
            [_dot(sq[:, c * MXU_N:(c + 1) * MXU_N], ones_ref[...]) for c in range(width // MXU_N)],
            axis=1)
        r = lax.rsqrt(ss * (1.0 / ATTN_HEAD_DIM) + EPS)
        half = ATTN_HEAD_DIM // 2
        lane = lax.broadcasted_iota(jnp.int32, acc.shape, 1)
        first_half = (lane % ATTN_HEAD_DIM) < half
        partner = jnp.where(first_half,
                            pltpu.roll(acc, width - half, axis=1),
                            pltpu.roll(acc, half, axis=1))
        reps = width // LANES
        ta = cos_ref[...] * gain_ref[0:1, :]
        tb = sin_ref[...] * gain_ref[1:2, :]
        ta_w = jnp.concatenate([ta] * reps, axis=1)
        tb_w = jnp.concatenate([tb] * reps, axis=1)
        return r * (acc * ta_w + partner * tb_w)

    def kv_part():
        acc = sub_dot(wkv_ref)
        k = norm_rope(acc[:, :KV_WIDTH], kgain_ref)
        kv_ref[:, :2 * KV_WIDTH] = dup_heads(k).astype(BF16)
        kv_ref[:, 2 * KV_WIDTH:] = dup_heads(acc[:, KV_WIDTH:]).astype(BF16)

    def qg_part():
        for s in range(N_SUB // 2):
            qg_ref[:, _sub_cols(s)] = norm_rope(sub_dot(w_refs[s]), qgain_ref).astype(BF16)
        for s in range(N_SUB // 2, N_SUB):
            qg_ref[:, _sub_cols(s)] = _silu(sub_dot(w_refs[s])).astype(BF16)

    @pl.when(j == 0)
    def _():
        kv_part()
        qg_part()

    @pl.when(j != 0)
    def _():
        qg_part()


def _odd_in(h2d, w, cos_l, sin_l, q_gain, k_gain, ones_bd, seq):
    t = h2d.shape[0]
    tm = IN_TM
    seq_tiles = seq // tm
    half = N_SUB // 2
    assert ATTN_WIDTH == Q_STEPS * half * SUB_N
    gate_col0 = (ATTN_WIDTH + 2 * KV_WIDTH) // SUB_N

    def w_spec(s):
        def index(i, j):
            return 0, half * j + (s % half) + (s // half) * gate_col0
        return pl.BlockSpec((D_MODEL, SUB_N), index)

    def tab_spec():
        return pl.BlockSpec((tm, LANES), lambda i, j: (i % seq_tiles, 0))

    def resident_spec(shape, col):
        return pl.BlockSpec(shape, lambda i, j: (0, col), pipeline_mode=pl.Buffered(1))

    return pl.pallas_call(
        _odd_in_kernel,
        grid=(t // tm, Q_STEPS),
        in_specs=[
            pl.BlockSpec((tm, D_MODEL), lambda i, j: (i, 0)),
            w_spec(0), w_spec(1), w_spec(2), w_spec(3),
            resident_spec((D_MODEL, SUB_N), ATTN_WIDTH // SUB_N),
            tab_spec(), tab_spec(),
            resident_spec(q_gain.shape, 0), resident_spec(k_gain.shape, 0),
            resident_spec((MXU_N, MXU_N), 0),
        ],
        out_specs=[
            pl.BlockSpec((tm, IN_TN), lambda i, j: (i, j)),
            pl.BlockSpec((tm, 4 * KV_WIDTH), lambda i, j: (i, 0)),
        ],
        out_shape=[
            jax.ShapeDtypeStruct((t, 2 * ATTN_WIDTH), BF16),
            jax.ShapeDtypeStruct((t, 4 * KV_WIDTH), BF16),
        ],
        compiler_params=pltpu.CompilerParams(
            dimension_semantics=("arbitrary", "arbitrary"),
            vmem_limit_bytes=VMEM_LIMIT),
        name="odd_in",
    )(h2d, w, w, w, w, w, cos_l, sin_l, q_gain, k_gain, ones_bd)


def _odd_out_kernel(sink_ref, qg_ref, kv_ref, kvp_ref, x_ref, w_ref,
                    o_ref, y_scr):
    tm = x_ref.shape[0]
    group_w = ATTN_WIDTH // Q_STEPS
    q_refs = tuple(qg_ref.at[:, 2 * n * group_w:(2 * n + 1) * group_w] for n in range(Q_STEPS))
    g_refs = tuple(qg_ref.at[:, (2 * n + 1) * group_w:(2 * n + 2) * group_w] for n in range(Q_STEPS))
    kv_per_group = ATTN_KV_HEADS // Q_STEPS
    at_seq_start = pl.program_id(1) == 0
    pairs = ATTN_GROUP // 2
    half_w = 2 * KV_WIDTH

    lane_kv = lax.broadcasted_iota(jnp.int32, (2 * BLOCK, LANES), 1)
    low_kv = lane_kv < ATTN_HEAD_DIM
    rr = lax.broadcasted_iota(jnp.int32, (BLOCK, BLOCK), 0)
    cc = lax.broadcasted_iota(jnp.int32, (BLOCK, BLOCK), 1)
    tri = cc <= rr
    low_o = cc < ATTN_HEAD_DIM

    for bl in range(tm // BLOCK):
        rows = slice(bl * BLOCK, (bl + 1) * BLOCK)
        for c in range(ATTN_KV_HEADS):
            kcols = slice(c * LANES, (c + 1) * LANES)
            vcols = slice(half_w + c * LANES, half_w + (c + 1) * LANES)
            if bl == 0:
                k_prev = kvp_ref[:, kcols]
                v_prev = kvp_ref[:, vcols]
            else:
                prows = slice((bl - 1) * BLOCK, bl * BLOCK)
                k_prev = kv_ref[prows, kcols]
                v_prev = kv_ref[prows, vcols]
            kk = jnp.concatenate([k_prev, kv_ref[rows, kcols]], axis=0)
            vv = jnp.concatenate([v_prev, kv_ref[rows, vcols]], axis=0)
            zero = jnp.zeros_like(kk)
            k_par = (jnp.where(low_kv, kk, zero), jnp.where(low_kv, zero, kk))
            q_ref = q_refs[c // kv_per_group]
            g_ref = g_refs[c // kv_per_group]
            c_in = c % kv_per_group
            qs = jnp.concatenate(
                [q_ref[rows, (pairs * c_in + a) * LANES:(pairs * c_in + a + 1) * LANES]
                 for a in range(pairs)], axis=0)
            probs = []
            denoms = []
            for par in range(2):
                s = _dot_nt(qs, k_par[par])
                for a in range(pairs):
                    sa = s[a * BLOCK:(a + 1) * BLOCK]
                    s_prev = sa[:, :BLOCK]
                    if bl == 0:
                        s_prev = jnp.where(at_seq_start, NEG, s_prev)
                    f = jnp.where(tri, sa[:, BLOCK:], s_prev)
                    sink = sink_ref[ATTN_GROUP * c + 2 * a + par]
                    m = jnp.maximum(jnp.max(f, axis=1, keepdims=True), sink)
                    p = jnp.exp(f - m)
                    denoms.append(jnp.sum(p, axis=1, keepdims=True) + jnp.exp(sink - m))
                    probs.append(jnp.concatenate(
                        [jnp.where(tri, 0.0, p), jnp.where(tri, p, 0.0)], axis=1).astype(BF16))
            o = _dot(jnp.concatenate(probs, axis=0), vv)
            for a in range(pairs):
                o_even = o[a * BLOCK:(a + 1) * BLOCK] / denoms[a]
                o_odd = o[(pairs + a) * BLOCK:(pairs + a + 1) * BLOCK] / denoms[pairs + a]
                gcols = slice((pairs * c_in + a) * LANES, (pairs * c_in + a + 1) * LANES)
                ocols = slice((pairs * c + a) * LANES, (pairs * c + a + 1) * LANES)
                gated = jnp.where(low_o, o_even, o_odd) * g_ref[rows, gcols].astype(F32)
                y_scr[rows, ocols] = gated.astype(BF16)

    for n in range(D_MODEL // SUB_N):
        o_ref[:, _sub_cols(n)] = x_ref[:, _sub_cols(n)] + _dot(y_scr[...], w_ref[:, _sub_cols(n)])


def _odd_out(sinks, qg, kv, x2d, w_out, batch, seq):
    t = x2d.shape[0]
    tm = OUT_TM
    ns = seq // tm
    bpt = tm // BLOCK

    def row(b, i):
        return b * ns + i

    return pl.pallas_call(
        _odd_out_kernel,
        grid=(batch, ns),
        in_specs=[
            pl.BlockSpec(memory_space=pltpu.SMEM),
            pl.BlockSpec((tm, 2 * ATTN_WIDTH), lambda b, i: (row(b, i), 0)),
            pl.BlockSpec((tm, 4 * KV_WIDTH), lambda b, i: (row(b, i), 0)),
            pl.BlockSpec((BLOCK, 4 * KV_WIDTH),
                         lambda b, i: (jnp.maximum(row(b, i) * bpt - 1, 0), 0)),
            pl.BlockSpec((tm, D_MODEL), lambda b, i: (row(b, i), 0)),
            pl.BlockSpec((ATTN_WIDTH, D_MODEL), lambda b, i: (0, 0), pipeline_mode=pl.Buffered(1)),
        ],
        out_specs=pl.BlockSpec((tm, D_MODEL), lambda b, i: (row(b, i), 0)),
        out_shape=jax.ShapeDtypeStruct((t, D_MODEL), F32),
        scratch_shapes=[pltpu.VMEM((tm, ATTN_WIDTH), BF16)],
        compiler_params=pltpu.CompilerParams(
            dimension_semantics=("arbitrary", "arbitrary"),
            vmem_limit_bytes=VMEM_LIMIT),
        name="odd_out",
    )(sinks, qg, kv, kv, x2d, w_out)


def _rope_cos_sin(seq, dim):
    inv = 1.0 / (ROPE_THETA ** (np.arange(0, dim, 2, dtype=np.float64) / dim))
    ang = np.arange(seq, dtype=np.float64)[:, None] * inv[None, :]
    return np.cos(ang), np.sin(ang)


def _retention_tables():
    h = np.arange(RET_HEADS, dtype=np.float64)
    log_g = np.log(1.0 - 2.0 ** (-5.0 - h))
    idx = np.arange(RET_CHUNK, dtype=np.float64)
    diff = idx[:, None] - idx[None, :]
    scale = RET_HEAD_DIM ** -0.5
    intra = np.where(diff >= 0, np.exp(log_g[:, None, None] * np.maximum(diff, 0.0)), 0.0)
    q_dec = np.exp(log_g[:, None] * (idx[None, :] + 1.0))
    k_dec = np.exp(log_g[:, None] * (RET_CHUNK - 1.0 - idx[None, :]))
    c_dec = np.exp(log_g * RET_CHUNK)
    wide = (RET_HEADS, RET_CHUNK, RET_HEAD_DIM)
    dmat = jnp.asarray(intra * scale, F32)
    qdec = jnp.asarray(np.broadcast_to(q_dec[:, :, None], wide), F32)
    kdec = jnp.asarray(np.broadcast_to(k_dec[:, :, None] * scale, wide), F32)
    cdec = jnp.asarray(np.broadcast_to(c_dec[:, None, None], (RET_HEADS, 1, RET_HEAD_DIM)), F32)
    return dmat, qdec, kdec, cdec


def _head_rope_tables(seq):
    reps = LANES // ATTN_HEAD_DIM
    cos, sin = _rope_cos_sin(seq, ATTN_HEAD_DIM)
    cos_l = jnp.asarray(np.tile(np.concatenate([cos, cos], axis=1), (1, reps)), F32)
    sin_l = jnp.asarray(np.tile(np.concatenate([-sin, sin], axis=1), (1, reps)), F32)
    return cos_l, sin_l


def _head_gain_rows(norm_w, scale):
    half = ATTN_HEAD_DIM // 2
    reps = LANES // ATTN_HEAD_DIM
    w = norm_w.astype(F32) * scale
    w_partner = jnp.concatenate([w[half:], w[:half]])
    rows = jnp.stack([jnp.tile(w, reps), jnp.tile(w_partner, reps)])
    return jnp.pad(rows, ((0, SUBLANES - rows.shape[0]), (0, 0)))


def kernel(x, ev_norm_w, ev_w_in, ev_conv_w, ev_w_out, od_norm_w, od_w_in,
           od_q_norm_w, od_k_norm_w, od_sinks, od_w_out):
    batch, seq, d = x.shape
    assert d == D_MODEL and seq % IN_TM == 0 and seq % OUT_TM == 0
    x2d = x.reshape(batch * seq, d)

    cos, sin = _rope_cos_sin(seq, RET_HEAD_DIM)
    cw = jnp.pad(ev_conv_w[0], ((0, SUBLANES - CONV_K), (0, 0)))
    qkvg, conv, w_out_e, w_in_o, w_out_o = _even_in(
        x2d, ev_norm_w[0][None, :], ev_w_in[0].astype(BF16),
        jnp.asarray(cos, F32), jnp.asarray(sin, F32), cw,
        (ev_w_out[0], od_w_in[0], od_w_out[0]), seq)
    dmat, qdec, kdec, cdec = _retention_tables()
    x2d, h2d = _even_out(qkvg, conv, x2d, w_out_e, dmat, qdec, kdec, cdec,
                         od_norm_w[0][None, :], batch, seq)

    cos_l, sin_l = _head_rope_tables(seq)
    q_gain = _head_gain_rows(od_q_norm_w[0], ATTN_HEAD_DIM ** -0.5)
    k_gain = _head_gain_rows(od_k_norm_w[0], 1.0)
    head_id = np.arange(MXU_N) // ATTN_HEAD_DIM
    ones_bd = jnp.asarray(head_id[:, None] == head_id[None, :], BF16)
    qg, kv = _odd_in(h2d, w_in_o, cos_l, sin_l, q_gain, k_gain, ones_bd, seq)
    x2d = _odd_out(od_sinks[0].astype(F32), qg, kv, x2d, w_out_o, batch, seq)
    return x2d.reshape(batch, seq, d)
```

```python
import functools

import numpy as np
import jax
import jax.numpy as jnp
from jax import lax
from jax.experimental import pallas as pl
from jax.experimental.pallas import tpu as pltpu

D_MODEL = 2048
RET_HEADS = 4
RET_HEAD_DIM = 256
RET_WIDTH = RET_HEADS * RET_HEAD_DIM
CONV_WIDTH = D_MODEL - RET_WIDTH
CONV_K = 3
RET_CHUNK = 256
EVEN_IN = 4 * RET_WIDTH + 4 * CONV_WIDTH
ATTN_HEAD_DIM = 64
ATTN_Q_HEADS = 32
ATTN_KV_HEADS = 4
ATTN_GROUP = ATTN_Q_HEADS // ATTN_KV_HEADS
ATTN_WIDTH = ATTN_Q_HEADS * ATTN_HEAD_DIM
KV_WIDTH = ATTN_KV_HEADS * ATTN_HEAD_DIM
BLOCK = 128
ROPE_THETA = 10000.0
EPS = 1e-6
NEG = -1e30

LANES = 128
SUBLANES = 8
BF16_ROWS = 16
MXU_N = 256
SUB_N = 512
N_SUB = 4
Q_STEPS = 2
IN_TM = 1024
IN_TN = N_SUB * SUB_N
OUT_TM = 512
HALO = SUBLANES
V7X_VMEM_BYTES = 64 * 1024 * 1024
VMEM_LIMIT = V7X_VMEM_BYTES * 7 // 8

F32 = jnp.float32
BF16 = jnp.bfloat16


def _dot(a, b):
    return jnp.dot(a, b, preferred_element_type=F32)


def _dot_nt(a, b):
    return lax.dot_general(a, b, (((1,), (1,)), ((), ())), preferred_element_type=F32)


def _dot_tn(a, b):
    return lax.dot_general(a, b, (((0,), (0,)), ((), ())), preferred_element_type=F32)


def _silu(g):
    return (0.5 * g) * (1.0 + jnp.tanh(0.5 * g))


def _rms_rows(x, nw):
    ms = jnp.mean(x * x, axis=-1, keepdims=True)
    return x * lax.rsqrt(ms + EPS) * nw


def _sub_cols(s):
    return slice(s * SUB_N, (s + 1) * SUB_N)


def _even_in_kernel(x_ref, nw_ref, w0_ref, w1_ref, w2_ref, w3_ref,
                    cos_ref, sin_ref, cw_ref, c0_ref, c1_ref, c2_ref,
                    qkvg_ref, conv_ref, c0_out, c1_out, c2_out,
                    h_cur, h_next, cu_scr, tail_scr, *, seq_tiles):
    i = pl.program_id(0)
    j = pl.program_id(1)
    tm = h_cur.shape[0]
    chunk = x_ref.shape[0]
    w_refs = (w0_ref, w1_ref, w2_ref, w3_ref)

    for src, dst in ((c0_ref, c0_out), (c1_ref, c1_out), (c2_ref, c2_out)):
        dst[...] = src[...].astype(BF16)

    def norm_next(step=None):
        if step is None:
            rows = pl.ds(pl.multiple_of(j * chunk, chunk), chunk)
        else:
            rows = slice(step * chunk, (step + 1) * chunk)
        h_next[rows, :] = _rms_rows(x_ref[...], nw_ref[...]).astype(BF16)

    def sub_dot(s):
        return _dot(h_cur[...], w_refs[s][...])

    @pl.when(i == 0)
    def _():
        norm_next()
        tail_scr[...] = jnp.zeros_like(tail_scr)

    projecting = i >= 1

    def rope_then(tail_fn, step):
        if step == 0:
            for kb in range(D_MODEL // MXU_N):
                h_cur[:, kb * MXU_N:(kb + 1) * MXU_N] = h_next[:, kb * MXU_N:(kb + 1) * MXU_N]
        norm_next(step)
        cos = cos_ref[...]
        sin = sin_ref[...]
        for s in range(N_SUB // 2):
            acc = sub_dot(s)
            for hd in range(SUB_N // RET_HEAD_DIM):
                base = s * SUB_N + hd * RET_HEAD_DIM
                x1 = acc[:, hd * RET_HEAD_DIM:hd * RET_HEAD_DIM + LANES]
                x2 = acc[:, hd * RET_HEAD_DIM + LANES:(hd + 1) * RET_HEAD_DIM]
                qkvg_ref[:, base:base + LANES] = (x1 * cos - x2 * sin).astype(BF16)
                qkvg_ref[:, base + LANES:base + RET_HEAD_DIM] = (x2 * cos + x1 * sin).astype(BF16)
        for s in range(N_SUB // 2, N_SUB):
            qkvg_ref[:, _sub_cols(s)] = tail_fn(sub_dot(s)).astype(BF16)

    @pl.when(jnp.logical_and(projecting, j == 0))
    def _():
        rope_then(lambda v: v, 0)

    @pl.when(jnp.logical_and(projecting, j == 1))
    def _():
        rope_then(_silu, 1)

    @pl.when(jnp.logical_and(projecting, j >= 2))
    def _():
        cs = j - 2
        norm_next()
        cu = sub_dot(0) * sub_dot(1)
        at_seq_start = ((i - 1) % seq_tiles) == 0
        halo_rows = slice(HALO - (CONV_K - 1), HALO)
        cu_scr[halo_rows, :] = jnp.where(at_seq_start, 0.0, tail_scr[cs, halo_rows, :])
        cu_scr[HALO:HALO + tm, :] = cu
        tail_scr[cs, halo_rows, :] = cu_scr[HALO + tm - (CONV_K - 1):HALO + tm, :]
        cw = cw_ref[...]
        conv = cw[CONV_K - 1:CONV_K, :] * cu
        for t in range(CONV_K - 1):
            shift = CONV_K - 1 - t
            conv = conv + cw[t:t + 1, :] * cu_scr[HALO - shift:HALO - shift + tm, :]
        gated = sub_dot(2) * conv
        conv_ref[...] = (gated * _silu(sub_dot(3))).astype(BF16)


def _even_in(x2d, nw, w, cos, sin, cw, to_cast, seq):
    t = x2d.shape[0]
    tm = IN_TM
    n_tiles = t // tm
    seq_tiles = seq // tm
    n_ret_steps = 4 * RET_WIDTH // IN_TN
    n_conv_steps = CONV_WIDTH // SUB_N
    nj = n_ret_steps + n_conv_steps
    chunk = tm // nj
    n_proj_steps = n_tiles * nj
    conv_first = tuple((4 * RET_WIDTH + b * CONV_WIDTH) // SUB_N for b in (1, 2, 0, 3))

    def tile(i):
        return jnp.maximum(i - 1, 0)

    def step(i, j):
        return jnp.where(i == 0, 0, j)

    def w_spec(s):
        def index(i, j):
            jj = step(i, j)
            half = N_SUB // 2
            ret_idx = half * jnp.minimum(jj, n_ret_steps - 1) + (s % half) + (s // half) * N_SUB
            conv_idx = conv_first[s] + jj - n_ret_steps
            return 0, jnp.where(jj < n_ret_steps, ret_idx, conv_idx)
        return pl.BlockSpec((D_MODEL, SUB_N), index)

    def cast_spec(a):
        rows = a.shape[0] // n_proj_steps
        assert rows * n_proj_steps == a.shape[0] and rows % BF16_ROWS == 0
        return pl.BlockSpec((rows, a.shape[1]), lambda i, j: (tile(i) * nj + step(i, j), 0))

    def table_spec():
        return pl.BlockSpec((tm, LANES), lambda i, j: (tile(i) % seq_tiles, 0))

    kern = functools.partial(_even_in_kernel, seq_tiles=seq_tiles)
    return pl.pallas_call(
        kern,
        grid=(n_tiles + 1, nj),
        in_specs=[
            pl.BlockSpec((chunk, D_MODEL), lambda i, j: (jnp.minimum(i, n_tiles - 1) * nj + j, 0)),
            pl.BlockSpec((1, D_MODEL), lambda i, j: (0, 0), pipeline_mode=pl.Buffered(1)),
            w_spec(0), w_spec(1), w_spec(2), w_spec(3),
            table_spec(), table_spec(),
            pl.BlockSpec((SUBLANES, SUB_N), lambda i, j: (0, jnp.maximum(step(i, j) - n_ret_steps, 0))),
        ] + [cast_spec(a) for a in to_cast],
        out_specs=[
            pl.BlockSpec((tm, IN_TN),
                         lambda i, j: (tile(i), jnp.minimum(step(i, j), n_ret_steps - 1))),
            pl.BlockSpec((tm, SUB_N),
                         lambda i, j: (tile(i), jnp.maximum(step(i, j) - n_ret_steps, 0))),
        ] + [cast_spec(a) for a in to_cast],
        out_shape=[
            jax.ShapeDtypeStruct((t, 4 * RET_WIDTH), BF16),
            jax.ShapeDtypeStruct((t, CONV_WIDTH), BF16),
        ] + [jax.ShapeDtypeStruct(a.shape, BF16) for a in to_cast],
        scratch_shapes=[
            pltpu.VMEM((tm, D_MODEL), BF16),
            pltpu.VMEM((tm, D_MODEL), BF16),
            pltpu.VMEM((HALO + tm, SUB_N), F32),
            pltpu.VMEM((n_conv_steps, HALO, SUB_N), F32),
        ],
        compiler_params=pltpu.CompilerParams(
            dimension_semantics=("arbitrary", "arbitrary"),
            vmem_limit_bytes=VMEM_LIMIT),
        name="even_in",
    )(x2d, nw, w, w, w, w, cos, sin, cw, *to_cast)


def _even_out_kernel(qkvg_ref, conv_ref, x_ref, w_ref,
                     dmat_ref, qdec_ref, kdec_ref, cdec_ref, nw_next_ref,
                     o_ref, h_next_ref, state_scr, y_scr):
    tm = x_ref.shape[0]
    q_ref, v_ref, k_ref, g_ref = (
        qkvg_ref.at[:, n * RET_WIDTH:(n + 1) * RET_WIDTH] for n in range(4))

    @pl.when(pl.program_id(1) == 0)
    def _():
        state_scr[...] = jnp.zeros_like(state_scr)

    for c in range(tm // RET_CHUNK):
        rows = slice(c * RET_CHUNK, (c + 1) * RET_CHUNK)
        for h in range(RET_HEADS):
            cols = slice(h * RET_HEAD_DIM, (h + 1) * RET_HEAD_DIM)
            qh = q_ref[rows, cols]
            kh = k_ref[rows, cols]
            vh = v_ref[rows, cols]
            scores = _dot_nt(qh, kh) * dmat_ref[h]
            inner = _dot(scores.astype(BF16), vh)
            state = state_scr[h]
            cross = _dot(qh, state.astype(BF16)) * qdec_ref[h]
            o = inner + cross
            kd = (kh.astype(F32) * kdec_ref[h]).astype(BF16)
            state_scr[h] = cdec_ref[h] * state + _dot_tn(kd, vh)
            ms = jnp.mean(o * o, axis=-1, keepdims=True)
            ro = o * lax.rsqrt(ms + EPS) * g_ref[rows, cols].astype(F32)
            y_scr[rows, cols] = ro.astype(BF16)
    y_scr[:, RET_WIDTH:] = conv_ref[...]

    ssq = jnp.zeros((tm, 1), F32)
    for n in range(D_MODEL // SUB_N):
        xn = x_ref[:, _sub_cols(n)] + _dot(y_scr[...], w_ref[:, _sub_cols(n)])
        o_ref[:, _sub_cols(n)] = xn
        ssq = ssq + jnp.sum(xn * xn, axis=-1, keepdims=True)
    r = lax.rsqrt(ssq * (1.0 / D_MODEL) + EPS)
    for n in range(D_MODEL // SUB_N):
        h_next_ref[:, _sub_cols(n)] = (o_ref[:, _sub_cols(n)] * r * nw_next_ref[:, _sub_cols(n)]).astype(BF16)


def _even_out(qkvg, conv, x2d, w_out, dmat, qdec, kdec, cdec, nw_next, batch, seq):
    t = x2d.shape[0]
    tm = OUT_TM
    ns = seq // tm

    def row(b, i):
        return b * ns + i

    def const_spec(shape):
        return pl.BlockSpec(shape, lambda b, i: (0,) * len(shape), pipeline_mode=pl.Buffered(1))

    return pl.pallas_call(
        _even_out_kernel,
        grid=(batch, ns),
        in_specs=[
            pl.BlockSpec((tm, 4 * RET_WIDTH), lambda b, i: (row(b, i), 0)),
            pl.BlockSpec((tm, CONV_WIDTH), lambda b, i: (row(b, i), 0)),
            pl.BlockSpec((tm, D_MODEL), lambda b, i: (row(b, i), 0)),
            const_spec((D_MODEL, D_MODEL)),
            const_spec(dmat.shape), const_spec(qdec.shape),
            const_spec(kdec.shape), const_spec(cdec.shape), const_spec(nw_next.shape),
        ],
        out_specs=[
            pl.BlockSpec((tm, D_MODEL), lambda b, i: (row(b, i), 0)),
            pl.BlockSpec((tm, D_MODEL), lambda b, i: (row(b, i), 0)),
        ],
        out_shape=[
            jax.ShapeDtypeStruct((t, D_MODEL), F32),
            jax.ShapeDtypeStruct((t, D_MODEL), BF16),
        ],
        scratch_shapes=[
            pltpu.VMEM((RET_HEADS, RET_HEAD_DIM, RET_HEAD_DIM), F32),
            pltpu.VMEM((tm, D_MODEL), BF16),
        ],
        compiler_params=pltpu.CompilerParams(
            dimension_semantics=("arbitrary", "arbitrary"),
            vmem_limit_bytes=VMEM_LIMIT),
        name="even_out",
    )(qkvg, conv, x2d, w_out, dmat, qdec, kdec, cdec, nw_next)


def _odd_in_kernel(h_ref, w_ref,
                   cos_ref, sin_ref, qgain_ref, kgain_ref, ones_ref,
                   qg_ref, kv_ref):
    j = pl.program_id(1)
    half = N_SUB // 2
    kv_block = ATTN_WIDTH // SUB_N
    gate_block0 = kv_block + 1

    def sub_dot(block):
        return _dot(h_ref[...], w_ref[:, block * SUB_N:(block + 1) * SUB_N])

    def dup_heads(x):
        lane = lax.broadcasted_iota(jnp.int32, (x.shape[0], LANES), 1)
        low = lane < ATTN_HEAD_DIM
        blocks = []
        for b in range(x.shape[1] // LANES):
            xb = x[:, b * LANES:(b + 1) * LANES]
            swapped = pltpu.roll(xb, ATTN_HEAD_DIM, axis=1)
            blocks += [jnp.where(low, xb, swapped), jnp.where(low, swapped, xb)]
        return jnp.concatenate(blocks, axis=1)

    def norm_rope(acc, gain_ref):
        width = acc.shape[1]
        sq = (acc * acc).astype(BF16)
        ss = jnp.concatenate(
            [_dot(sq[:, c * MXU_N:(c + 1) * MXU_N], ones_ref[...]) for c in range(width // MXU_N)],
            axis=1)
        r = lax.rsqrt(ss * (1.0 / ATTN_HEAD_DIM) + EPS)
        half = ATTN_HEAD_DIM // 2
        lane = lax.broadcasted_iota(jnp.int32, acc.shape, 1)
        first_half = (lane % ATTN_HEAD_DIM) < half
        partner = jnp.where(first_half,
                            pltpu.roll(acc, width - half, axis=1),
                            pltpu.roll(acc, half, axis=1))
        reps = width // LANES
        ta = cos_ref[...] * gain_ref[0:1, :]
        tb = sin_ref[...] * gain_ref[1:2, :]
        ta_w = jnp.concatenate([ta] * reps, axis=1)
        tb_w = jnp.concatenate([tb] * reps, axis=1)
        return r * (acc * ta_w + partner * tb_w)

    def kv_part():
        acc = sub_dot(kv_block)
        k = norm_rope(acc[:, :KV_WIDTH], kgain_ref)
        kv_ref[:, :2 * KV_WIDTH] = dup_heads(k).astype(BF16)
        kv_ref[:, 2 * KV_WIDTH:] = dup_heads(acc[:, KV_WIDTH:]).astype(BF16)

    def qg_part(group):
        for s in range(half):
            acc = sub_dot(half * group + s)
            qg_ref[:, _sub_cols(s)] = norm_rope(acc, qgain_ref).astype(BF16)
        for s in range(half):
            acc = sub_dot(gate_block0 + half * group + s)
            qg_ref[:, _sub_cols(half + s)] = _silu(acc).astype(BF16)

    for group in range(Q_STEPS):
        @pl.when(j == group)
        def _(group=group):
            if group == 0:
                kv_part()
            qg_part(group)


def _odd_in(h2d, w, cos_l, sin_l, q_gain, k_gain, ones_bd, seq):
    t = h2d.shape[0]
    tm = IN_TM
    seq_tiles = seq // tm
    half = N_SUB // 2
    assert ATTN_WIDTH == Q_STEPS * half * SUB_N
    gate_col0 = (ATTN_WIDTH + 2 * KV_WIDTH) // SUB_N

    def tab_spec():
        return pl.BlockSpec((tm, LANES), lambda i, j: (i % seq_tiles, 0))

    def resident_spec(shape, col):
        return pl.BlockSpec(shape, lambda i, j: (0, col), pipeline_mode=pl.Buffered(1))

    return pl.pallas_call(
        _odd_in_kernel,
        grid=(t // tm, Q_STEPS),
        in_specs=[
            pl.BlockSpec((tm, D_MODEL), lambda i, j: (i, 0)),
            resident_spec(w.shape, 0),
            tab_spec(), tab_spec(),
            resident_spec(q_gain.shape, 0), resident_spec(k_gain.shape, 0),
            resident_spec((MXU_N, MXU_N), 0),
        ],
        out_specs=[
            pl.BlockSpec((tm, IN_TN), lambda i, j: (i, j)),
            pl.BlockSpec((tm, 4 * KV_WIDTH), lambda i, j: (i, 0)),
        ],
        out_shape=[
            jax.ShapeDtypeStruct((t, 2 * ATTN_WIDTH), BF16),
            jax.ShapeDtypeStruct((t, 4 * KV_WIDTH), BF16),
        ],
        compiler_params=pltpu.CompilerParams(
            dimension_semantics=("arbitrary", "arbitrary"),
            vmem_limit_bytes=VMEM_LIMIT),
        name="odd_in",
    )(h2d, w, cos_l, sin_l, q_gain, k_gain, ones_bd)


def _odd_out_kernel(sink_ref, qg_ref, kv_ref, kvp_ref, x_ref, w_ref,
                    o_ref, y_scr):
    tm = x_ref.shape[0]
    group_w = ATTN_WIDTH // Q_STEPS
    q_refs = tuple(qg_ref.at[:, 2 * n * group_w:(2 * n + 1) * group_w] for n in range(Q_STEPS))
    g_refs = tuple(qg_ref.at[:, (2 * n + 1) * group_w:(2 * n + 2) * group_w] for n in range(Q_STEPS))
    kv_per_group = ATTN_KV_HEADS // Q_STEPS
    at_seq_start = pl.program_id(1) == 0
    pairs = ATTN_GROUP // 2
    half_w = 2 * KV_WIDTH

    lane_kv = lax.broadcasted_iota(jnp.int32, (2 * BLOCK, LANES), 1)
    low_kv = lane_kv < ATTN_HEAD_DIM
    rr = lax.broadcasted_iota(jnp.int32, (BLOCK, BLOCK), 0)
    cc = lax.broadcasted_iota(jnp.int32, (BLOCK, BLOCK), 1)
    tri = cc <= rr
    low_o = cc < ATTN_HEAD_DIM

    for bl in range(tm // BLOCK):
        rows = slice(bl * BLOCK, (bl + 1) * BLOCK)
        for c in range(ATTN_KV_HEADS):
            kcols = slice(c * LANES, (c + 1) * LANES)
            vcols = slice(half_w + c * LANES, half_w + (c + 1) * LANES)
            if bl == 0:
                k_prev = kvp_ref[:, kcols]
                v_prev = kvp_ref[:, vcols]
            else:
                prows = slice((bl - 1) * BLOCK, bl * BLOCK)
                k_prev = kv_ref[prows, kcols]
                v_prev = kv_ref[prows, vcols]
            kk = jnp.concatenate([k_prev, kv_ref[rows, kcols]], axis=0)
            vv = jnp.concatenate([v_prev, kv_ref[rows, vcols]], axis=0)
            zero = jnp.zeros_like(kk)
            k_par = (jnp.where(low_kv, kk, zero), jnp.where(low_kv, zero, kk))
            q_ref = q_refs[c // kv_per_group]
            g_ref = g_refs[c // kv_per_group]
            c_in = c % kv_per_group
            qs = jnp.concatenate(
                [q_ref[rows, (pairs * c_in + a) * LANES:(pairs * c_in + a + 1) * LANES]
                 for a in range(pairs)], axis=0)
            probs = []
            denoms = []
            for par in range(2):
                s = _dot_nt(qs, k_par[par])
                for a in range(pairs):
                    sa = s[a * BLOCK:(a + 1) * BLOCK]
                    s_prev = sa[:, :BLOCK]
                    if bl == 0:
                        s_prev = jnp.where(at_seq_start, NEG, s_prev)
                    f = jnp.where(tri, sa[:, BLOCK:], s_prev)
                    sink = sink_ref[ATTN_GROUP * c + 2 * a + par]
                    m = jnp.maximum(jnp.max(f, axis=1, keepdims=True), sink)
                    p = jnp.exp(f - m)
                    denoms.append(jnp.sum(p, axis=1, keepdims=True) + jnp.exp(sink - m))
                    probs.append(jnp.concatenate(
                        [jnp.where(tri, 0.0, p), jnp.where(tri, p, 0.0)], axis=1).astype(BF16))
            o = _dot(jnp.concatenate(probs, axis=0), vv)
            for a in range(pairs):
                o_even = o[a * BLOCK:(a + 1) * BLOCK] / denoms[a]
                o_odd = o[(pairs + a) * BLOCK:(pairs + a + 1) * BLOCK] / denoms[pairs + a]
                gcols = slice((pairs * c_in + a) * LANES, (pairs * c_in + a + 1) * LANES)
                ocols = slice((pairs * c + a) * LANES, (pairs * c + a + 1) * LANES)
                gated = jnp.where(low_o, o_even, o_odd) * g_ref[rows, gcols].astype(F32)
                y_scr[rows, ocols] = gated.astype(BF16)

    for n in range(D_MODEL // SUB_N):
        o_ref[:, _sub_cols(n)] = x_ref[:, _sub_cols(n)] + _dot(y_scr[...], w_ref[:, _sub_cols(n)])


def _odd_out(sinks, qg, kv, x2d, w_out, batch, seq):
    t = x2d.shape[0]
    tm = OUT_TM
    ns = seq // tm
    bpt = tm // BLOCK

    def row(b, i):
        return b * ns + i

    return pl.pallas_call(
        _odd_out_kernel,
        grid=(batch, ns),
        in_specs=[
            pl.BlockSpec(memory_space=pltpu.SMEM),
            pl.BlockSpec((tm, 2 * ATTN_WIDTH), lambda b, i: (row(b, i), 0)),
            pl.BlockSpec((tm, 4 * KV_WIDTH), lambda b, i: (row(b, i), 0)),
            pl.BlockSpec((BLOCK, 4 * KV_WIDTH),
                         lambda b, i: (jnp.maximum(row(b, i) * bpt - 1, 0), 0)),
            pl.BlockSpec((tm, D_MODEL), lambda b, i: (row(b, i), 0)),
            pl.BlockSpec((ATTN_WIDTH, D_MODEL), lambda b, i: (0, 0), pipeline_mode=pl.Buffered(1)),
        ],
        out_specs=pl.BlockSpec((tm, D_MODEL), lambda b, i: (row(b, i), 0)),
        out_shape=jax.ShapeDtypeStruct((t, D_MODEL), F32),
        scratch_shapes=[pltpu.VMEM((tm, ATTN_WIDTH), BF16)],
        compiler_params=pltpu.CompilerParams(
            dimension_semantics=("arbitrary", "arbitrary"),
            vmem_limit_bytes=VMEM_LIMIT),
        name="odd_out",
    )(sinks, qg, kv, kv, x2d, w_out)


def _rope_cos_sin(seq, dim):
    inv = 1.0 / (ROPE_THETA ** (np.arange(0, dim, 2, dtype=np.float64) / dim))
    ang = np.arange(seq, dtype=np.float64)[:, None] * inv[None, :]
    return np.cos(ang), np.sin(ang)


def _retention_tables():
    h = np.arange(RET_HEADS, dtype=np.float64)
    log_g = np.log(1.0 - 2.0 ** (-5.0 - h))
    idx = np.arange(RET_CHUNK, dtype=np.float64)
    diff = idx[:, None] - idx[None, :]
    scale = RET_HEAD_DIM ** -0.5
    intra = np.where(diff >= 0, np.exp(log_g[:, None, None] * np.maximum(diff, 0.0)), 0.0)
    q_dec = np.exp(log_g[:, None] * (idx[None, :] + 1.0))
    k_dec = np.exp(log_g[:, None] * (RET_CHUNK - 1.0 - idx[None, :]))
    c_dec = np.exp(log_g * RET_CHUNK)
    wide = (RET_HEADS, RET_CHUNK, RET_HEAD_DIM)
    dmat = jnp.asarray(intra * scale, F32)
    qdec = jnp.asarray(np.broadcast_to(q_dec[:, :, None], wide), F32)
    kdec = jnp.asarray(np.broadcast_to(k_dec[:, :, None] * scale, wide), F32)
    cdec = jnp.asarray(np.broadcast_to(c_dec[:, None, None], (RET_HEADS, 1, RET_HEAD_DIM)), F32)
    return dmat, qdec, kdec, cdec


def _head_rope_tables(seq):
    reps = LANES // ATTN_HEAD_DIM
    cos, sin = _rope_cos_sin(seq, ATTN_HEAD_DIM)
    cos_l = jnp.asarray(np.tile(np.concatenate([cos, cos], axis=1), (1, reps)), F32)
    sin_l = jnp.asarray(np.tile(np.concatenate([-sin, sin], axis=1), (1, reps)), F32)
    return cos_l, sin_l


def _head_gain_rows(norm_w, scale):
    half = ATTN_HEAD_DIM // 2
    reps = LANES // ATTN_HEAD_DIM
    w = norm_w.astype(F32) * scale
    w_partner = jnp.concatenate([w[half:], w[:half]])
    rows = jnp.stack([jnp.tile(w, reps), jnp.tile(w_partner, reps)])
    return jnp.pad(rows, ((0, SUBLANES - rows.shape[0]), (0, 0)))


def kernel(x, ev_norm_w, ev_w_in, ev_conv_w, ev_w_out, od_norm_w, od_w_in,
           od_q_norm_w, od_k_norm_w, od_sinks, od_w_out):
    batch, seq, d = x.shape
    assert d == D_MODEL and seq % IN_TM == 0 and seq % OUT_TM == 0
    x2d = x.reshape(batch * seq, d)

    cos, sin = _rope_cos_sin(seq, RET_HEAD_DIM)
    cw = jnp.pad(ev_conv_w[0], ((0, SUBLANES - CONV_K), (0, 0)))
    qkvg, conv, w_out_e, w_in_o, w_out_o = _even_in(
        x2d, ev_norm_w[0][None, :], ev_w_in[0].astype(BF16),
        jnp.asarray(cos, F32), jnp.asarray(sin, F32), cw,
        (ev_w_out[0], od_w_in[0], od_w_out[0]), seq)
    dmat, qdec, kdec, cdec = _retention_tables()
    x2d, h2d = _even_out(qkvg, conv, x2d, w_out_e, dmat, qdec, kdec, cdec,
                         od_norm_w[0][None, :], batch, seq)

    cos_l, sin_l = _head_rope_tables(seq)
    q_gain = _head_gain_rows(od_q_norm_w[0], ATTN_HEAD_DIM ** -0.5)
    k_gain = _head_gain_rows(od_k_norm_w[0], 1.0)
    head_id = np.arange(MXU_N) // ATTN_HEAD_DIM
    ones_bd = jnp.asarray(head_id[:, None] == head_id[None, :], BF16)
    qg, kv = _odd_in(h2d, w_in_o, cos_l, sin_l, q_gain, k_gain, ones_bd, seq)
    x2d = _odd_out(od_sinks[0].astype(F32), qg, kv, x2d, w_out_o, batch, seq)
    return x2d.reshape(batch, seq, d)
```

```python
import functools

import numpy as np
import jax
import jax.numpy as jnp
from jax import lax
from jax.experimental import pallas as pl
from jax.experimental.pallas import tpu as pltpu

D_MODEL = 2048
RET_HEADS = 4
RET_HEAD_DIM = 256
RET_WIDTH = RET_HEADS * RET_HEAD_DIM
CONV_WIDTH = D_MODEL - RET_WIDTH
CONV_K = 3
RET_CHUNK = 256
EVEN_IN = 4 * RET_WIDTH + 4 * CONV_WIDTH
ATTN_HEAD_DIM = 64
ATTN_Q_HEADS = 32
ATTN_KV_HEADS = 4
ATTN_GROUP = ATTN_Q_HEADS // ATTN_KV_HEADS
ATTN_WIDTH = ATTN_Q_HEADS * ATTN_HEAD_DIM
KV_WIDTH = ATTN_KV_HEADS * ATTN_HEAD_DIM
BLOCK = 128
ROPE_THETA = 10000.0
EPS = 1e-6
NEG = -1e30

LANES = 128
SUBLANES = 8
BF16_ROWS = 16
MXU_N = 256
SUB_N = 512
N_SUB = 4
IN_TM = 1024
IN_TN = N_SUB * SUB_N
OUT_TM = 512
HALO = SUBLANES
V7X_VMEM_BYTES = 64 * 1024 * 1024
VMEM_LIMIT = V7X_VMEM_BYTES * 7 // 8

F32 = jnp.float32
BF16 = jnp.bfloat16


def _dot(a, b):
    return jnp.dot(a, b, preferred_element_type=F32)


def _dot_nt(a, b):
    return lax.dot_general(a, b, (((1,), (1,)), ((), ())), preferred_element_type=F32)


def _dot_tn(a, b):
    return lax.dot_general(a, b, (((0,), (0,)), ((), ())), preferred_element_type=F32)


def _silu(g):
    return (0.5 * g) * (1.0 + jnp.tanh(0.5 * g))


def _rms_rows(x, nw):
    ms = jnp.mean(x * x, axis=-1, keepdims=True)
    return x * lax.rsqrt(ms + EPS) * nw


def _sub_cols(s):
    return slice(s * SUB_N, (s + 1) * SUB_N)


def _even_in_kernel(x_ref, nw_ref, w0_ref, w1_ref, w2_ref, w3_ref,
                    cos_ref, sin_ref, cw_ref, c0_ref, c1_ref, c2_ref,
                    qkvg_ref, conv_ref, c0_out, c1_out, c2_out,
                    h_cur, h_next, cu_scr, tail_scr, *, seq_tiles):
    i = pl.program_id(0)
    j = pl.program_id(1)
    tm = h_cur.shape[0]
    chunk = x_ref.shape[0]
    w_refs = (w0_ref, w1_ref, w2_ref, w3_ref)

    for src, dst in ((c0_ref, c0_out), (c1_ref, c1_out), (c2_ref, c2_out)):
        dst[...] = src[...].astype(BF16)

    def norm_next(step=None):
        if step is None:
            rows = pl.ds(pl.multiple_of(j * chunk, chunk), chunk)
        else:
            rows = slice(step * chunk, (step + 1) * chunk)
        h_next[rows, :] = _rms_rows(x_ref[...], nw_ref[...]).astype(BF16)

    def sub_dot(s):
        return _dot(h_cur[...], w_refs[s][...])

    @pl.when(i == 0)
    def _():
        norm_next()
        tail_scr[...] = jnp.zeros_like(tail_scr)

    projecting = i >= 1

    def rope_then(tail_fn, step):
        if step == 0:
            for kb in range(D_MODEL // MXU_N):
                h_cur[:, kb * MXU_N:(kb + 1) * MXU_N] = h_next[:, kb * MXU_N:(kb + 1) * MXU_N]
        norm_next(step)
        cos = cos_ref[...]
        sin = sin_ref[...]
        for s in range(N_SUB // 2):
            acc = sub_dot(s)
            for hd in range(SUB_N // RET_HEAD_DIM):
                base = s * SUB_N + hd * RET_HEAD_DIM
                x1 = acc[:, hd * RET_HEAD_DIM:hd * RET_HEAD_DIM + LANES]
                x2 = acc[:, hd * RET_HEAD_DIM + LANES:(hd + 1) * RET_HEAD_DIM]
                qkvg_ref[:, base:base + LANES] = (x1 * cos - x2 * sin).astype(BF16)
                qkvg_ref[:, base + LANES:base + RET_HEAD_DIM] = (x2 * cos + x1 * sin).astype(BF16)
        for s in range(N_SUB // 2, N_SUB):
            qkvg_ref[:, _sub_cols(s)] = tail_fn(sub_dot(s)).astype(BF16)

    @pl.when(jnp.logical_and(projecting, j == 0))
    def _():
        rope_then(lambda v: v, 0)

    @pl.when(jnp.logical_and(projecting, j == 1))
    def _():
        rope_then(_silu, 1)

    @pl.when(jnp.logical_and(projecting, j >= 2))
    def _():
        cs = j - 2
        norm_next()
        cu = sub_dot(0) * sub_dot(1)
        at_seq_start = ((i - 1) % seq_tiles) == 0
        halo_rows = slice(HALO - (CONV_K - 1), HALO)
        cu_scr[halo_rows, :] = jnp.where(at_seq_start, 0.0, tail_scr[cs, halo_rows, :])
        cu_scr[HALO:HALO + tm, :] = cu
        tail_scr[cs, halo_rows, :] = cu_scr[HALO + tm - (CONV_K - 1):HALO + tm, :]
        cw = cw_ref[...]
        conv = cw[CONV_K - 1:CONV_K, :] * cu
        for t in range(CONV_K - 1):
            shift = CONV_K - 1 - t
            conv = conv + cw[t:t + 1, :] * cu_scr[HALO - shift:HALO - shift + tm, :]
        gated = sub_dot(2) * conv
        conv_ref[...] = (gated * _silu(sub_dot(3))).astype(BF16)


def _even_in(x2d, nw, w, cos, sin, cw, to_cast, seq):
    t = x2d.shape[0]
    tm = IN_TM
    n_tiles = t // tm
    seq_tiles = seq // tm
    n_ret_steps = 4 * RET_WIDTH // IN_TN
    n_conv_steps = CONV_WIDTH // SUB_N
    nj = n_ret_steps + n_conv_steps
    chunk = tm // nj
    n_proj_steps = n_tiles * nj
    conv_first = tuple((4 * RET_WIDTH + b * CONV_WIDTH) // SUB_N for b in (1, 2, 0, 3))

    def tile(i):
        return jnp.maximum(i - 1, 0)

    def step(i, j):
        return jnp.where(i == 0, 0, j)

    def w_spec(s):
        def index(i, j):
            jj = step(i, j)
            half = N_SUB // 2
            ret_idx = half * jnp.minimum(jj, n_ret_steps - 1) + (s % half) + (s // half) * N_SUB
            conv_idx = conv_first[s] + jj - n_ret_steps
            return 0, jnp.where(jj < n_ret_steps, ret_idx, conv_idx)
        return pl.BlockSpec((D_MODEL, SUB_N), index)

    def cast_spec(a):
        rows = a.shape[0] // n_proj_steps
        assert rows * n_proj_steps == a.shape[0] and rows % BF16_ROWS == 0
        return pl.BlockSpec((rows, a.shape[1]), lambda i, j: (tile(i) * nj + step(i, j), 0))

    def table_spec():
        return pl.BlockSpec((tm, LANES), lambda i, j: (tile(i) % seq_tiles, 0))

    kern = functools.partial(_even_in_kernel, seq_tiles=seq_tiles)
    return pl.pallas_call(
        kern,
        grid=(n_tiles + 1, nj),
        in_specs=[
            pl.BlockSpec((chunk, D_MODEL), lambda i, j: (jnp.minimum(i, n_tiles - 1) * nj + j, 0)),
            pl.BlockSpec((1, D_MODEL), lambda i, j: (0, 0), pipeline_mode=pl.Buffered(1)),
            w_spec(0), w_spec(1), w_spec(2), w_spec(3),
            table_spec(), table_spec(),
            pl.BlockSpec((SUBLANES, SUB_N), lambda i, j: (0, jnp.maximum(step(i, j) - n_ret_steps, 0))),
        ] + [cast_spec(a) for a in to_cast],
        out_specs=[
            pl.BlockSpec((tm, IN_TN),
                         lambda i, j: (tile(i), jnp.minimum(step(i, j), n_ret_steps - 1))),
            pl.BlockSpec((tm, SUB_N),
                         lambda i, j: (tile(i), jnp.maximum(step(i, j) - n_ret_steps, 0))),
        ] + [cast_spec(a) for a in to_cast],
        out_shape=[
            jax.ShapeDtypeStruct((t, 4 * RET_WIDTH), BF16),
            jax.ShapeDtypeStruct((t, CONV_WIDTH), BF16),
        ] + [jax.ShapeDtypeStruct(a.shape, BF16) for a in to_cast],
        scratch_shapes=[
            pltpu.VMEM((tm, D_MODEL), BF16),
            pltpu.VMEM((tm, D_MODEL), BF16),
            pltpu.VMEM((HALO + tm, SUB_N), F32),
            pltpu.VMEM((n_conv_steps, HALO, SUB_N), F32),
        ],
        compiler_params=pltpu.CompilerParams(
            dimension_semantics=("arbitrary", "arbitrary"),
            vmem_limit_bytes=VMEM_LIMIT),
        name="even_in",
    )(x2d, nw, w, w, w, w, cos, sin, cw, *to_cast)


def _even_out_kernel(qkvg_ref, conv_ref, x_ref, w_ref,
                     dmat_ref, qdec_ref, kdec_ref, cdec_ref, nw_next_ref,
                     o_ref, h_next_ref, state_scr, y_scr):
    tm = x_ref.shape[0]
    q_ref, v_ref, k_ref, g_ref = (
        qkvg_ref.at[:, n * RET_WIDTH:(n + 1) * RET_WIDTH] for n in range(4))

    @pl.when(pl.program_id(1) == 0)
    def _():
        state_scr[...] = jnp.zeros_like(state_scr)

    for c in range(tm // RET_CHUNK):
        rows = slice(c * RET_CHUNK, (c + 1) * RET_CHUNK)
        for h in range(RET_HEADS):
            cols = slice(h * RET_HEAD_DIM, (h + 1) * RET_HEAD_DIM)
            qh = q_ref[rows, cols]
            kh = k_ref[rows, cols]
            vh = v_ref[rows, cols]
            scores = _dot_nt(qh, kh) * dmat_ref[h]
            inner = _dot(scores.astype(BF16), vh)
            state = state_scr[h]
            cross = _dot(qh, state.astype(BF16)) * qdec_ref[h]
            o = inner + cross
            kd = (kh.astype(F32) * kdec_ref[h]).astype(BF16)
            state_scr[h] = cdec_ref[h] * state + _dot_tn(kd, vh)
            ms = jnp.mean(o * o, axis=-1, keepdims=True)
            ro = o * lax.rsqrt(ms + EPS) * g_ref[rows, cols].astype(F32)
            y_scr[rows, cols] = ro.astype(BF16)
    y_scr[:, RET_WIDTH:] = conv_ref[...]

    ssq = jnp.zeros((tm, 1), F32)
    for n in range(D_MODEL // SUB_N):
        xn = x_ref[:, _sub_cols(n)] + _dot(y_scr[...], w_ref[:, _sub_cols(n)])
        o_ref[:, _sub_cols(n)] = xn
        ssq = ssq + jnp.sum(xn * xn, axis=-1, keepdims=True)
    r = lax.rsqrt(ssq * (1.0 / D_MODEL) + EPS)
    for n in range(D_MODEL // SUB_N):
        h_next_ref[:, _sub_cols(n)] = (o_ref[:, _sub_cols(n)] * r * nw_next_ref[:, _sub_cols(n)]).astype(BF16)


def _even_out(qkvg, conv, x2d, w_out, dmat, qdec, kdec, cdec, nw_next, batch, seq):
    t = x2d.shape[0]
    tm = OUT_TM
    ns = seq // tm

    def row(b, i):
        return b * ns + i

    def const_spec(shape):
        return pl.BlockSpec(shape, lambda b, i: (0,) * len(shape), pipeline_mode=pl.Buffered(1))

    return pl.pallas_call(
        _even_out_kernel,
        grid=(batch, ns),
        in_specs=[
            pl.BlockSpec((tm, 4 * RET_WIDTH), lambda b, i: (row(b, i), 0)),
            pl.BlockSpec((tm, CONV_WIDTH), lambda b, i: (row(b, i), 0)),
            pl.BlockSpec((tm, D_MODEL), lambda b, i: (row(b, i), 0)),
            const_spec((D_MODEL, D_MODEL)),
            const_spec(dmat.shape), const_spec(qdec.shape),
            const_spec(kdec.shape), const_spec(cdec.shape), const_spec(nw_next.shape),
        ],
        out_specs=[
            pl.BlockSpec((tm, D_MODEL), lambda b, i: (row(b, i), 0)),
            pl.BlockSpec((tm, D_MODEL), lambda b, i: (row(b, i), 0)),
        ],
        out_shape=[
            jax.ShapeDtypeStruct((t, D_MODEL), F32),
            jax.ShapeDtypeStruct((t, D_MODEL), BF16),
        ],
        scratch_shapes=[
            pltpu.VMEM((RET_HEADS, RET_HEAD_DIM, RET_HEAD_DIM), F32),
            pltpu.VMEM((tm, D_MODEL), BF16),
        ],
        compiler_params=pltpu.CompilerParams(
            dimension_semantics=("arbitrary", "arbitrary"),
            vmem_limit_bytes=VMEM_LIMIT),
        name="even_out",
    )(qkvg, conv, x2d, w_out, dmat, qdec, kdec, cdec, nw_next)


def _odd_in_kernel(h_ref, w_ref,
                   cos_ref, sin_ref, qgain_ref, kgain_ref, ones_ref,
                   qg_ref, kv_ref):
    q_blocks = ATTN_WIDTH // SUB_N
    kv_block = q_blocks
    gate_block0 = kv_block + 1

    def sub_dot(block):
        return _dot(h_ref[...], w_ref[:, block * SUB_N:(block + 1) * SUB_N])

    def dup_heads(x):
        lane = lax.broadcasted_iota(jnp.int32, (x.shape[0], LANES), 1)
        low = lane < ATTN_HEAD_DIM
        blocks = []
        for b in range(x.shape[1] // LANES):
            xb = x[:, b * LANES:(b + 1) * LANES]
            swapped = pltpu.roll(xb, ATTN_HEAD_DIM, axis=1)
            blocks += [jnp.where(low, xb, swapped), jnp.where(low, swapped, xb)]
        return jnp.concatenate(blocks, axis=1)

    def norm_rope(acc, gain_ref):
        width = acc.shape[1]
        sq = (acc * acc).astype(BF16)
        ss = jnp.concatenate(
            [_dot(sq[:, c * MXU_N:(c + 1) * MXU_N], ones_ref[...]) for c in range(width // MXU_N)],
            axis=1)
        r = lax.rsqrt(ss * (1.0 / ATTN_HEAD_DIM) + EPS)
        half = ATTN_HEAD_DIM // 2
        lane = lax.broadcasted_iota(jnp.int32, acc.shape, 1)
        first_half = (lane % ATTN_HEAD_DIM) < half
        partner = jnp.where(first_half,
                            pltpu.roll(acc, width - half, axis=1),
                            pltpu.roll(acc, half, axis=1))
        reps = width // LANES
        ta = cos_ref[...] * gain_ref[0:1, :]
        tb = sin_ref[...] * gain_ref[1:2, :]
        ta_w = jnp.concatenate([ta] * reps, axis=1)
        tb_w = jnp.concatenate([tb] * reps, axis=1)
        return r * (acc * ta_w + partner * tb_w)

    def kv_part():
        acc = sub_dot(kv_block)
        k = norm_rope(acc[:, :KV_WIDTH], kgain_ref)
        kv_ref[:, :2 * KV_WIDTH] = dup_heads(k).astype(BF16)
        kv_ref[:, 2 * KV_WIDTH:] = dup_heads(acc[:, KV_WIDTH:]).astype(BF16)

    kv_part()
    for s in range(q_blocks):
        qg_ref[:, _sub_cols(s)] = norm_rope(sub_dot(s), qgain_ref).astype(BF16)
    for s in range(q_blocks):
        qg_ref[:, _sub_cols(q_blocks + s)] = _silu(sub_dot(gate_block0 + s)).astype(BF16)


def _odd_in(h2d, w, cos_l, sin_l, q_gain, k_gain, ones_bd, seq):
    t = h2d.shape[0]
    tm = IN_TM
    seq_tiles = seq // tm
    assert 2 * KV_WIDTH == SUB_N

    def tab_spec():
        return pl.BlockSpec((tm, LANES), lambda i: (i % seq_tiles, 0))

    def resident_spec(shape, col):
        return pl.BlockSpec(shape, lambda i: (0, col), pipeline_mode=pl.Buffered(1))

    return pl.pallas_call(
        _odd_in_kernel,
        grid=(t // tm,),
        in_specs=[
            pl.BlockSpec((tm, D_MODEL), lambda i: (i, 0)),
            resident_spec(w.shape, 0),
            tab_spec(), tab_spec(),
            resident_spec(q_gain.shape, 0), resident_spec(k_gain.shape, 0),
            resident_spec((MXU_N, MXU_N), 0),
        ],
        out_specs=[
            pl.BlockSpec((tm, 2 * ATTN_WIDTH), lambda i: (i, 0)),
            pl.BlockSpec((tm, 4 * KV_WIDTH), lambda i: (i, 0)),
        ],
        out_shape=[
            jax.ShapeDtypeStruct((t, 2 * ATTN_WIDTH), BF16),
            jax.ShapeDtypeStruct((t, 4 * KV_WIDTH), BF16),
        ],
        compiler_params=pltpu.CompilerParams(
            dimension_semantics=("arbitrary",),
            vmem_limit_bytes=VMEM_LIMIT),
        name="odd_in",
    )(h2d, w, cos_l, sin_l, q_gain, k_gain, ones_bd)


def _odd_out_kernel(sink_ref, qg_ref, kv_ref, kvp_ref, x_ref, w_ref,
                    o_ref, y_scr):
    tm = x_ref.shape[0]
    q_ref = qg_ref.at[:, :ATTN_WIDTH]
    g_ref = qg_ref.at[:, ATTN_WIDTH:]
    at_seq_start = pl.program_id(1) == 0
    pairs = ATTN_GROUP // 2
    half_w = 2 * KV_WIDTH

    lane_kv = lax.broadcasted_iota(jnp.int32, (2 * BLOCK, LANES), 1)
    low_kv = lane_kv < ATTN_HEAD_DIM
    rr = lax.broadcasted_iota(jnp.int32, (BLOCK, BLOCK), 0)
    cc = lax.broadcasted_iota(jnp.int32, (BLOCK, BLOCK), 1)
    tri = cc <= rr
    low_o = cc < ATTN_HEAD_DIM

    for bl in range(tm // BLOCK):
        rows = slice(bl * BLOCK, (bl + 1) * BLOCK)
        for c in range(ATTN_KV_HEADS):
            kcols = slice(c * LANES, (c + 1) * LANES)
            vcols = slice(half_w + c * LANES, half_w + (c + 1) * LANES)
            if bl == 0:
                k_prev = kvp_ref[:, kcols]
                v_prev = kvp_ref[:, vcols]
            else:
                prows = slice((bl - 1) * BLOCK, bl * BLOCK)
                k_prev = kv_ref[prows, kcols]
                v_prev = kv_ref[prows, vcols]
            kk = jnp.concatenate([k_prev, kv_ref[rows, kcols]], axis=0)
            vv = jnp.concatenate([v_prev, kv_ref[rows, vcols]], axis=0)
            zero = jnp.zeros_like(kk)
            k_par = (jnp.where(low_kv, kk, zero), jnp.where(low_kv, zero, kk))
            qs = jnp.concatenate(
                [q_ref[rows, (pairs * c + a) * LANES:(pairs * c + a + 1) * LANES]
                 for a in range(pairs)], axis=0)
            probs = []
            denoms = []
            for par in range(2):
                s = _dot_nt(qs, k_par[par])
                for a in range(pairs):
                    sa = s[a * BLOCK:(a + 1) * BLOCK]
                    s_prev = sa[:, :BLOCK]
                    if bl == 0:
                        s_prev = jnp.where(at_seq_start, NEG, s_prev)
                    f = jnp.where(tri, sa[:, BLOCK:], s_prev)
                    sink = sink_ref[ATTN_GROUP * c + 2 * a + par]
                    m = jnp.maximum(jnp.max(f, axis=1, keepdims=True), sink)
                    p = jnp.exp(f - m)
                    denoms.append(jnp.sum(p, axis=1, keepdims=True) + jnp.exp(sink - m))
                    probs.append(jnp.concatenate(
                        [jnp.where(tri, 0.0, p), jnp.where(tri, p, 0.0)], axis=1).astype(BF16))
            o = _dot(jnp.concatenate(probs, axis=0), vv)
            for a in range(pairs):
                o_even = o[a * BLOCK:(a + 1) * BLOCK] / denoms[a]
                o_odd = o[(pairs + a) * BLOCK:(pairs + a + 1) * BLOCK] / denoms[pairs + a]
                ocols = slice((pairs * c + a) * LANES, (pairs * c + a + 1) * LANES)
                gated = jnp.where(low_o, o_even, o_odd) * g_ref[rows, ocols].astype(F32)
                y_scr[rows, ocols] = gated.astype(BF16)

    for n in range(D_MODEL // SUB_N):
        o_ref[:, _sub_cols(n)] = x_ref[:, _sub_cols(n)] + _dot(y_scr[...], w_ref[:, _sub_cols(n)])


def _odd_out(sinks, qg, kv, x2d, w_out, batch, seq):
    t = x2d.shape[0]
    tm = OUT_TM
    ns = seq // tm
    bpt = tm // BLOCK

    def row(b, i):
        return b * ns + i

    return pl.pallas_call(
        _odd_out_kernel,
        grid=(batch, ns),
        in_specs=[
            pl.BlockSpec(memory_space=pltpu.SMEM),
            pl.BlockSpec((tm, 2 * ATTN_WIDTH), lambda b, i: (row(b, i), 0)),
            pl.BlockSpec((tm, 4 * KV_WIDTH), lambda b, i: (row(b, i), 0)),
            pl.BlockSpec((BLOCK, 4 * KV_WIDTH),
                         lambda b, i: (jnp.maximum(row(b, i) * bpt - 1, 0), 0)),
            pl.BlockSpec((tm, D_MODEL), lambda b, i: (row(b, i), 0)),
            pl.BlockSpec((ATTN_WIDTH, D_MODEL), lambda b, i: (0, 0), pipeline_mode=pl.Buffered(1)),
        ],
        out_specs=pl.BlockSpec((tm, D_MODEL), lambda b, i: (row(b, i), 0)),
        out_shape=jax.ShapeDtypeStruct((t, D_MODEL), F32),
        scratch_shapes=[pltpu.VMEM((tm, ATTN_WIDTH), BF16)],
        compiler_params=pltpu.CompilerParams(
            dimension_semantics=("arbitrary", "arbitrary"),
            vmem_limit_bytes=VMEM_LIMIT),
        name="odd_out",
    )(sinks, qg, kv, kv, x2d, w_out)


def _rope_cos_sin(seq, dim):
    inv = 1.0 / (ROPE_THETA ** (np.arange(0, dim, 2, dtype=np.float64) / dim))
    ang = np.arange(seq, dtype=np.float64)[:, None] * inv[None, :]
    return np.cos(ang), np.sin(ang)


def _retention_tables():
    h = np.arange(RET_HEADS, dtype=np.float64)
    log_g = np.log(1.0 - 2.0 ** (-5.0 - h))
    idx = np.arange(RET_CHUNK, dtype=np.float64)
    diff = idx[:, None] - idx[None, :]
    scale = RET_HEAD_DIM ** -0.5
    intra = np.where(diff >= 0, np.exp(log_g[:, None, None] * np.maximum(diff, 0.0)), 0.0)
    q_dec = np.exp(log_g[:, None] * (idx[None, :] + 1.0))
    k_dec = np.exp(log_g[:, None] * (RET_CHUNK - 1.0 - idx[None, :]))
    c_dec = np.exp(log_g * RET_CHUNK)
    wide = (RET_HEADS, RET_CHUNK, RET_HEAD_DIM)
    dmat = jnp.asarray(intra * scale, F32)
    qdec = jnp.asarray(np.broadcast_to(q_dec[:, :, None], wide), F32)
    kdec = jnp.asarray(np.broadcast_to(k_dec[:, :, None] * scale, wide), F32)
    cdec = jnp.asarray(np.broadcast_to(c_dec[:, None, None], (RET_HEADS, 1, RET_HEAD_DIM)), F32)
    return dmat, qdec, kdec, cdec


def _head_rope_tables(seq):
    reps = LANES // ATTN_HEAD_DIM
    cos, sin = _rope_cos_sin(seq, ATTN_HEAD_DIM)
    cos_l = jnp.asarray(np.tile(np.concatenate([cos, cos], axis=1), (1, reps)), F32)
    sin_l = jnp.asarray(np.tile(np.concatenate([-sin, sin], axis=1), (1, reps)), F32)
    return cos_l, sin_l


def _head_gain_rows(norm_w, scale):
    half = ATTN_HEAD_DIM // 2
    reps = LANES // ATTN_HEAD_DIM
    w = norm_w.astype(F32) * scale
    w_partner = jnp.concatenate([w[half:], w[:half]])
    rows = jnp.stack([jnp.tile(w, reps), jnp.tile(w_partner, reps)])
    return jnp.pad(rows, ((0, SUBLANES - rows.shape[0]), (0, 0)))


def kernel(x, ev_norm_w, ev_w_in, ev_conv_w, ev_w_out, od_norm_w, od_w_in,
           od_q_norm_w, od_k_norm_w, od_sinks, od_w_out):
    batch, seq, d = x.shape
    assert d == D_MODEL and seq % IN_TM == 0 and seq % OUT_TM == 0
    x2d = x.reshape(batch * seq, d)

    cos, sin = _rope_cos_sin(seq, RET_HEAD_DIM)
    cw = jnp.pad(ev_conv_w[0], ((0, SUBLANES - CONV_K), (0, 0)))
    qkvg, conv, w_out_e, w_in_o, w_out_o = _even_in(
        x2d, ev_norm_w[0][None, :], ev_w_in[0].astype(BF16),
        jnp.asarray(cos, F32), jnp.asarray(sin, F32), cw,
        (ev_w_out[0], od_w_in[0], od_w_out[0]), seq)
    dmat, qdec, kdec, cdec = _retention_tables()
    x2d, h2d = _even_out(qkvg, conv, x2d, w_out_e, dmat, qdec, kdec, cdec,
                         od_norm_w[0][None, :], batch, seq)

    cos_l, sin_l = _head_rope_tables(seq)
    q_gain = _head_gain_rows(od_q_norm_w[0], ATTN_HEAD_DIM ** -0.5)
    k_gain = _head_gain_rows(od_k_norm_w[0], 1.0)
    head_id = np.arange(MXU_N) // ATTN_HEAD_DIM
    ones_bd = jnp.asarray(head_id[:, None] == head_id[None, :], BF16)
    qg, kv = _odd_in(h2d, w_in_o, cos_l, sin_l, q_gain, k_gain, ones_bd, seq)
    x2d = _odd_out(od_sinks[0].astype(F32), qg, kv, x2d, w_out_o, batch, seq)
    return x2d.reshape(batch, seq, d)
```

```python
import functools

import numpy as np
import jax
import jax.numpy as jnp
from jax import lax
from jax.experimental import pallas as pl
from jax.experimental.pallas import tpu as pltpu

D_MODEL = 2048
RET_HEADS = 4
RET_HEAD_DIM = 256
RET_WIDTH = RET_HEADS * RET_HEAD_DIM
CONV_WIDTH = D_MODEL - RET_WIDTH
CONV_K = 3
RET_CHUNK = 256
ATTN_HEAD_DIM = 64
ATTN_Q_HEADS = 32
ATTN_KV_HEADS = 4
ATTN_GROUP = ATTN_Q_HEADS // ATTN_KV_HEADS
ATTN_WIDTH = ATTN_Q_HEADS * ATTN_HEAD_DIM
KV_WIDTH = ATTN_KV_HEADS * ATTN_HEAD_DIM
BLOCK = 128
ROPE_THETA = 10000.0
EPS = 1e-6
NEG = -1e30

LANES = 128
SUBLANES = 8
BF16_ROWS = 16
MXU_N = 256
SUB_N = 512
N_SUB = 4
IN_TM = 1024
IN_TN = N_SUB * SUB_N
OUT_TM = 512
HALO = SUBLANES
V7X_VMEM_BYTES = 64 * 1024 * 1024
VMEM_LIMIT = V7X_VMEM_BYTES * 7 // 8

F32 = jnp.float32
BF16 = jnp.bfloat16


def _dot(a, b):
    return jnp.dot(a, b, preferred_element_type=F32)


def _dot_nt(a, b):
    return lax.dot_general(a, b, (((1,), (1,)), ((), ())), preferred_element_type=F32)


def _dot_tn(a, b):
    return lax.dot_general(a, b, (((0,), (0,)), ((), ())), preferred_element_type=F32)


def _silu(g):
    return (0.5 * g) * (1.0 + jnp.tanh(0.5 * g))


def _rms_rows(x, nw):
    ms = jnp.mean(x * x, axis=-1, keepdims=True)
    return x * lax.rsqrt(ms + EPS) * nw


def _sub_cols(s):
    return slice(s * SUB_N, (s + 1) * SUB_N)


def _even_in_kernel(x_ref, nw_ref, w0_ref, w1_ref, w2_ref, w3_ref,
                    cos_ref, sin_ref, cw_ref, c0_ref, c1_ref, c2_ref,
                    qkvg_ref, conv_ref, c0_out, c1_out, c2_out,
                    h_cur, h_next, cu_scr, tail_scr, *, seq_tiles):
    i = pl.program_id(0)
    j = pl.program_id(1)
    tm = h_cur.shape[0]
    chunk = x_ref.shape[0]
    w_refs = (w0_ref, w1_ref, w2_ref, w3_ref)

    for src, dst in ((c0_ref, c0_out), (c1_ref, c1_out), (c2_ref, c2_out)):
        dst[...] = src[...].astype(BF16)

    def norm_next(step=None):
        if step is None:
            rows = pl.ds(pl.multiple_of(j * chunk, chunk), chunk)
        else:
            rows = slice(step * chunk, (step + 1) * chunk)
        h_next[rows, :] = _rms_rows(x_ref[...], nw_ref[...]).astype(BF16)

    def sub_dot(s):
        return _dot(h_cur[...], w_refs[s][...])

    @pl.when(i == 0)
    def _():
        norm_next()
        tail_scr[...] = jnp.zeros_like(tail_scr)

    projecting = i >= 1

    def rope_then(tail_fn, step):
        if step == 0:
            for kb in range(D_MODEL // MXU_N):
                h_cur[:, kb * MXU_N:(kb + 1) * MXU_N] = h_next[:, kb * MXU_N:(kb + 1) * MXU_N]
        norm_next(step)
        cos = cos_ref[...]
        sin = sin_ref[...]
        for s in range(N_SUB // 2):
            acc = sub_dot(s)
            for hd in range(SUB_N // RET_HEAD_DIM):
                base = s * SUB_N + hd * RET_HEAD_DIM
                x1 = acc[:, hd * RET_HEAD_DIM:hd * RET_HEAD_DIM + LANES]
                x2 = acc[:, hd * RET_HEAD_DIM + LANES:(hd + 1) * RET_HEAD_DIM]
                qkvg_ref[:, base:base + LANES] = (x1 * cos - x2 * sin).astype(BF16)
                qkvg_ref[:, base + LANES:base + RET_HEAD_DIM] = (x2 * cos + x1 * sin).astype(BF16)
        for s in range(N_SUB // 2, N_SUB):
            qkvg_ref[:, _sub_cols(s)] = tail_fn(sub_dot(s)).astype(BF16)

    @pl.when(jnp.logical_and(projecting, j == 0))
    def _():
        rope_then(lambda v: v, 0)

    @pl.when(jnp.logical_and(projecting, j == 1))
    def _():
        rope_then(_silu, 1)

    @pl.when(jnp.logical_and(projecting, j >= 2))
    def _():
        cs = j - 2
        norm_next()
        cu = sub_dot(0) * sub_dot(1)
        at_seq_start = ((i - 1) % seq_tiles) == 0
        halo_rows = slice(HALO - (CONV_K - 1), HALO)
        cu_scr[halo_rows, :] = jnp.where(at_seq_start, 0.0, tail_scr[cs, halo_rows, :])
        cu_scr[HALO:HALO + tm, :] = cu
        tail_scr[cs, halo_rows, :] = cu_scr[HALO + tm - (CONV_K - 1):HALO + tm, :]
        cw = cw_ref[...]
        conv = cw[CONV_K - 1:CONV_K, :] * cu
        for t in range(CONV_K - 1):
            shift = CONV_K - 1 - t
            conv = conv + cw[t:t + 1, :] * cu_scr[HALO - shift:HALO - shift + tm, :]
        gated = sub_dot(2) * conv
        conv_ref[...] = (gated * _silu(sub_dot(3))).astype(BF16)


def _even_in(x2d, nw, w, cos, sin, cw, to_cast, seq):
    t = x2d.shape[0]
    tm = IN_TM
    n_tiles = t // tm
    seq_tiles = seq // tm
    n_ret_steps = 4 * RET_WIDTH // IN_TN
    n_conv_steps = CONV_WIDTH // SUB_N
    nj = n_ret_steps + n_conv_steps
    chunk = tm // nj
    n_proj_steps = n_tiles * nj
    conv_first = tuple((4 * RET_WIDTH + b * CONV_WIDTH) // SUB_N for b in (1, 2, 0, 3))

    def tile(i):
        return jnp.maximum(i - 1, 0)

    def step(i, j):
        return jnp.where(i == 0, 0, j)

    def w_spec(s):
        def index(i, j):
            jj = step(i, j)
            half = N_SUB // 2
            ret_idx = half * jnp.minimum(jj, n_ret_steps - 1) + (s % half) + (s // half) * N_SUB
            conv_idx = conv_first[s] + jj - n_ret_steps
            return 0, jnp.where(jj < n_ret_steps, ret_idx, conv_idx)
        return pl.BlockSpec((D_MODEL, SUB_N), index)

    def cast_spec(a):
        rows = a.shape[0] // n_proj_steps
        assert rows * n_proj_steps == a.shape[0] and rows % BF16_ROWS == 0
        return pl.BlockSpec((rows, a.shape[1]), lambda i, j: (tile(i) * nj + step(i, j), 0))

    def table_spec():
        return pl.BlockSpec((tm, LANES), lambda i, j: (tile(i) % seq_tiles, 0))

    kern = functools.partial(_even_in_kernel, seq_tiles=seq_tiles)
    return pl.pallas_call(
        kern,
        grid=(n_tiles + 1, nj),
        in_specs=[
            pl.BlockSpec((chunk, D_MODEL), lambda i, j: (jnp.minimum(i, n_tiles - 1) * nj + j, 0)),
            pl.BlockSpec((1, D_MODEL), lambda i, j: (0, 0), pipeline_mode=pl.Buffered(1)),
            w_spec(0), w_spec(1), w_spec(2), w_spec(3),
            table_spec(), table_spec(),
            pl.BlockSpec((SUBLANES, SUB_N), lambda i, j: (0, jnp.maximum(step(i, j) - n_ret_steps, 0))),
        ] + [cast_spec(a) for a in to_cast],
        out_specs=[
            pl.BlockSpec((tm, IN_TN),
                         lambda i, j: (tile(i), jnp.minimum(step(i, j), n_ret_steps - 1))),
            pl.BlockSpec((tm, SUB_N),
                         lambda i, j: (tile(i), jnp.maximum(step(i, j) - n_ret_steps, 0))),
        ] + [cast_spec(a) for a in to_cast],
        out_shape=[
            jax.ShapeDtypeStruct((t, 4 * RET_WIDTH), BF16),
            jax.ShapeDtypeStruct((t, CONV_WIDTH), BF16),
        ] + [jax.ShapeDtypeStruct(a.shape, BF16) for a in to_cast],
        scratch_shapes=[
            pltpu.VMEM((tm, D_MODEL), BF16),
            pltpu.VMEM((tm, D_MODEL), BF16),
            pltpu.VMEM((HALO + tm, SUB_N), F32),
            pltpu.VMEM((n_conv_steps, HALO, SUB_N), F32),
        ],
        compiler_params=pltpu.CompilerParams(
            dimension_semantics=("arbitrary", "arbitrary"),
            vmem_limit_bytes=VMEM_LIMIT),
        name="even_in",
    )(x2d, nw, w, w, w, w, cos, sin, cw, *to_cast)


def _even_out_kernel(qkvg_ref, conv_ref, x_ref, w_ref,
                     dmat_ref, qdec_ref, kdec_ref, cdec_ref, nw_next_ref,
                     o_ref, h_next_ref, state_scr, y_scr):
    tm = x_ref.shape[0]
    q_ref, v_ref, k_ref, g_ref = (
        qkvg_ref.at[:, n * RET_WIDTH:(n + 1) * RET_WIDTH] for n in range(4))

    @pl.when(pl.program_id(1) == 0)
    def _():
        state_scr[...] = jnp.zeros_like(state_scr)

    for c in range(tm // RET_CHUNK):
        rows = slice(c * RET_CHUNK, (c + 1) * RET_CHUNK)
        for h in range(RET_HEADS):
            cols = slice(h * RET_HEAD_DIM, (h + 1) * RET_HEAD_DIM)
            qh = q_ref[rows, cols]
            kh = k_ref[rows, cols]
            vh = v_ref[rows, cols]
            scores = _dot_nt(qh, kh) * dmat_ref[h]
            inner = _dot(scores.astype(BF16), vh)
            state = state_scr[h]
            cross = _dot(qh, state.astype(BF16)) * qdec_ref[h]
            o = inner + cross
            kd = (kh.astype(F32) * kdec_ref[h]).astype(BF16)
            state_scr[h] = cdec_ref[h] * state + _dot_tn(kd, vh)
            ms = jnp.mean(o * o, axis=-1, keepdims=True)
            ro = o * lax.rsqrt(ms + EPS) * g_ref[rows, cols].astype(F32)
            y_scr[rows, cols] = ro.astype(BF16)
    y_scr[:, RET_WIDTH:] = conv_ref[...]

    ssq = jnp.zeros((tm, 1), F32)
    for n in range(D_MODEL // SUB_N):
        xn = x_ref[:, _sub_cols(n)] + _dot(y_scr[...], w_ref[:, _sub_cols(n)])
        o_ref[:, _sub_cols(n)] = xn
        ssq = ssq + jnp.sum(xn * xn, axis=-1, keepdims=True)
    r = lax.rsqrt(ssq * (1.0 / D_MODEL) + EPS)
    for n in range(D_MODEL // SUB_N):
        h_next_ref[:, _sub_cols(n)] = (o_ref[:, _sub_cols(n)] * r * nw_next_ref[:, _sub_cols(n)]).astype(BF16)


def _even_out(qkvg, conv, x2d, w_out, dmat, qdec, kdec, cdec, nw_next, batch, seq):
    t = x2d.shape[0]
    tm = OUT_TM
    ns = seq // tm

    def row(b, i):
        return b * ns + i

    def const_spec(shape):
        return pl.BlockSpec(shape, lambda b, i: (0,) * len(shape), pipeline_mode=pl.Buffered(1))

    return pl.pallas_call(
        _even_out_kernel,
        grid=(batch, ns),
        in_specs=[
            pl.BlockSpec((tm, 4 * RET_WIDTH), lambda b, i: (row(b, i), 0)),
            pl.BlockSpec((tm, CONV_WIDTH), lambda b, i: (row(b, i), 0)),
            pl.BlockSpec((tm, D_MODEL), lambda b, i: (row(b, i), 0)),
            const_spec((D_MODEL, D_MODEL)),
            const_spec(dmat.shape), const_spec(qdec.shape),
            const_spec(kdec.shape), const_spec(cdec.shape), const_spec(nw_next.shape),
        ],
        out_specs=[
            pl.BlockSpec((tm, D_MODEL), lambda b, i: (row(b, i), 0)),
            pl.BlockSpec((tm, D_MODEL), lambda b, i: (row(b, i), 0)),
        ],
        out_shape=[
            jax.ShapeDtypeStruct((t, D_MODEL), F32),
            jax.ShapeDtypeStruct((t, D_MODEL), BF16),
        ],
        scratch_shapes=[
            pltpu.VMEM((RET_HEADS, RET_HEAD_DIM, RET_HEAD_DIM), F32),
            pltpu.VMEM((tm, D_MODEL), BF16),
        ],
        compiler_params=pltpu.CompilerParams(
            dimension_semantics=("arbitrary", "arbitrary"),
            vmem_limit_bytes=VMEM_LIMIT),
        name="even_out",
    )(qkvg, conv, x2d, w_out, dmat, qdec, kdec, cdec, nw_next)


def _odd_in_kernel(h_ref, w_ref,
                   cos_ref, sin_ref, qgain_ref, kgain_ref, ones_ref,
                   qg_ref, kv_ref):
    q_blocks = ATTN_WIDTH // SUB_N
    kv_block = q_blocks
    gate_block0 = kv_block + 1

    def sub_dot(block):
        return _dot(h_ref[...], w_ref[:, block * SUB_N:(block + 1) * SUB_N])

    def dup_heads(x):
        lane = lax.broadcasted_iota(jnp.int32, (x.shape[0], LANES), 1)
        low = lane < ATTN_HEAD_DIM
        blocks = []
        for b in range(x.shape[1] // LANES):
            xb = x[:, b * LANES:(b + 1) * LANES]
            swapped = pltpu.roll(xb, ATTN_HEAD_DIM, axis=1)
            blocks += [jnp.where(low, xb, swapped), jnp.where(low, swapped, xb)]
        return jnp.concatenate(blocks, axis=1)

    def norm_rope(acc, gain_ref):
        width = acc.shape[1]
        sq = (acc * acc).astype(BF16)
        ss = jnp.concatenate(
            [_dot(sq[:, c * MXU_N:(c + 1) * MXU_N], ones_ref[...]) for c in range(width // MXU_N)],
            axis=1)
        r = lax.rsqrt(ss * (1.0 / ATTN_HEAD_DIM) + EPS)
        half = ATTN_HEAD_DIM // 2
        lane = lax.broadcasted_iota(jnp.int32, acc.shape, 1)
        first_half = (lane % ATTN_HEAD_DIM) < half
        partner = jnp.where(first_half,
                            pltpu.roll(acc, width - half, axis=1),
                            pltpu.roll(acc, half, axis=1))
        reps = width // LANES
        ta = cos_ref[...] * gain_ref[0:1, :]
        tb = sin_ref[...] * gain_ref[1:2, :]
        ta_w = jnp.concatenate([ta] * reps, axis=1)
        tb_w = jnp.concatenate([tb] * reps, axis=1)
        return r * (acc * ta_w + partner * tb_w)

    def kv_part():
        acc = sub_dot(kv_block)
        k = norm_rope(acc[:, :KV_WIDTH], kgain_ref)
        kv_ref[:, :2 * KV_WIDTH] = dup_heads(k).astype(BF16)
        kv_ref[:, 2 * KV_WIDTH:] = dup_heads(acc[:, KV_WIDTH:]).astype(BF16)

    kv_part()
    for s in range(q_blocks):
        qg_ref[:, _sub_cols(s)] = norm_rope(sub_dot(s), qgain_ref).astype(BF16)
    for s in range(q_blocks):
        qg_ref[:, _sub_cols(q_blocks + s)] = _silu(sub_dot(gate_block0 + s)).astype(BF16)


def _odd_in(h2d, w, cos_l, sin_l, q_gain, k_gain, ones_bd, seq):
    t = h2d.shape[0]
    tm = IN_TM
    seq_tiles = seq // tm
    assert 2 * KV_WIDTH == SUB_N

    def tab_spec():
        return pl.BlockSpec((tm, LANES), lambda i: (i % seq_tiles, 0))

    def resident_spec(shape, col):
        return pl.BlockSpec(shape, lambda i: (0, col), pipeline_mode=pl.Buffered(1))

    return pl.pallas_call(
        _odd_in_kernel,
        grid=(t // tm,),
        in_specs=[
            pl.BlockSpec((tm, D_MODEL), lambda i: (i, 0)),
            resident_spec(w.shape, 0),
            tab_spec(), tab_spec(),
            resident_spec(q_gain.shape, 0), resident_spec(k_gain.shape, 0),
            resident_spec((MXU_N, MXU_N), 0),
        ],
        out_specs=[
            pl.BlockSpec((tm, 2 * ATTN_WIDTH), lambda i: (i, 0)),
            pl.BlockSpec((tm, 4 * KV_WIDTH), lambda i: (i, 0)),
        ],
        out_shape=[
            jax.ShapeDtypeStruct((t, 2 * ATTN_WIDTH), BF16),
            jax.ShapeDtypeStruct((t, 4 * KV_WIDTH), BF16),
        ],
        compiler_params=pltpu.CompilerParams(
            dimension_semantics=("arbitrary",),
            vmem_limit_bytes=VMEM_LIMIT),
        name="odd_in",
    )(h2d, w, cos_l, sin_l, q_gain, k_gain, ones_bd)


def _odd_out_kernel(sink_ref, qg_ref, kv_ref, kvp_ref, x_ref, w_ref,
                    o_ref, y_scr):
    tm = x_ref.shape[0]
    q_ref = qg_ref.at[:, :ATTN_WIDTH]
    g_ref = qg_ref.at[:, ATTN_WIDTH:]
    at_seq_start = pl.program_id(1) == 0
    pairs = ATTN_GROUP // 2
    half_w = 2 * KV_WIDTH

    lane_kv = lax.broadcasted_iota(jnp.int32, (2 * BLOCK, LANES), 1)
    low_kv = lane_kv < ATTN_HEAD_DIM
    rr = lax.broadcasted_iota(jnp.int32, (BLOCK, BLOCK), 0)
    cc = lax.broadcasted_iota(jnp.int32, (BLOCK, BLOCK), 1)
    tri = cc <= rr
    low_o = cc < ATTN_HEAD_DIM

    for bl in range(tm // BLOCK):
        rows = slice(bl * BLOCK, (bl + 1) * BLOCK)
        for c in range(ATTN_KV_HEADS):
            kcols = slice(c * LANES, (c + 1) * LANES)
            vcols = slice(half_w + c * LANES, half_w + (c + 1) * LANES)
            if bl == 0:
                k_prev = kvp_ref[:, kcols]
                v_prev = kvp_ref[:, vcols]
            else:
                prows = slice((bl - 1) * BLOCK, bl * BLOCK)
                k_prev = kv_ref[prows, kcols]
                v_prev = kv_ref[prows, vcols]
            kk = jnp.concatenate([k_prev, kv_ref[rows, kcols]], axis=0)
            vv = jnp.concatenate([v_prev, kv_ref[rows, vcols]], axis=0)
            zero = jnp.zeros_like(kk)
            k_par = (jnp.where(low_kv, kk, zero), jnp.where(low_kv, zero, kk))
            qs = jnp.concatenate(
                [q_ref[rows, (pairs * c + a) * LANES:(pairs * c + a + 1) * LANES]
                 for a in range(pairs)], axis=0)
            probs = []
            denoms = []
            for par in range(2):
                s = _dot_nt(qs, k_par[par])
                for a in range(pairs):
                    sa = s[a * BLOCK:(a + 1) * BLOCK]
                    s_prev = sa[:, :BLOCK]
                    if bl == 0:
                        s_prev = jnp.where(at_seq_start, NEG, s_prev)
                    f = jnp.where(tri, sa[:, BLOCK:], s_prev)
                    sink = sink_ref[ATTN_GROUP * c + 2 * a + par]
                    m = jnp.maximum(jnp.max(f, axis=1, keepdims=True), sink)
                    p = jnp.exp(f - m)
                    denoms.append(jnp.sum(p, axis=1, keepdims=True) + jnp.exp(sink - m))
                    probs.append(jnp.concatenate(
                        [jnp.where(tri, 0.0, p), jnp.where(tri, p, 0.0)], axis=1).astype(BF16))
            o = _dot(jnp.concatenate(probs, axis=0), vv)
            for a in range(pairs):
                o_even = o[a * BLOCK:(a + 1) * BLOCK] / denoms[a]
                o_odd = o[(pairs + a) * BLOCK:(pairs + a + 1) * BLOCK] / denoms[pairs + a]
                ocols = slice((pairs * c + a) * LANES, (pairs * c + a + 1) * LANES)
                gated = jnp.where(low_o, o_even, o_odd) * g_ref[rows, ocols].astype(F32)
                y_scr[rows, ocols] = gated.astype(BF16)

    for n in range(D_MODEL // SUB_N):
        o_ref[:, _sub_cols(n)] = x_ref[:, _sub_cols(n)] + _dot(y_scr[...], w_ref[:, _sub_cols(n)])


def _odd_out(sinks, qg, kv, x2d, w_out, batch, seq):
    t = x2d.shape[0]
    tm = OUT_TM
    ns = seq // tm
    bpt = tm // BLOCK

    def row(b, i):
        return b * ns + i

    return pl.pallas_call(
        _odd_out_kernel,
        grid=(batch, ns),
        in_specs=[
            pl.BlockSpec(memory_space=pltpu.SMEM),
            pl.BlockSpec((tm, 2 * ATTN_WIDTH), lambda b, i: (row(b, i), 0)),
            pl.BlockSpec((tm, 4 * KV_WIDTH), lambda b, i: (row(b, i), 0)),
            pl.BlockSpec((BLOCK, 4 * KV_WIDTH),
                         lambda b, i: (jnp.maximum(row(b, i) * bpt - 1, 0), 0)),
            pl.BlockSpec((tm, D_MODEL), lambda b, i: (row(b, i), 0)),
            pl.BlockSpec((ATTN_WIDTH, D_MODEL), lambda b, i: (0, 0), pipeline_mode=pl.Buffered(1)),
        ],
        out_specs=pl.BlockSpec((tm, D_MODEL), lambda b, i: (row(b, i), 0)),
        out_shape=jax.ShapeDtypeStruct((t, D_MODEL), F32),
        scratch_shapes=[pltpu.VMEM((tm, ATTN_WIDTH), BF16)],
        compiler_params=pltpu.CompilerParams(
            dimension_semantics=("arbitrary", "arbitrary"),
            vmem_limit_bytes=VMEM_LIMIT),
        name="odd_out",
    )(sinks, qg, kv, kv, x2d, w_out)


def _rope_cos_sin(seq, dim):
    inv = 1.0 / (ROPE_THETA ** (np.arange(0, dim, 2, dtype=np.float64) / dim))
    ang = np.arange(seq, dtype=np.float64)[:, None] * inv[None, :]
    return np.cos(ang), np.sin(ang)


def _retention_tables():
    h = np.arange(RET_HEADS, dtype=np.float64)
    log_g = np.log(1.0 - 2.0 ** (-5.0 - h))
    idx = np.arange(RET_CHUNK, dtype=np.float64)
    diff = idx[:, None] - idx[None, :]
    scale = RET_HEAD_DIM ** -0.5
    intra = np.where(diff >= 0, np.exp(log_g[:, None, None] * np.maximum(diff, 0.0)), 0.0)
    q_dec = np.exp(log_g[:, None] * (idx[None, :] + 1.0))
    k_dec = np.exp(log_g[:, None] * (RET_CHUNK - 1.0 - idx[None, :]))
    c_dec = np.exp(log_g * RET_CHUNK)
    wide = (RET_HEADS, RET_CHUNK, RET_HEAD_DIM)
    dmat = jnp.asarray(intra * scale, F32)
    qdec = jnp.asarray(np.broadcast_to(q_dec[:, :, None], wide), F32)
    kdec = jnp.asarray(np.broadcast_to(k_dec[:, :, None] * scale, wide), F32)
    cdec = jnp.asarray(np.broadcast_to(c_dec[:, None, None], (RET_HEADS, 1, RET_HEAD_DIM)), F32)
    return dmat, qdec, kdec, cdec


def _head_rope_tables(seq):
    reps = LANES // ATTN_HEAD_DIM
    cos, sin = _rope_cos_sin(seq, ATTN_HEAD_DIM)
    cos_l = jnp.asarray(np.tile(np.concatenate([cos, cos], axis=1), (1, reps)), F32)
    sin_l = jnp.asarray(np.tile(np.concatenate([-sin, sin], axis=1), (1, reps)), F32)
    return cos_l, sin_l


def _head_gain_rows(norm_w, scale):
    half = ATTN_HEAD_DIM // 2
    reps = LANES // ATTN_HEAD_DIM
    w = norm_w.astype(F32) * scale
    w_partner = jnp.concatenate([w[half:], w[:half]])
    rows = jnp.stack([jnp.tile(w, reps), jnp.tile(w_partner, reps)])
    return jnp.pad(rows, ((0, SUBLANES - rows.shape[0]), (0, 0)))


def kernel(x, ev_norm_w, ev_w_in, ev_conv_w, ev_w_out, od_norm_w, od_w_in,
           od_q_norm_w, od_k_norm_w, od_sinks, od_w_out):
    batch, seq, d = x.shape
    assert d == D_MODEL and seq % IN_TM == 0 and seq % OUT_TM == 0
    x2d = x.reshape(batch * seq, d)

    cos, sin = _rope_cos_sin(seq, RET_HEAD_DIM)
    cw = jnp.pad(ev_conv_w[0], ((0, SUBLANES - CONV_K), (0, 0)))
    qkvg, conv, w_out_e, w_in_o, w_out_o = _even_in(
        x2d, ev_norm_w[0][None, :], ev_w_in[0].astype(BF16),
        jnp.asarray(cos, F32), jnp.asarray(sin, F32), cw,
        (ev_w_out[0], od_w_in[0], od_w_out[0]), seq)
    dmat, qdec, kdec, cdec = _retention_tables()
    x2d, h2d = _even_out(qkvg, conv, x2d, w_out_e, dmat, qdec, kdec, cdec,
                         od_norm_w[0][None, :], batch, seq)

    cos_l, sin_l = _head_rope_tables(seq)
    q_gain = _head_gain_rows(od_q_norm_w[0], ATTN_HEAD_DIM ** -0.5)
    k_gain = _head_gain_rows(od_k_norm_w[0], 1.0)
    head_id = np.arange(MXU_N) // ATTN_HEAD_DIM
    ones_bd = jnp.asarray(head_id[:, None] == head_id[None, :], BF16)
    qg, kv = _odd_in(h2d, w_in_o, cos_l, sin_l, q_gain, k_gain, ones_bd, seq)
    x2d = _odd_out(od_sinks[0].astype(F32), qg, kv, x2d, w_out_o, batch, seq)
    return x2d.reshape(batch, seq, d)
```

```python
import functools

import numpy as np
import jax
import jax.numpy as jnp
from jax import lax
from jax.experimental import pallas as pl
from jax.experimental.pallas import tpu as pltpu

D_MODEL = 2048
RET_HEADS = 4
RET_HEAD_DIM = 256
RET_WIDTH = RET_HEADS * RET_HEAD_DIM
CONV_WIDTH = D_MODEL - RET_WIDTH
CONV_K = 3
RET_CHUNK = 256
ATTN_HEAD_DIM = 64
ATTN_Q_HEADS = 32
ATTN_KV_HEADS = 4
ATTN_GROUP = ATTN_Q_HEADS // ATTN_KV_HEADS
ATTN_WIDTH = ATTN_Q_HEADS * ATTN_HEAD_DIM
KV_WIDTH = ATTN_KV_HEADS * ATTN_HEAD_DIM
BLOCK = 128
ROPE_THETA = 10000.0
EPS = 1e-6
NEG = -1e30

LANES = 128
SUBLANES = 8
BF16_ROWS = 16
MXU_N = 256
SUB_N = 512
N_SUB = 4
IN_TM = 1024
IN_TN = N_SUB * SUB_N
OUT_TM = 512
HALO = SUBLANES
V7X_VMEM_BYTES = 64 * 1024 * 1024
VMEM_LIMIT = V7X_VMEM_BYTES * 7 // 8

F32 = jnp.float32
BF16 = jnp.bfloat16


def _dot(a, b):
    return jnp.dot(a, b, preferred_element_type=F32)


def _dot_nt(a, b):
    return lax.dot_general(a, b, (((1,), (1,)), ((), ())), preferred_element_type=F32)


def _dot_tn(a, b):
    return lax.dot_general(a, b, (((0,), (0,)), ((), ())), preferred_element_type=F32)


def _silu(g):
    return (0.5 * g) * (1.0 + jnp.tanh(0.5 * g))


def _sub_cols(s):
    return slice(s * SUB_N, (s + 1) * SUB_N)


def _even_in_kernel(x_ref, w0_ref, w1_ref, w2_ref, w3_ref,
                    cos_ref, sin_ref, cw_ref, c0_ref, c1_ref, c2_ref,
                    qkvg_ref, conv_ref, c0_out, c1_out, c2_out,
                    h_cur, h_next, cu_scr, tail_scr, *, seq_tiles):
    i = pl.program_id(0)
    j = pl.program_id(1)
    tm = h_cur.shape[0]
    chunk = x_ref.shape[0]
    w_refs = (w0_ref, w1_ref, w2_ref, w3_ref)

    for src, dst in ((c0_ref, c0_out), (c1_ref, c1_out), (c2_ref, c2_out)):
        dst[...] = src[...].astype(BF16)

    def norm_next(step=None):
        if step is None:
            rows = pl.ds(pl.multiple_of(j * chunk, chunk), chunk)
        else:
            rows = slice(step * chunk, (step + 1) * chunk)
        x = x_ref[...]
        ms = jnp.mean(x * x, axis=-1, keepdims=True)
        h_next[rows, :] = (x * lax.rsqrt(ms + EPS)).astype(BF16)

    def sub_dot(s):
        return _dot(h_cur[...], w_refs[s][...])

    @pl.when(i == 0)
    def _():
        norm_next()
        tail_scr[...] = jnp.zeros_like(tail_scr)

    projecting = i >= 1

    def rope_then(tail_fn, step):
        if step == 0:
            for kb in range(D_MODEL // MXU_N):
                h_cur[:, kb * MXU_N:(kb + 1) * MXU_N] = h_next[:, kb * MXU_N:(kb + 1) * MXU_N]
        norm_next(step)
        cos = cos_ref[...]
        sin = sin_ref[...]
        for s in range(N_SUB // 2):
            acc = sub_dot(s)
            for hd in range(SUB_N // RET_HEAD_DIM):
                base = s * SUB_N + hd * RET_HEAD_DIM
                x1 = acc[:, hd * RET_HEAD_DIM:hd * RET_HEAD_DIM + LANES]
                x2 = acc[:, hd * RET_HEAD_DIM + LANES:(hd + 1) * RET_HEAD_DIM]
                qkvg_ref[:, base:base + LANES] = (x1 * cos - x2 * sin).astype(BF16)
                qkvg_ref[:, base + LANES:base + RET_HEAD_DIM] = (x2 * cos + x1 * sin).astype(BF16)
        for s in range(N_SUB // 2, N_SUB):
            qkvg_ref[:, _sub_cols(s)] = tail_fn(sub_dot(s)).astype(BF16)

    @pl.when(jnp.logical_and(projecting, j == 0))
    def _():
        rope_then(lambda v: v, 0)

    @pl.when(jnp.logical_and(projecting, j == 1))
    def _():
        rope_then(_silu, 1)

    @pl.when(jnp.logical_and(projecting, j >= 2))
    def _():
        cs = j - 2
        norm_next()
        cu = sub_dot(0) * sub_dot(1)
        at_seq_start = ((i - 1) % seq_tiles) == 0
        halo_rows = slice(HALO - (CONV_K - 1), HALO)
        cu_scr[halo_rows, :] = jnp.where(at_seq_start, 0.0, tail_scr[cs, halo_rows, :])
        cu_scr[HALO:HALO + tm, :] = cu
        tail_scr[cs, halo_rows, :] = cu_scr[HALO + tm - (CONV_K - 1):HALO + tm, :]
        cw = cw_ref[...]
        conv = cw[CONV_K - 1:CONV_K, :] * cu
        for t in range(CONV_K - 1):
            shift = CONV_K - 1 - t
            conv = conv + cw[t:t + 1, :] * cu_scr[HALO - shift:HALO - shift + tm, :]
        gated = sub_dot(2) * conv
        conv_ref[...] = (gated * _silu(sub_dot(3))).astype(BF16)


def _even_in(x2d, w, cos, sin, cw, to_cast, seq):
    t = x2d.shape[0]
    tm = IN_TM
    n_tiles = t // tm
    seq_tiles = seq // tm
    n_ret_steps = 4 * RET_WIDTH // IN_TN
    n_conv_steps = CONV_WIDTH // SUB_N
    nj = n_ret_steps + n_conv_steps
    chunk = tm // nj
    n_proj_steps = n_tiles * nj
    conv_first = tuple((4 * RET_WIDTH + b * CONV_WIDTH) // SUB_N for b in (1, 2, 0, 3))

    def tile(i):
        return jnp.maximum(i - 1, 0)

    def step(i, j):
        return jnp.where(i == 0, 0, j)

    def w_spec(s):
        def index(i, j):
            jj = step(i, j)
            half = N_SUB // 2
            ret_idx = half * jnp.minimum(jj, n_ret_steps - 1) + (s % half) + (s // half) * N_SUB
            conv_idx = conv_first[s] + jj - n_ret_steps
            return 0, jnp.where(jj < n_ret_steps, ret_idx, conv_idx)
        return pl.BlockSpec((D_MODEL, SUB_N), index)

    def cast_spec(a):
        rows = a.shape[0] // n_proj_steps
        assert rows * n_proj_steps == a.shape[0] and rows % BF16_ROWS == 0
        return pl.BlockSpec((rows, a.shape[1]), lambda i, j: (tile(i) * nj + step(i, j), 0))

    def table_spec():
        return pl.BlockSpec((tm, LANES), lambda i, j: (tile(i) % seq_tiles, 0))

    kern = functools.partial(_even_in_kernel, seq_tiles=seq_tiles)
    return pl.pallas_call(
        kern,
        grid=(n_tiles + 1, nj),
        in_specs=[
            pl.BlockSpec((chunk, D_MODEL), lambda i, j: (jnp.minimum(i, n_tiles - 1) * nj + j, 0)),
            w_spec(0), w_spec(1), w_spec(2), w_spec(3),
            table_spec(), table_spec(),
            pl.BlockSpec((SUBLANES, SUB_N), lambda i, j: (0, jnp.maximum(step(i, j) - n_ret_steps, 0))),
        ] + [cast_spec(a) for a in to_cast],
        out_specs=[
            pl.BlockSpec((tm, IN_TN),
                         lambda i, j: (tile(i), jnp.minimum(step(i, j), n_ret_steps - 1))),
            pl.BlockSpec((tm, SUB_N),
                         lambda i, j: (tile(i), jnp.maximum(step(i, j) - n_ret_steps, 0))),
        ] + [cast_spec(a) for a in to_cast],
        out_shape=[
            jax.ShapeDtypeStruct((t, 4 * RET_WIDTH), BF16),
            jax.ShapeDtypeStruct((t, CONV_WIDTH), BF16),
        ] + [jax.ShapeDtypeStruct(a.shape, BF16) for a in to_cast],
        scratch_shapes=[
            pltpu.VMEM((tm, D_MODEL), BF16),
            pltpu.VMEM((tm, D_MODEL), BF16),
            pltpu.VMEM((HALO + tm, SUB_N), F32),
            pltpu.VMEM((n_conv_steps, HALO, SUB_N), F32),
        ],
        compiler_params=pltpu.CompilerParams(
            dimension_semantics=("arbitrary", "arbitrary"),
            vmem_limit_bytes=VMEM_LIMIT),
        name="even_in",
    )(x2d, w, w, w, w, cos, sin, cw, *to_cast)


def _even_out_kernel(qkvg_ref, conv_ref, x_ref, w_ref,
                     dmat_ref, qdec_ref, kdec_ref, cdec_ref, nw_next_ref,
                     o_ref, h_next_ref, state_scr, y_scr):
    tm = x_ref.shape[0]
    q_ref, v_ref, k_ref, g_ref = (
        qkvg_ref.at[:, n * RET_WIDTH:(n + 1) * RET_WIDTH] for n in range(4))

    @pl.when(pl.program_id(1) == 0)
    def _():
        state_scr[...] = jnp.zeros_like(state_scr)

    for c in range(tm // RET_CHUNK):
        rows = slice(c * RET_CHUNK, (c + 1) * RET_CHUNK)
        for h in range(RET_HEADS):
            cols = slice(h * RET_HEAD_DIM, (h + 1) * RET_HEAD_DIM)
            qh = q_ref[rows, cols]
            kh = k_ref[rows, cols]
            vh = v_ref[rows, cols]
            scores = _dot_nt(qh, kh) * dmat_ref[h]
            inner = _dot(scores.astype(BF16), vh)
            state = state_scr[h]
            cross = _dot(qh, state.astype(BF16)) * qdec_ref[h]
            o = inner + cross
            kd = (kh.astype(F32) * kdec_ref[h]).astype(BF16)
            state_scr[h] = cdec_ref[h] * state + _dot_tn(kd, vh)
            ms = jnp.mean(o * o, axis=-1, keepdims=True)
            ro = o * lax.rsqrt(ms + EPS) * g_ref[rows, cols].astype(F32)
            y_scr[rows, cols] = ro.astype(BF16)
    y_scr[:, RET_WIDTH:] = conv_ref[...]

    ssq = jnp.zeros((tm, 1), F32)
    for n in range(D_MODEL // SUB_N):
        xn = x_ref[:, _sub_cols(n)] + _dot(y_scr[...], w_ref[:, _sub_cols(n)])
        o_ref[:, _sub_cols(n)] = xn
        ssq = ssq + jnp.sum(xn * xn, axis=-1, keepdims=True)
    r = lax.rsqrt(ssq * (1.0 / D_MODEL) + EPS)
    for n in range(D_MODEL // SUB_N):
        h_next_ref[:, _sub_cols(n)] = (o_ref[:, _sub_cols(n)] * r * nw_next_ref[:, _sub_cols(n)]).astype(BF16)


def _even_out(qkvg, conv, x2d, w_out, dmat, qdec, kdec, cdec, nw_next, batch, seq):
    t = x2d.shape[0]
    tm = OUT_TM
    ns = seq // tm

    def row(b, i):
        return b * ns + i

    def const_spec(shape):
        return pl.BlockSpec(shape, lambda b, i: (0,) * len(shape), pipeline_mode=pl.Buffered(1))

    return pl.pallas_call(
        _even_out_kernel,
        grid=(batch, ns),
        in_specs=[
            pl.BlockSpec((tm, 4 * RET_WIDTH), lambda b, i: (row(b, i), 0)),
            pl.BlockSpec((tm, CONV_WIDTH), lambda b, i: (row(b, i), 0)),
            pl.BlockSpec((tm, D_MODEL), lambda b, i: (row(b, i), 0)),
            const_spec((D_MODEL, D_MODEL)),
            const_spec(dmat.shape), const_spec(qdec.shape),
            const_spec(kdec.shape), const_spec(cdec.shape), const_spec(nw_next.shape),
        ],
        out_specs=[
            pl.BlockSpec((tm, D_MODEL), lambda b, i: (row(b, i), 0)),
            pl.BlockSpec((tm, D_MODEL), lambda b, i: (row(b, i), 0)),
        ],
        out_shape=[
            jax.ShapeDtypeStruct((t, D_MODEL), F32),
            jax.ShapeDtypeStruct((t, D_MODEL), BF16),
        ],
        scratch_shapes=[
            pltpu.VMEM((RET_HEADS, RET_HEAD_DIM, RET_HEAD_DIM), F32),
            pltpu.VMEM((tm, D_MODEL), BF16),
        ],
        compiler_params=pltpu.CompilerParams(
            dimension_semantics=("arbitrary", "arbitrary"),
            vmem_limit_bytes=VMEM_LIMIT),
        name="even_out",
    )(qkvg, conv, x2d, w_out, dmat, qdec, kdec, cdec, nw_next)


def _odd_in_kernel(h_ref, w_ref,
                   cos_ref, sin_ref, qgain_ref, kgain_ref, ones_ref,
                   qg_ref, kv_ref):
    q_blocks = ATTN_WIDTH // SUB_N
    kv_block = q_blocks
    gate_block0 = kv_block + 1

    def sub_dot(block):
        return _dot(h_ref[...], w_ref[:, block * SUB_N:(block + 1) * SUB_N])

    def dup_heads(x):
        lane = lax.broadcasted_iota(jnp.int32, (x.shape[0], LANES), 1)
        low = lane < ATTN_HEAD_DIM
        blocks = []
        for b in range(x.shape[1] // LANES):
            xb = x[:, b * LANES:(b + 1) * LANES]
            swapped = pltpu.roll(xb, ATTN_HEAD_DIM, axis=1)
            blocks += [jnp.where(low, xb, swapped), jnp.where(low, swapped, xb)]
        return jnp.concatenate(blocks, axis=1)

    def norm_rope(acc, gain_ref):
        width = acc.shape[1]
        sq = (acc * acc).astype(BF16)
        ss = jnp.concatenate(
            [_dot(sq[:, c * MXU_N:(c + 1) * MXU_N], ones_ref[...]) for c in range(width // MXU_N)],
            axis=1)
        r = lax.rsqrt(ss * (1.0 / ATTN_HEAD_DIM) + EPS)
        half = ATTN_HEAD_DIM // 2
        lane = lax.broadcasted_iota(jnp.int32, acc.shape, 1)
        first_half = (lane % ATTN_HEAD_DIM) < half
        partner = jnp.where(first_half,
                            pltpu.roll(acc, width - half, axis=1),
                            pltpu.roll(acc, half, axis=1))
        reps = width // LANES
        ta = cos_ref[...] * gain_ref[0:1, :]
        tb = sin_ref[...] * gain_ref[1:2, :]
        ta_w = jnp.concatenate([ta] * reps, axis=1)
        tb_w = jnp.concatenate([tb] * reps, axis=1)
        return r * (acc * ta_w + partner * tb_w)

    def kv_part():
        acc = sub_dot(kv_block)
        k = norm_rope(acc[:, :KV_WIDTH], kgain_ref)
        kv_ref[:, :2 * KV_WIDTH] = dup_heads(k).astype(BF16)
        kv_ref[:, 2 * KV_WIDTH:] = dup_heads(acc[:, KV_WIDTH:]).astype(BF16)

    kv_part()
    for s in range(q_blocks):
        qg_ref[:, _sub_cols(s)] = norm_rope(sub_dot(s), qgain_ref).astype(BF16)
    for s in range(q_blocks):
        qg_ref[:, _sub_cols(q_blocks + s)] = _silu(sub_dot(gate_block0 + s)).astype(BF16)


def _odd_in(h2d, w, cos_l, sin_l, q_gain, k_gain, ones_bd, seq):
    t = h2d.shape[0]
    tm = IN_TM
    seq_tiles = seq // tm
    assert 2 * KV_WIDTH == SUB_N

    def tab_spec():
        return pl.BlockSpec((tm, LANES), lambda i: (i % seq_tiles, 0))

    def resident_spec(shape, col):
        return pl.BlockSpec(shape, lambda i: (0, col), pipeline_mode=pl.Buffered(1))

    return pl.pallas_call(
        _odd_in_kernel,
        grid=(t // tm,),
        in_specs=[
            pl.BlockSpec((tm, D_MODEL), lambda i: (i, 0)),
            resident_spec(w.shape, 0),
            tab_spec(), tab_spec(),
            resident_spec(q_gain.shape, 0), resident_spec(k_gain.shape, 0),
            resident_spec((MXU_N, MXU_N), 0),
        ],
        out_specs=[
            pl.BlockSpec((tm, 2 * ATTN_WIDTH), lambda i: (i, 0)),
            pl.BlockSpec((tm, 4 * KV_WIDTH), lambda i: (i, 0)),
        ],
        out_shape=[
            jax.ShapeDtypeStruct((t, 2 * ATTN_WIDTH), BF16),
            jax.ShapeDtypeStruct((t, 4 * KV_WIDTH), BF16),
        ],
        compiler_params=pltpu.CompilerParams(
            dimension_semantics=("arbitrary",),
            vmem_limit_bytes=VMEM_LIMIT),
        name="odd_in",
    )(h2d, w, cos_l, sin_l, q_gain, k_gain, ones_bd)


def _odd_out_kernel(sink_ref, qg_ref, kv_ref, kvp_ref, x_ref, w_ref,
                    o_ref, y_scr):
    tm = x_ref.shape[0]
    q_ref = qg_ref.at[:, :ATTN_WIDTH]
    g_ref = qg_ref.at[:, ATTN_WIDTH:]
    at_seq_start = pl.program_id(1) == 0
    pairs = ATTN_GROUP // 2
    half_w = 2 * KV_WIDTH

    lane_kv = lax.broadcasted_iota(jnp.int32, (2 * BLOCK, LANES), 1)
    low_kv = lane_kv < ATTN_HEAD_DIM
    rr = lax.broadcasted_iota(jnp.int32, (BLOCK, BLOCK), 0)
    cc = lax.broadcasted_iota(jnp.int32, (BLOCK, BLOCK), 1)
    tri = cc <= rr
    low_o = cc < ATTN_HEAD_DIM

    for bl in range(tm // BLOCK):
        rows = slice(bl * BLOCK, (bl + 1) * BLOCK)
        for c in range(ATTN_KV_HEADS):
            kcols = slice(c * LANES, (c + 1) * LANES)
            vcols = slice(half_w + c * LANES, half_w + (c + 1) * LANES)
            if bl == 0:
                k_prev = kvp_ref[:, kcols]
                v_prev = kvp_ref[:, vcols]
            else:
                prows = slice((bl - 1) * BLOCK, bl * BLOCK)
                k_prev = kv_ref[prows, kcols]
                v_prev = kv_ref[prows, vcols]
            kk = jnp.concatenate([k_prev, kv_ref[rows, kcols]], axis=0)
            vv = jnp.concatenate([v_prev, kv_ref[rows, vcols]], axis=0)
            zero = jnp.zeros_like(kk)
            k_par = (jnp.where(low_kv, kk, zero), jnp.where(low_kv, zero, kk))
            qs = jnp.concatenate(
                [q_ref[rows, (pairs * c + a) * LANES:(pairs * c + a + 1) * LANES]
                 for a in range(pairs)], axis=0)
            probs = []
            denoms = []
            for par in range(2):
                s = _dot_nt(qs, k_par[par])
                for a in range(pairs):
                    sa = s[a * BLOCK:(a + 1) * BLOCK]
                    s_prev = sa[:, :BLOCK]
                    if bl == 0:
                        s_prev = jnp.where(at_seq_start, NEG, s_prev)
                    f = jnp.where(tri, sa[:, BLOCK:], s_prev)
                    sink = sink_ref[ATTN_GROUP * c + 2 * a + par]
                    m = jnp.maximum(jnp.max(f, axis=1, keepdims=True), sink)
                    p = jnp.exp(f - m)
                    denoms.append(jnp.sum(p, axis=1, keepdims=True) + jnp.exp(sink - m))
                    probs.append(jnp.concatenate(
                        [jnp.where(tri, 0.0, p), jnp.where(tri, p, 0.0)], axis=1).astype(BF16))
            o = _dot(jnp.concatenate(probs, axis=0), vv)
            for a in range(pairs):
                o_even = o[a * BLOCK:(a + 1) * BLOCK] / denoms[a]
                o_odd = o[(pairs + a) * BLOCK:(pairs + a + 1) * BLOCK] / denoms[pairs + a]
                ocols = slice((pairs * c + a) * LANES, (pairs * c + a + 1) * LANES)
                gated = jnp.where(low_o, o_even, o_odd) * g_ref[rows, ocols].astype(F32)
                y_scr[rows, ocols] = gated.astype(BF16)

    for n in range(D_MODEL // SUB_N):
        o_ref[:, _sub_cols(n)] = x_ref[:, _sub_cols(n)] + _dot(y_scr[...], w_ref[:, _sub_cols(n)])


def _odd_out(sinks, qg, kv, x2d, w_out, batch, seq):
    t = x2d.shape[0]
    tm = OUT_TM
    ns = seq // tm
    bpt = tm // BLOCK

    def row(b, i):
        return b * ns + i

    return pl.pallas_call(
        _odd_out_kernel,
        grid=(batch, ns),
        in_specs=[
            pl.BlockSpec(memory_space=pltpu.SMEM),
            pl.BlockSpec((tm, 2 * ATTN_WIDTH), lambda b, i: (row(b, i), 0)),
            pl.BlockSpec((tm, 4 * KV_WIDTH), lambda b, i: (row(b, i), 0)),
            pl.BlockSpec((BLOCK, 4 * KV_WIDTH),
                         lambda b, i: (jnp.maximum(row(b, i) * bpt - 1, 0), 0)),
            pl.BlockSpec((tm, D_MODEL), lambda b, i: (row(b, i), 0)),
            pl.BlockSpec((ATTN_WIDTH, D_MODEL), lambda b, i: (0, 0), pipeline_mode=pl.Buffered(1)),
        ],
        out_specs=pl.BlockSpec((tm, D_MODEL), lambda b, i: (row(b, i), 0)),
        out_shape=jax.ShapeDtypeStruct((t, D_MODEL), F32),
        scratch_shapes=[pltpu.VMEM((tm, ATTN_WIDTH), BF16)],
        compiler_params=pltpu.CompilerParams(
            dimension_semantics=("arbitrary", "arbitrary"),
            vmem_limit_bytes=VMEM_LIMIT),
        name="odd_out",
    )(sinks, qg, kv, kv, x2d, w_out)


def _rope_cos_sin(seq, dim):
    inv = 1.0 / (ROPE_THETA ** (np.arange(0, dim, 2, dtype=np.float64) / dim))
    ang = np.arange(seq, dtype=np.float64)[:, None] * inv[None, :]
    return np.cos(ang), np.sin(ang)


def _retention_tables():
    h = np.arange(RET_HEADS, dtype=np.float64)
    log_g = np.log(1.0 - 2.0 ** (-5.0 - h))
    idx = np.arange(RET_CHUNK, dtype=np.float64)
    diff = idx[:, None] - idx[None, :]
    scale = RET_HEAD_DIM ** -0.5
    intra = np.where(diff >= 0, np.exp(log_g[:, None, None] * np.maximum(diff, 0.0)), 0.0)
    q_dec = np.exp(log_g[:, None] * (idx[None, :] + 1.0))
    k_dec = np.exp(log_g[:, None] * (RET_CHUNK - 1.0 - idx[None, :]))
    c_dec = np.exp(log_g * RET_CHUNK)
    wide = (RET_HEADS, RET_CHUNK, RET_HEAD_DIM)
    dmat = jnp.asarray(intra * scale, F32)
    qdec = jnp.asarray(np.broadcast_to(q_dec[:, :, None], wide), F32)
    kdec = jnp.asarray(np.broadcast_to(k_dec[:, :, None] * scale, wide), F32)
    cdec = jnp.asarray(np.broadcast_to(c_dec[:, None, None], (RET_HEADS, 1, RET_HEAD_DIM)), F32)
    return dmat, qdec, kdec, cdec


def _head_rope_tables(seq):
    reps = LANES // ATTN_HEAD_DIM
    cos, sin = _rope_cos_sin(seq, ATTN_HEAD_DIM)
    cos_l = jnp.asarray(np.tile(np.concatenate([cos, cos], axis=1), (1, reps)), F32)
    sin_l = jnp.asarray(np.tile(np.concatenate([-sin, sin], axis=1), (1, reps)), F32)
    return cos_l, sin_l


def _head_gain_rows(norm_w, scale):
    half = ATTN_HEAD_DIM // 2
    reps = LANES // ATTN_HEAD_DIM
    w = norm_w.astype(F32) * scale
    w_partner = jnp.concatenate([w[half:], w[:half]])
    rows = jnp.stack([jnp.tile(w, reps), jnp.tile(w_partner, reps)])
    return jnp.pad(rows, ((0, SUBLANES - rows.shape[0]), (0, 0)))


def kernel(x, ev_norm_w, ev_w_in, ev_conv_w, ev_w_out, od_norm_w, od_w_in,
           od_q_norm_w, od_k_norm_w, od_sinks, od_w_out):
    batch, seq, d = x.shape
    assert d == D_MODEL and seq % IN_TM == 0 and seq % OUT_TM == 0
    x2d = x.reshape(batch * seq, d)

    cos, sin = _rope_cos_sin(seq, RET_HEAD_DIM)
    cw = jnp.pad(ev_conv_w[0], ((0, SUBLANES - CONV_K), (0, 0)))
    qkvg, conv, w_out_e, w_in_o, w_out_o = _even_in(
        x2d, (ev_w_in[0] * ev_norm_w[0][:, None]).astype(BF16),
        jnp.asarray(cos, F32), jnp.asarray(sin, F32), cw,
        (ev_w_out[0], od_w_in[0], od_w_out[0]), seq)
    dmat, qdec, kdec, cdec = _retention_tables()
    x2d, h2d = _even_out(qkvg, conv, x2d, w_out_e, dmat, qdec, kdec, cdec,
                         od_norm_w[0][None, :], batch, seq)

    cos_l, sin_l = _head_rope_tables(seq)
    q_gain = _head_gain_rows(od_q_norm_w[0], ATTN_HEAD_DIM ** -0.5)
    k_gain = _head_gain_rows(od_k_norm_w[0], 1.0)
    head_id = np.arange(MXU_N) // ATTN_HEAD_DIM
    ones_bd = jnp.asarray(head_id[:, None] == head_id[None, :], BF16)
    qg, kv = _odd_in(h2d, w_in_o, cos_l, sin_l, q_gain, k_gain, ones_bd, seq)
    x2d = _odd_out(od_sinks[0].astype(F32), qg, kv, x2d, w_out_o, batch, seq)
    return x2d.reshape(batch, seq, d)
```

```python
import functools

import numpy as np
import jax
import jax.numpy as jnp
from jax import lax
from jax.experimental import pallas as pl
from jax.experimental.pallas import tpu as pltpu

D_MODEL = 2048
RET_HEADS = 4
RET_HEAD_DIM = 256
RET_WIDTH = RET_HEADS * RET_HEAD_DIM
CONV_WIDTH = D_MODEL - RET_WIDTH
CONV_K = 3
RET_CHUNK = 256
ATTN_HEAD_DIM = 64
ATTN_Q_HEADS = 32
ATTN_KV_HEADS = 4
ATTN_GROUP = ATTN_Q_HEADS // ATTN_KV_HEADS
ATTN_WIDTH = ATTN_Q_HEADS * ATTN_HEAD_DIM
KV_WIDTH = ATTN_KV_HEADS * ATTN_HEAD_DIM
BLOCK = 128
ROPE_THETA = 10000.0
EPS = 1e-6
NEG = -1e30

LANES = 128
SUBLANES = 8
BF16_ROWS = 16
MXU_N = 256
SUB_N = 512
N_SUB = 4
IN_TM = 1024
IN_TN = N_SUB * SUB_N
OUT_TM = 512
HALO = SUBLANES
V7X_VMEM_BYTES = 64 * 1024 * 1024
VMEM_LIMIT = V7X_VMEM_BYTES * 7 // 8

F32 = jnp.float32
BF16 = jnp.bfloat16


def _dot(a, b):
    return jnp.dot(a, b, preferred_element_type=F32)


def _dot_nt(a, b):
    return lax.dot_general(a, b, (((1,), (1,)), ((), ())), preferred_element_type=F32)


def _dot_tn(a, b):
    return lax.dot_general(a, b, (((0,), (0,)), ((), ())), preferred_element_type=F32)


def _silu(g):
    return (0.5 * g) * (1.0 + jnp.tanh(0.5 * g))


def _rms_rows(x, nw):
    ms = jnp.mean(x * x, axis=-1, keepdims=True)
    return x * lax.rsqrt(ms + EPS) * nw


def _sub_cols(s):
    return slice(s * SUB_N, (s + 1) * SUB_N)


def _even_in_kernel(x_ref, nw_ref, w0_ref, w1_ref, w2_ref, w3_ref,
                    cos_ref, sin_ref, cw_ref, c0_ref, c1_ref, c2_ref,
                    qkvg_ref, conv_ref, c0_out, c1_out, c2_out,
                    h_cur, h_next, cu_scr, tail_scr, *, seq_tiles):
    i = pl.program_id(0)
    j = pl.program_id(1)
    tm = h_cur.shape[0]
    chunk = x_ref.shape[0]
    w_refs = (w0_ref, w1_ref, w2_ref, w3_ref)

    for src, dst in ((c0_ref, c0_out), (c1_ref, c1_out), (c2_ref, c2_out)):
        dst[...] = src[...].astype(BF16)

    def norm_next(step=None):
        if step is None:
            rows = pl.ds(pl.multiple_of(j * chunk, chunk), chunk)
        else:
            rows = slice(step * chunk, (step + 1) * chunk)
        h_next[rows, :] = _rms_rows(x_ref[...], nw_ref[...]).astype(BF16)

    def sub_dot(s):
        return _dot(h_cur[...], w_refs[s][...])

    @pl.when(i == 0)
    def _():
        norm_next()
        tail_scr[...] = jnp.zeros_like(tail_scr)

    projecting = i >= 1

    def rope_then(tail_fn, step):
        if step == 0:
            for kb in range(D_MODEL // MXU_N):
                h_cur[:, kb * MXU_N:(kb + 1) * MXU_N] = h_next[:, kb * MXU_N:(kb + 1) * MXU_N]
        norm_next(step)
        cos = cos_ref[...]
        sin = sin_ref[...]
        for s in range(N_SUB // 2):
            acc = sub_dot(s)
            for hd in range(SUB_N // RET_HEAD_DIM):
                base = s * SUB_N + hd * RET_HEAD_DIM
                x1 = acc[:, hd * RET_HEAD_DIM:hd * RET_HEAD_DIM + LANES]
                x2 = acc[:, hd * RET_HEAD_DIM + LANES:(hd + 1) * RET_HEAD_DIM]
                qkvg_ref[:, base:base + LANES] = (x1 * cos - x2 * sin).astype(BF16)
                qkvg_ref[:, base + LANES:base + RET_HEAD_DIM] = (x2 * cos + x1 * sin).astype(BF16)
        for s in range(N_SUB // 2, N_SUB):
            qkvg_ref[:, _sub_cols(s)] = tail_fn(sub_dot(s)).astype(BF16)

    @pl.when(jnp.logical_and(projecting, j == 0))
    def _():
        rope_then(lambda v: v, 0)

    @pl.when(jnp.logical_and(projecting, j == 1))
    def _():
        rope_then(_silu, 1)

    @pl.when(jnp.logical_and(projecting, j >= 2))
    def _():
        cs = j - 2
        norm_next()
        cu = sub_dot(0) * sub_dot(1)
        at_seq_start = ((i - 1) % seq_tiles) == 0
        halo_rows = slice(HALO - (CONV_K - 1), HALO)
        cu_scr[halo_rows, :] = jnp.where(at_seq_start, 0.0, tail_scr[cs, halo_rows, :])
        cu_scr[HALO:HALO + tm, :] = cu
        tail_scr[cs, halo_rows, :] = cu_scr[HALO + tm - (CONV_K - 1):HALO + tm, :]
        cw = cw_ref[...]
        conv = cw[CONV_K - 1:CONV_K, :] * cu
        for t in range(CONV_K - 1):
            shift = CONV_K - 1 - t
            conv = conv + cw[t:t + 1, :] * cu_scr[HALO - shift:HALO - shift + tm, :]
        gated = sub_dot(2) * conv
        conv_ref[...] = (gated * _silu(sub_dot(3))).astype(BF16)


def _even_in(x2d, nw, w, cos, sin, cw, to_cast, seq):
    t = x2d.shape[0]
    tm = IN_TM
    n_tiles = t // tm
    seq_tiles = seq // tm
    n_ret_steps = 4 * RET_WIDTH // IN_TN
    n_conv_steps = CONV_WIDTH // SUB_N
    nj = n_ret_steps + n_conv_steps
    chunk = tm // nj
    n_proj_steps = n_tiles * nj
    conv_first = tuple((4 * RET_WIDTH + b * CONV_WIDTH) // SUB_N for b in (1, 2, 0, 3))

    def tile(i):
        return jnp.maximum(i - 1, 0)

    def step(i, j):
        return jnp.where(i == 0, 0, j)

    def w_spec(s):
        def index(i, j):
            jj = step(i, j)
            half = N_SUB // 2
            ret_idx = half * jnp.minimum(jj, n_ret_steps - 1) + (s % half) + (s // half) * N_SUB
            conv_idx = conv_first[s] + jj - n_ret_steps
            return 0, jnp.where(jj < n_ret_steps, ret_idx, conv_idx)
        return pl.BlockSpec((D_MODEL, SUB_N), index)

    def cast_spec(a):
        rows = a.shape[0] // n_proj_steps
        assert rows * n_proj_steps == a.shape[0] and rows % BF16_ROWS == 0
        return pl.BlockSpec((rows, a.shape[1]), lambda i, j: (tile(i) * nj + step(i, j), 0))

    def table_spec():
        return pl.BlockSpec((tm, LANES), lambda i, j: (tile(i) % seq_tiles, 0))

    kern = functools.partial(_even_in_kernel, seq_tiles=seq_tiles)
    return pl.pallas_call(
        kern,
        grid=(n_tiles + 1, nj),
        in_specs=[
            pl.BlockSpec((chunk, D_MODEL), lambda i, j: (jnp.minimum(i, n_tiles - 1) * nj + j, 0)),
            pl.BlockSpec((1, D_MODEL), lambda i, j: (0, 0), pipeline_mode=pl.Buffered(1)),
            w_spec(0), w_spec(1), w_spec(2), w_spec(3),
            table_spec(), table_spec(),
            pl.BlockSpec((SUBLANES, SUB_N), lambda i, j: (0, jnp.maximum(step(i, j) - n_ret_steps, 0))),
        ] + [cast_spec(a) for a in to_cast],
        out_specs=[
            pl.BlockSpec((tm, IN_TN),
                         lambda i, j: (tile(i), jnp.minimum(step(i, j), n_ret_steps - 1))),
            pl.BlockSpec((tm, SUB_N),
                         lambda i, j: (tile(i), jnp.maximum(step(i, j) - n_ret_steps, 0))),
        ] + [cast_spec(a) for a in to_cast],
        out_shape=[
            jax.ShapeDtypeStruct((t, 4 * RET_WIDTH), BF16),
            jax.ShapeDtypeStruct((t, CONV_WIDTH), BF16),
        ] + [jax.ShapeDtypeStruct(a.shape, BF16) for a in to_cast],
        scratch_shapes=[
            pltpu.VMEM((tm, D_MODEL), BF16),
            pltpu.VMEM((tm, D_MODEL), BF16),
            pltpu.VMEM((HALO + tm, SUB_N), F32),
            pltpu.VMEM((n_conv_steps, HALO, SUB_N), F32),
        ],
        compiler_params=pltpu.CompilerParams(
            dimension_semantics=("arbitrary", "arbitrary"),
            vmem_limit_bytes=VMEM_LIMIT),
        name="even_in",
    )(x2d, nw, w, w, w, w, cos, sin, cw, *to_cast)


def _even_out_kernel(qkvg_ref, conv_ref, x_ref, w_ref,
                     dmat_ref, qdec_ref, kdec_ref, cdec_ref, nw_next_ref,
                     o_ref, h_next_ref, state_scr, y_scr):
    tm = x_ref.shape[0]
    q_ref, v_ref, k_ref, g_ref = (
        qkvg_ref.at[:, n * RET_WIDTH:(n + 1) * RET_WIDTH] for n in range(4))

    @pl.when(pl.program_id(1) == 0)
    def _():
        state_scr[...] = jnp.zeros_like(state_scr)

    for c in range(tm // RET_CHUNK):
        rows = slice(c * RET_CHUNK, (c + 1) * RET_CHUNK)
        for h in range(RET_HEADS):
            cols = slice(h * RET_HEAD_DIM, (h + 1) * RET_HEAD_DIM)
            qh = q_ref[rows, cols]
            kh = k_ref[rows, cols]
            vh = v_ref[rows, cols]
            scores = _dot_nt(qh, kh) * dmat_ref[h]
            inner = _dot(scores.astype(BF16), vh)
            state = state_scr[h]
            cross = _dot(qh, state.astype(BF16)) * qdec_ref[h]
            o = inner + cross
            kd = (kh.astype(F32) * kdec_ref[h]).astype(BF16)
            state_scr[h] = cdec_ref[h] * state + _dot_tn(kd, vh)
            ms = jnp.mean(o * o, axis=-1, keepdims=True)
            ro = o * lax.rsqrt(ms + EPS) * g_ref[rows, cols].astype(F32)
            y_scr[rows, cols] = ro.astype(BF16)
    y_scr[:, RET_WIDTH:] = conv_ref[...]

    ssq = jnp.zeros((tm, 1), F32)
    for n in range(D_MODEL // SUB_N):
        xn = x_ref[:, _sub_cols(n)] + _dot(y_scr[...], w_ref[:, _sub_cols(n)])
        o_ref[:, _sub_cols(n)] = xn
        ssq = ssq + jnp.sum(xn * xn, axis=-1, keepdims=True)
    r = lax.rsqrt(ssq * (1.0 / D_MODEL) + EPS)
    for n in range(D_MODEL // SUB_N):
        h_next_ref[:, _sub_cols(n)] = (o_ref[:, _sub_cols(n)] * r * nw_next_ref[:, _sub_cols(n)]).astype(BF16)


def _even_out(qkvg, conv, x2d, w_out, dmat, qdec, kdec, cdec, nw_next, batch, seq):
    t = x2d.shape[0]
    tm = OUT_TM
    ns = seq // tm

    def row(b, i):
        return b * ns + i

    def const_spec(shape):
        return pl.BlockSpec(shape, lambda b, i: (0,) * len(shape), pipeline_mode=pl.Buffered(1))

    return pl.pallas_call(
        _even_out_kernel,
        grid=(batch, ns),
        in_specs=[
            pl.BlockSpec((tm, 4 * RET_WIDTH), lambda b, i: (row(b, i), 0)),
            pl.BlockSpec((tm, CONV_WIDTH), lambda b, i: (row(b, i), 0)),
            pl.BlockSpec((tm, D_MODEL), lambda b, i: (row(b, i), 0)),
            const_spec((D_MODEL, D_MODEL)),
            const_spec(dmat.shape), const_spec(qdec.shape),
            const_spec(kdec.shape), const_spec(cdec.shape), const_spec(nw_next.shape),
        ],
        out_specs=[
            pl.BlockSpec((tm, D_MODEL), lambda b, i: (row(b, i), 0)),
            pl.BlockSpec((tm, D_MODEL), lambda b, i: (row(b, i), 0)),
        ],
        out_shape=[
            jax.ShapeDtypeStruct((t, D_MODEL), F32),
            jax.ShapeDtypeStruct((t, D_MODEL), BF16),
        ],
        scratch_shapes=[
            pltpu.VMEM((RET_HEADS, RET_HEAD_DIM, RET_HEAD_DIM), F32),
            pltpu.VMEM((tm, D_MODEL), BF16),
        ],
        compiler_params=pltpu.CompilerParams(
            dimension_semantics=("arbitrary", "arbitrary"),
            vmem_limit_bytes=VMEM_LIMIT),
        name="even_out",
    )(qkvg, conv, x2d, w_out, dmat, qdec, kdec, cdec, nw_next)


def _odd_in_kernel(h_ref, w_ref,
                   cos_ref, sin_ref, qgain_ref, kgain_ref, ones_ref,
                   qg_ref, kv_ref):
    q_blocks = ATTN_WIDTH // SUB_N
    kv_block = q_blocks
    gate_block0 = kv_block + 1

    def sub_dot(block):
        return _dot(h_ref[...], w_ref[:, block * SUB_N:(block + 1) * SUB_N])

    def dup_heads(x):
        lane = lax.broadcasted_iota(jnp.int32, (x.shape[0], LANES), 1)
        low = lane < ATTN_HEAD_DIM
        blocks = []
        for b in range(x.shape[1] // LANES):
            xb = x[:, b * LANES:(b + 1) * LANES]
            swapped = pltpu.roll(xb, ATTN_HEAD_DIM, axis=1)
            blocks += [jnp.where(low, xb, swapped), jnp.where(low, swapped, xb)]
        return jnp.concatenate(blocks, axis=1)

    def norm_rope(acc, gain_ref):
        width = acc.shape[1]
        sq = (acc * acc).astype(BF16)
        ss = jnp.concatenate(
            [_dot(sq[:, c * MXU_N:(c + 1) * MXU_N], ones_ref[...]) for c in range(width // MXU_N)],
            axis=1)
        r = lax.rsqrt(ss * (1.0 / ATTN_HEAD_DIM) + EPS)
        half = ATTN_HEAD_DIM // 2
        lane = lax.broadcasted_iota(jnp.int32, acc.shape, 1)
        first_half = (lane % ATTN_HEAD_DIM) < half
        partner = jnp.where(first_half,
                            pltpu.roll(acc, width - half, axis=1),
                            pltpu.roll(acc, half, axis=1))
        reps = width // LANES
        ta = cos_ref[...] * gain_ref[0:1, :]
        tb = sin_ref[...] * gain_ref[1:2, :]
        ta_w = jnp.concatenate([ta] * reps, axis=1)
        tb_w = jnp.concatenate([tb] * reps, axis=1)
        return r * (acc * ta_w + partner * tb_w)

    def kv_part():
        acc = sub_dot(kv_block)
        k = norm_rope(acc[:, :KV_WIDTH], kgain_ref)
        kv_ref[:, :2 * KV_WIDTH] = dup_heads(k).astype(BF16)
        kv_ref[:, 2 * KV_WIDTH:] = dup_heads(acc[:, KV_WIDTH:]).astype(BF16)

    kv_part()
    for s in range(q_blocks):
        qg_ref[:, _sub_cols(s)] = norm_rope(sub_dot(s), qgain_ref).astype(BF16)
    for s in range(q_blocks):
        qg_ref[:, _sub_cols(q_blocks + s)] = _silu(sub_dot(gate_block0 + s)).astype(BF16)


def _odd_in(h2d, w, cos_l, sin_l, q_gain, k_gain, ones_bd, seq):
    t = h2d.shape[0]
    tm = IN_TM
    seq_tiles = seq // tm
    assert 2 * KV_WIDTH == SUB_N

    def tab_spec():
        return pl.BlockSpec((tm, LANES), lambda i: (i % seq_tiles, 0))

    def outer(h_hbm, w_ref, cos_hbm, sin_hbm, qgain_ref, kgain_ref, ones_ref, qg_hbm, kv_hbm):
        def tile_body(h_ref, cos_ref, sin_ref, qg_ref, kv_ref):
            _odd_in_kernel(h_ref, w_ref, cos_ref, sin_ref, qgain_ref, kgain_ref, ones_ref,
                           qg_ref, kv_ref)

        pltpu.emit_pipeline(
            tile_body,
            grid=(t // tm,),
            in_specs=[pl.BlockSpec((tm, D_MODEL), lambda i: (i, 0)), tab_spec(), tab_spec()],
            out_specs=[
                pl.BlockSpec((tm, 2 * ATTN_WIDTH), lambda i: (i, 0)),
                pl.BlockSpec((tm, 4 * KV_WIDTH), lambda i: (i, 0)),
            ],
            trace_scopes=False,
        )(h_hbm, cos_hbm, sin_hbm, qg_hbm, kv_hbm)

    streamed = pl.BlockSpec(memory_space=pl.ANY)
    resident = pl.BlockSpec(memory_space=pltpu.VMEM)
    return pl.pallas_call(
        outer,
        in_specs=[streamed, resident, streamed, streamed, resident, resident, resident],
        out_specs=[streamed, streamed],
        out_shape=[
            jax.ShapeDtypeStruct((t, 2 * ATTN_WIDTH), BF16),
            jax.ShapeDtypeStruct((t, 4 * KV_WIDTH), BF16),
        ],
        compiler_params=pltpu.CompilerParams(vmem_limit_bytes=VMEM_LIMIT),
        name="odd_in",
    )(h2d, w, cos_l, sin_l, q_gain, k_gain, ones_bd)


def _odd_out_kernel(sink_ref, qg_ref, kv_ref, kvp_ref, x_ref, w_ref,
                    o_ref, y_scr):
    tm = x_ref.shape[0]
    q_ref = qg_ref.at[:, :ATTN_WIDTH]
    g_ref = qg_ref.at[:, ATTN_WIDTH:]
    at_seq_start = pl.program_id(1) == 0
    pairs = ATTN_GROUP // 2
    half_w = 2 * KV_WIDTH

    lane_kv = lax.broadcasted_iota(jnp.int32, (2 * BLOCK, LANES), 1)
    low_kv = lane_kv < ATTN_HEAD_DIM
    rr = lax.broadcasted_iota(jnp.int32, (BLOCK, BLOCK), 0)
    cc = lax.broadcasted_iota(jnp.int32, (BLOCK, BLOCK), 1)
    tri = cc <= rr
    low_o = cc < ATTN_HEAD_DIM

    for bl in range(tm // BLOCK):
        rows = slice(bl * BLOCK, (bl + 1) * BLOCK)
        for c in range(ATTN_KV_HEADS):
            kcols = slice(c * LANES, (c + 1) * LANES)
            vcols = slice(half_w + c * LANES, half_w + (c + 1) * LANES)
            if bl == 0:
                k_prev = kvp_ref[:, kcols]
                v_prev = kvp_ref[:, vcols]
            else:
                prows = slice((bl - 1) * BLOCK, bl * BLOCK)
                k_prev = kv_ref[prows, kcols]
                v_prev = kv_ref[prows, vcols]
            kk = jnp.concatenate([k_prev, kv_ref[rows, kcols]], axis=0)
            vv = jnp.concatenate([v_prev, kv_ref[rows, vcols]], axis=0)
            zero = jnp.zeros_like(kk)
            k_par = (jnp.where(low_kv, kk, zero), jnp.where(low_kv, zero, kk))
            qs = jnp.concatenate(
                [q_ref[rows, (pairs * c + a) * LANES:(pairs * c + a + 1) * LANES]
                 for a in range(pairs)], axis=0)
            probs = []
            denoms = []
            for par in range(2):
                s = _dot_nt(qs, k_par[par])
                for a in range(pairs):
                    sa = s[a * BLOCK:(a + 1) * BLOCK]
                    s_prev = sa[:, :BLOCK]
                    if bl == 0:
                        s_prev = jnp.where(at_seq_start, NEG, s_prev)
                    f = jnp.where(tri, sa[:, BLOCK:], s_prev)
                    sink = sink_ref[ATTN_GROUP * c + 2 * a + par]
                    m = jnp.maximum(jnp.max(f, axis=1, keepdims=True), sink)
                    p = jnp.exp(f - m)
                    denoms.append(jnp.sum(p, axis=1, keepdims=True) + jnp.exp(sink - m))
                    probs.append(jnp.concatenate(
                        [jnp.where(tri, 0.0, p), jnp.where(tri, p, 0.0)], axis=1).astype(BF16))
            o = _dot(jnp.concatenate(probs, axis=0), vv)
            for a in range(pairs):
                o_even = o[a * BLOCK:(a + 1) * BLOCK] / denoms[a]
                o_odd = o[(pairs + a) * BLOCK:(pairs + a + 1) * BLOCK] / denoms[pairs + a]
                ocols = slice((pairs * c + a) * LANES, (pairs * c + a + 1) * LANES)
                gated = jnp.where(low_o, o_even, o_odd) * g_ref[rows, ocols].astype(F32)
                y_scr[rows, ocols] = gated.astype(BF16)

    for n in range(D_MODEL // SUB_N):
        o_ref[:, _sub_cols(n)] = x_ref[:, _sub_cols(n)] + _dot(y_scr[...], w_ref[:, _sub_cols(n)])


def _odd_out(sinks, qg, kv, x2d, w_out, batch, seq):
    t = x2d.shape[0]
    tm = OUT_TM
    ns = seq // tm
    bpt = tm // BLOCK

    def row(b, i):
        return b * ns + i

    return pl.pallas_call(
        _odd_out_kernel,
        grid=(batch, ns),
        in_specs=[
            pl.BlockSpec(memory_space=pltpu.SMEM),
            pl.BlockSpec((tm, 2 * ATTN_WIDTH), lambda b, i: (row(b, i), 0)),
            pl.BlockSpec((tm, 4 * KV_WIDTH), lambda b, i: (row(b, i), 0)),
            pl.BlockSpec((BLOCK, 4 * KV_WIDTH),
                         lambda b, i: (jnp.maximum(row(b, i) * bpt - 1, 0), 0)),
            pl.BlockSpec((tm, D_MODEL), lambda b, i: (row(b, i), 0)),
            pl.BlockSpec((ATTN_WIDTH, D_MODEL), lambda b, i: (0, 0), pipeline_mode=pl.Buffered(1)),
        ],
        out_specs=pl.BlockSpec((tm, D_MODEL), lambda b, i: (row(b, i), 0)),
        out_shape=jax.ShapeDtypeStruct((t, D_MODEL), F32),
        scratch_shapes=[pltpu.VMEM((tm, ATTN_WIDTH), BF16)],
        compiler_params=pltpu.CompilerParams(
            dimension_semantics=("arbitrary", "arbitrary"),
            vmem_limit_bytes=VMEM_LIMIT),
        name="odd_out",
    )(sinks, qg, kv, kv, x2d, w_out)


def _rope_cos_sin(seq, dim):
    inv = 1.0 / (ROPE_THETA ** (np.arange(0, dim, 2, dtype=np.float64) / dim))
    ang = np.arange(seq, dtype=np.float64)[:, None] * inv[None, :]
    return np.cos(ang), np.sin(ang)


def _retention_tables():
    h = np.arange(RET_HEADS, dtype=np.float64)
    log_g = np.log(1.0 - 2.0 ** (-5.0 - h))
    idx = np.arange(RET_CHUNK, dtype=np.float64)
    diff = idx[:, None] - idx[None, :]
    scale = RET_HEAD_DIM ** -0.5
    intra = np.where(diff >= 0, np.exp(log_g[:, None, None] * np.maximum(diff, 0.0)), 0.0)
    q_dec = np.exp(log_g[:, None] * (idx[None, :] + 1.0))
    k_dec = np.exp(log_g[:, None] * (RET_CHUNK - 1.0 - idx[None, :]))
    c_dec = np.exp(log_g * RET_CHUNK)
    wide = (RET_HEADS, RET_CHUNK, RET_HEAD_DIM)
    dmat = jnp.asarray(intra * scale, F32)
    qdec = jnp.asarray(np.broadcast_to(q_dec[:, :, None], wide), F32)
    kdec = jnp.asarray(np.broadcast_to(k_dec[:, :, None] * scale, wide), F32)
    cdec = jnp.asarray(np.broadcast_to(c_dec[:, None, None], (RET_HEADS, 1, RET_HEAD_DIM)), F32)
    return dmat, qdec, kdec, cdec


def _head_rope_tables(seq):
    reps = LANES // ATTN_HEAD_DIM
    cos, sin = _rope_cos_sin(seq, ATTN_HEAD_DIM)
    cos_l = jnp.asarray(np.tile(np.concatenate([cos, cos], axis=1), (1, reps)), F32)
    sin_l = jnp.asarray(np.tile(np.concatenate([-sin, sin], axis=1), (1, reps)), F32)
    return cos_l, sin_l


def _head_gain_rows(norm_w, scale):
    half = ATTN_HEAD_DIM // 2
    reps = LANES // ATTN_HEAD_DIM
    w = norm_w.astype(F32) * scale
    w_partner = jnp.concatenate([w[half:], w[:half]])
    rows = jnp.stack([jnp.tile(w, reps), jnp.tile(w_partner, reps)])
    return jnp.pad(rows, ((0, SUBLANES - rows.shape[0]), (0, 0)))


def kernel(x, ev_norm_w, ev_w_in, ev_conv_w, ev_w_out, od_norm_w, od_w_in,
           od_q_norm_w, od_k_norm_w, od_sinks, od_w_out):
    batch, seq, d = x.shape
    assert d == D_MODEL and seq % IN_TM == 0 and seq % OUT_TM == 0
    x2d = x.reshape(batch * seq, d)

    cos, sin = _rope_cos_sin(seq, RET_HEAD_DIM)
    cw = jnp.pad(ev_conv_w[0], ((0, SUBLANES - CONV_K), (0, 0)))
    qkvg, conv, w_out_e, w_in_o, w_out_o = _even_in(
        x2d, ev_norm_w[0][None, :], ev_w_in[0].astype(BF16),
        jnp.asarray(cos, F32), jnp.asarray(sin, F32), cw,
        (ev_w_out[0], od_w_in[0], od_w_out[0]), seq)
    dmat, qdec, kdec, cdec = _retention_tables()
    x2d, h2d = _even_out(qkvg, conv, x2d, w_out_e, dmat, qdec, kdec, cdec,
                         od_norm_w[0][None, :], batch, seq)

    cos_l, sin_l = _head_rope_tables(seq)
    q_gain = _head_gain_rows(od_q_norm_w[0], ATTN_HEAD_DIM ** -0.5)
    k_gain = _head_gain_rows(od_k_norm_w[0], 1.0)
    head_id = np.arange(MXU_N) // ATTN_HEAD_DIM
    ones_bd = jnp.asarray(head_id[:, None] == head_id[None, :], BF16)
    qg, kv = _odd_in(h2d, w_in_o, cos_l, sin_l, q_gain, k_gain, ones_bd, seq)
    x2d = _odd_out(od_sinks[0].astype(F32), qg, kv, x2d, w_out_o, batch, seq)
    return x2d.reshape(batch, seq, d)
```

```python
import functools

import numpy as np
import jax
import jax.numpy as jnp
from jax import lax
from jax.experimental import pallas as pl
from jax.experimental.pallas import tpu as pltpu

D_MODEL = 2048
RET_HEADS = 4
RET_HEAD_DIM = 256
RET_WIDTH = RET_HEADS * RET_HEAD_DIM
CONV_WIDTH = D_MODEL - RET_WIDTH
CONV_K = 3
RET_CHUNK = 256
ATTN_HEAD_DIM = 64
ATTN_Q_HEADS = 32
ATTN_KV_HEADS = 4
ATTN_GROUP = ATTN_Q_HEADS // ATTN_KV_HEADS
ATTN_WIDTH = ATTN_Q_HEADS * ATTN_HEAD_DIM
KV_WIDTH = ATTN_KV_HEADS * ATTN_HEAD_DIM
BLOCK = 128
ROPE_THETA = 10000.0
EPS = 1e-6
NEG = -1e30

LANES = 128
SUBLANES = 8
BF16_ROWS = 16
MXU_N = 256
SUB_N = 512
N_SUB = 4
IN_TM = 1024
IN_TN = N_SUB * SUB_N
OUT_TM = 512
HALO = SUBLANES
V7X_VMEM_BYTES = 64 * 1024 * 1024
VMEM_LIMIT = V7X_VMEM_BYTES * 7 // 8

F32 = jnp.float32
BF16 = jnp.bfloat16


def _dot(a, b):
    return jnp.dot(a, b, preferred_element_type=F32)


def _dot_nt(a, b):
    return lax.dot_general(a, b, (((1,), (1,)), ((), ())), preferred_element_type=F32)


def _dot_tn(a, b):
    return lax.dot_general(a, b, (((0,), (0,)), ((), ())), preferred_element_type=F32)


def _silu(g):
    return (0.5 * g) * (1.0 + jnp.tanh(0.5 * g))


def _rms_rows(x, nw):
    ms = jnp.mean(x * x, axis=-1, keepdims=True)
    return x * lax.rsqrt(ms + EPS) * nw


def _sub_cols(s):
    return slice(s * SUB_N, (s + 1) * SUB_N)


def _even_in_kernel(x_ref, nw_ref, w0_ref, w1_ref, w2_ref, w3_ref,
                    cos_ref, sin_ref, cw_ref, c0_ref, c1_ref, c2_ref,
                    qkvg_ref, conv_ref, c0_out, c1_out, c2_out,
                    h_cur, h_next, cu_scr, tail_scr, *, seq_tiles):
    i = pl.program_id(0)
    j = pl.program_id(1)
    tm = h_cur.shape[0]
    chunk = x_ref.shape[0]
    w_refs = (w0_ref, w1_ref, w2_ref, w3_ref)

    for src, dst in ((c0_ref, c0_out), (c1_ref, c1_out), (c2_ref, c2_out)):
        dst[...] = src[...].astype(BF16)

    def norm_next(step=None):
        if step is None:
            rows = pl.ds(pl.multiple_of(j * chunk, chunk), chunk)
        else:
            rows = slice(step * chunk, (step + 1) * chunk)
        h_next[rows, :] = _rms_rows(x_ref[...], nw_ref[...]).astype(BF16)

    def sub_dot(s):
        return _dot(h_cur[...], w_refs[s][...])

    @pl.when(i == 0)
    def _():
        norm_next()
        tail_scr[...] = jnp.zeros_like(tail_scr)

    projecting = i >= 1

    def rope_then(tail_fn, step):
        if step == 0:
            for kb in range(D_MODEL // MXU_N):
                h_cur[:, kb * MXU_N:(kb + 1) * MXU_N] = h_next[:, kb * MXU_N:(kb + 1) * MXU_N]
        norm_next(step)
        cos = cos_ref[...]
        sin = sin_ref[...]
        for s in range(N_SUB // 2):
            acc = sub_dot(s)
            for hd in range(SUB_N // RET_HEAD_DIM):
                base = s * SUB_N + hd * RET_HEAD_DIM
                x1 = acc[:, hd * RET_HEAD_DIM:hd * RET_HEAD_DIM + LANES]
                x2 = acc[:, hd * RET_HEAD_DIM + LANES:(hd + 1) * RET_HEAD_DIM]
                qkvg_ref[:, base:base + LANES] = (x1 * cos - x2 * sin).astype(BF16)
                qkvg_ref[:, base + LANES:base + RET_HEAD_DIM] = (x2 * cos + x1 * sin).astype(BF16)
        for s in range(N_SUB // 2, N_SUB):
            qkvg_ref[:, _sub_cols(s)] = tail_fn(sub_dot(s)).astype(BF16)

    @pl.when(jnp.logical_and(projecting, j == 0))
    def _():
        rope_then(lambda v: v, 0)

    @pl.when(jnp.logical_and(projecting, j == 1))
    def _():
        rope_then(_silu, 1)

    @pl.when(jnp.logical_and(projecting, j >= 2))
    def _():
        cs = j - 2
        norm_next()
        cu = sub_dot(0) * sub_dot(1)
        at_seq_start = ((i - 1) % seq_tiles) == 0
        halo_rows = slice(HALO - (CONV_K - 1), HALO)
        cu_scr[halo_rows, :] = jnp.where(at_seq_start, 0.0, tail_scr[cs, halo_rows, :])
        cu_scr[HALO:HALO + tm, :] = cu
        tail_scr[cs, halo_rows, :] = cu_scr[HALO + tm - (CONV_K - 1):HALO + tm, :]
        cw = cw_ref[...]
        conv = cw[CONV_K - 1:CONV_K, :] * cu
        for t in range(CONV_K - 1):
            shift = CONV_K - 1 - t
            conv = conv + cw[t:t + 1, :] * cu_scr[HALO - shift:HALO - shift + tm, :]
        gated = sub_dot(2) * conv
        conv_ref[...] = (gated * _silu(sub_dot(3))).astype(BF16)


def _even_in(x2d, nw, w, cos, sin, cw, to_cast, seq):
    t = x2d.shape[0]
    tm = IN_TM
    n_tiles = t // tm
    seq_tiles = seq // tm
    n_ret_steps = 4 * RET_WIDTH // IN_TN
    n_conv_steps = CONV_WIDTH // SUB_N
    nj = n_ret_steps + n_conv_steps
    chunk = tm // nj
    n_proj_steps = n_tiles * nj
    conv_first = tuple((4 * RET_WIDTH + b * CONV_WIDTH) // SUB_N for b in (1, 2, 0, 3))

    def tile(i):
        return jnp.maximum(i - 1, 0)

    def step(i, j):
        return jnp.where(i == 0, 0, j)

    def w_spec(s):
        def index(i, j):
            jj = step(i, j)
            half = N_SUB // 2
            ret_idx = half * jnp.minimum(jj, n_ret_steps - 1) + (s % half) + (s // half) * N_SUB
            conv_idx = conv_first[s] + jj - n_ret_steps
            return 0, jnp.where(jj < n_ret_steps, ret_idx, conv_idx)
        return pl.BlockSpec((D_MODEL, SUB_N), index)

    def cast_spec(a):
        rows = a.shape[0] // n_proj_steps
        assert rows * n_proj_steps == a.shape[0] and rows % BF16_ROWS == 0
        return pl.BlockSpec((rows, a.shape[1]), lambda i, j: (tile(i) * nj + step(i, j), 0))

    def table_spec():
        return pl.BlockSpec((tm, LANES), lambda i, j: (tile(i) % seq_tiles, 0))

    kern = functools.partial(_even_in_kernel, seq_tiles=seq_tiles)
    return pl.pallas_call(
        kern,
        grid=(n_tiles + 1, nj),
        in_specs=[
            pl.BlockSpec((chunk, D_MODEL), lambda i, j: (jnp.minimum(i, n_tiles - 1) * nj + j, 0)),
            pl.BlockSpec((1, D_MODEL), lambda i, j: (0, 0), pipeline_mode=pl.Buffered(1)),
            w_spec(0), w_spec(1), w_spec(2), w_spec(3),
            table_spec(), table_spec(),
            pl.BlockSpec((SUBLANES, SUB_N), lambda i, j: (0, jnp.maximum(step(i, j) - n_ret_steps, 0))),
        ] + [cast_spec(a) for a in to_cast],
        out_specs=[
            pl.BlockSpec((tm, IN_TN),
                         lambda i, j: (tile(i), jnp.minimum(step(i, j), n_ret_steps - 1))),
            pl.BlockSpec((tm, SUB_N),
                         lambda i, j: (tile(i), jnp.maximum(step(i, j) - n_ret_steps, 0))),
        ] + [cast_spec(a) for a in to_cast],
        out_shape=[
            jax.ShapeDtypeStruct((t, 4 * RET_WIDTH), BF16),
            jax.ShapeDtypeStruct((t, CONV_WIDTH), BF16),
        ] + [jax.ShapeDtypeStruct(a.shape, BF16) for a in to_cast],
        scratch_shapes=[
            pltpu.VMEM((tm, D_MODEL), BF16),
            pltpu.VMEM((tm, D_MODEL), BF16),
            pltpu.VMEM((HALO + tm, SUB_N), F32),
            pltpu.VMEM((n_conv_steps, HALO, SUB_N), F32),
        ],
        compiler_params=pltpu.CompilerParams(
            dimension_semantics=("arbitrary", "arbitrary"),
            vmem_limit_bytes=VMEM_LIMIT),
        name="even_in",
    )(x2d, nw, w, w, w, w, cos, sin, cw, *to_cast)


def _even_out_kernel(qkvg_ref, conv_ref, x_ref, w_ref,
                     dmat_ref, qdec_ref, kdec_ref, cdec_ref, nw_next_ref,
                     o_ref, h_next_ref, state_scr, y_scr):
    tm = x_ref.shape[0]
    q_ref, v_ref, k_ref, g_ref = (
        qkvg_ref.at[:, n * RET_WIDTH:(n + 1) * RET_WIDTH] for n in range(4))

    @pl.when(pl.program_id(1) == 0)
    def _():
        state_scr[...] = jnp.zeros_like(state_scr)

    for c in range(tm // RET_CHUNK):
        rows = slice(c * RET_CHUNK, (c + 1) * RET_CHUNK)
        for h in range(RET_HEADS):
            cols = slice(h * RET_HEAD_DIM, (h + 1) * RET_HEAD_DIM)
            qh = q_ref[rows, cols]
            kh = k_ref[rows, cols]
            vh = v_ref[rows, cols]
            scores = _dot_nt(qh, kh) * dmat_ref[h]
            inner = _dot(scores.astype(BF16), vh)
            state = state_scr[h]
            cross = _dot(qh, state.astype(BF16)) * qdec_ref[h]
            o = inner + cross
            kd = (kh.astype(F32) * kdec_ref[h]).astype(BF16)
            state_scr[h] = cdec_ref[h] * state + _dot_tn(kd, vh)
            ms = jnp.mean(o * o, axis=-1, keepdims=True)
            ro = o * lax.rsqrt(ms + EPS) * g_ref[rows, cols].astype(F32)
            y_scr[rows, cols] = ro.astype(BF16)
    y_scr[:, RET_WIDTH:] = conv_ref[...]

    ssq = jnp.zeros((tm, 1), F32)
    for n in range(D_MODEL // SUB_N):
        xn = x_ref[:, _sub_cols(n)] + _dot(y_scr[...], w_ref[:, _sub_cols(n)])
        o_ref[:, _sub_cols(n)] = xn
        ssq = ssq + jnp.sum(xn * xn, axis=-1, keepdims=True)
    r = lax.rsqrt(ssq * (1.0 / D_MODEL) + EPS)
    for n in range(D_MODEL // SUB_N):
        h_next_ref[:, _sub_cols(n)] = (o_ref[:, _sub_cols(n)] * r * nw_next_ref[:, _sub_cols(n)]).astype(BF16)


def _even_out(qkvg, conv, x2d, w_out, dmat, qdec, kdec, cdec, nw_next, batch, seq):
    t = x2d.shape[0]
    tm = OUT_TM
    ns = seq // tm

    def row(b, i):
        return b * ns + i

    def tile_spec(width):
        return pl.BlockSpec((tm, width), lambda b, i: (row(b, i), 0))

    def outer(qkvg_hbm, conv_hbm, x_hbm, w_ref, dmat_ref, qdec_ref, kdec_ref, cdec_ref, nw_ref,
              o_hbm, h_hbm, state_scr, y_scr):
        def tile_body(qkvg_ref, conv_ref, x_ref, o_ref, h_ref):
            _even_out_kernel(qkvg_ref, conv_ref, x_ref, w_ref, dmat_ref, qdec_ref, kdec_ref,
                             cdec_ref, nw_ref, o_ref, h_ref, state_scr, y_scr)

        pltpu.emit_pipeline(
            tile_body,
            grid=(batch, ns),
            in_specs=[tile_spec(4 * RET_WIDTH), tile_spec(CONV_WIDTH), tile_spec(D_MODEL)],
            out_specs=[tile_spec(D_MODEL), tile_spec(D_MODEL)],
            trace_scopes=False,
        )(qkvg_hbm, conv_hbm, x_hbm, o_hbm, h_hbm)

    streamed = pl.BlockSpec(memory_space=pl.ANY)
    resident = pl.BlockSpec(memory_space=pltpu.VMEM)
    return pl.pallas_call(
        outer,
        in_specs=[streamed, streamed, streamed] + [resident] * 6,
        out_specs=[streamed, streamed],
        out_shape=[
            jax.ShapeDtypeStruct((t, D_MODEL), F32),
            jax.ShapeDtypeStruct((t, D_MODEL), BF16),
        ],
        scratch_shapes=[
            pltpu.VMEM((RET_HEADS, RET_HEAD_DIM, RET_HEAD_DIM), F32),
            pltpu.VMEM((tm, D_MODEL), BF16),
        ],
        compiler_params=pltpu.CompilerParams(vmem_limit_bytes=VMEM_LIMIT),
        name="even_out",
    )(qkvg, conv, x2d, w_out, dmat, qdec, kdec, cdec, nw_next)


def _odd_in_kernel(h_ref, w_ref,
                   cos_ref, sin_ref, qgain_ref, kgain_ref, ones_ref,
                   qg_ref, kv_ref):
    q_blocks = ATTN_WIDTH // SUB_N
    kv_block = q_blocks
    gate_block0 = kv_block + 1

    def sub_dot(block):
        return _dot(h_ref[...], w_ref[:, block * SUB_N:(block + 1) * SUB_N])

    def dup_heads(x):
        lane = lax.broadcasted_iota(jnp.int32, (x.shape[0], LANES), 1)
        low = lane < ATTN_HEAD_DIM
        blocks = []
        for b in range(x.shape[1] // LANES):
            xb = x[:, b * LANES:(b + 1) * LANES]
            swapped = pltpu.roll(xb, ATTN_HEAD_DIM, axis=1)
            blocks += [jnp.where(low, xb, swapped), jnp.where(low, swapped, xb)]
        return jnp.concatenate(blocks, axis=1)

    def norm_rope(acc, gain_ref):
        width = acc.shape[1]
        sq = (acc * acc).astype(BF16)
        ss = jnp.concatenate(
            [_dot(sq[:, c * MXU_N:(c + 1) * MXU_N], ones_ref[...]) for c in range(width // MXU_N)],
            axis=1)
        r = lax.rsqrt(ss * (1.0 / ATTN_HEAD_DIM) + EPS)
        half = ATTN_HEAD_DIM // 2
        lane = lax.broadcasted_iota(jnp.int32, acc.shape, 1)
        first_half = (lane % ATTN_HEAD_DIM) < half
        partner = jnp.where(first_half,
                            pltpu.roll(acc, width - half, axis=1),
                            pltpu.roll(acc, half, axis=1))
        reps = width // LANES
        ta = cos_ref[...] * gain_ref[0:1, :]
        tb = sin_ref[...] * gain_ref[1:2, :]
        ta_w = jnp.concatenate([ta] * reps, axis=1)
        tb_w = jnp.concatenate([tb] * reps, axis=1)
        return r * (acc * ta_w + partner * tb_w)

    def kv_part():
        acc = sub_dot(kv_block)
        k = norm_rope(acc[:, :KV_WIDTH], kgain_ref)
        kv_ref[:, :2 * KV_WIDTH] = dup_heads(k).astype(BF16)
        kv_ref[:, 2 * KV_WIDTH:] = dup_heads(acc[:, KV_WIDTH:]).astype(BF16)

    kv_part()
    for s in range(q_blocks):
        qg_ref[:, _sub_cols(s)] = norm_rope(sub_dot(s), qgain_ref).astype(BF16)
    for s in range(q_blocks):
        qg_ref[:, _sub_cols(q_blocks + s)] = _silu(sub_dot(gate_block0 + s)).astype(BF16)


def _odd_in(h2d, w, cos_l, sin_l, q_gain, k_gain, ones_bd, seq):
    t = h2d.shape[0]
    tm = IN_TM
    seq_tiles = seq // tm
    assert 2 * KV_WIDTH == SUB_N

    def tab_spec():
        return pl.BlockSpec((tm, LANES), lambda i: (i % seq_tiles, 0))

    def outer(h_hbm, w_ref, cos_hbm, sin_hbm, qgain_ref, kgain_ref, ones_ref, qg_hbm, kv_hbm):
        def tile_body(h_ref, cos_ref, sin_ref, qg_ref, kv_ref):
            _odd_in_kernel(h_ref, w_ref, cos_ref, sin_ref, qgain_ref, kgain_ref, ones_ref,
                           qg_ref, kv_ref)

        pltpu.emit_pipeline(
            tile_body,
            grid=(t // tm,),
            in_specs=[pl.BlockSpec((tm, D_MODEL), lambda i: (i, 0)), tab_spec(), tab_spec()],
            out_specs=[
                pl.BlockSpec((tm, 2 * ATTN_WIDTH), lambda i: (i, 0)),
                pl.BlockSpec((tm, 4 * KV_WIDTH), lambda i: (i, 0)),
            ],
            trace_scopes=False,
        )(h_hbm, cos_hbm, sin_hbm, qg_hbm, kv_hbm)

    streamed = pl.BlockSpec(memory_space=pl.ANY)
    resident = pl.BlockSpec(memory_space=pltpu.VMEM)
    return pl.pallas_call(
        outer,
        in_specs=[streamed, resident, streamed, streamed, resident, resident, resident],
        out_specs=[streamed, streamed],
        out_shape=[
            jax.ShapeDtypeStruct((t, 2 * ATTN_WIDTH), BF16),
            jax.ShapeDtypeStruct((t, 4 * KV_WIDTH), BF16),
        ],
        compiler_params=pltpu.CompilerParams(vmem_limit_bytes=VMEM_LIMIT),
        name="odd_in",
    )(h2d, w, cos_l, sin_l, q_gain, k_gain, ones_bd)


def _odd_out_kernel(sink_ref, qg_ref, kv_ref, kvp_ref, x_ref, w_ref,
                    o_ref, y_scr):
    tm = x_ref.shape[0]
    q_ref = qg_ref.at[:, :ATTN_WIDTH]
    g_ref = qg_ref.at[:, ATTN_WIDTH:]
    at_seq_start = pl.program_id(1) == 0
    pairs = ATTN_GROUP // 2
    half_w = 2 * KV_WIDTH

    lane_kv = lax.broadcasted_iota(jnp.int32, (2 * BLOCK, LANES), 1)
    low_kv = lane_kv < ATTN_HEAD_DIM
    rr = lax.broadcasted_iota(jnp.int32, (BLOCK, BLOCK), 0)
    cc = lax.broadcasted_iota(jnp.int32, (BLOCK, BLOCK), 1)
    tri = cc <= rr
    low_o = cc < ATTN_HEAD_DIM

    for bl in range(tm // BLOCK):
        rows = slice(bl * BLOCK, (bl + 1) * BLOCK)
        for c in range(ATTN_KV_HEADS):
            kcols = slice(c * LANES, (c + 1) * LANES)
            vcols = slice(half_w + c * LANES, half_w + (c + 1) * LANES)
            if bl == 0:
                k_prev = kvp_ref[:, kcols]
                v_prev = kvp_ref[:, vcols]
            else:
                prows = slice((bl - 1) * BLOCK, bl * BLOCK)
                k_prev = kv_ref[prows, kcols]
                v_prev = kv_ref[prows, vcols]
            kk = jnp.concatenate([k_prev, kv_ref[rows, kcols]], axis=0)
            vv = jnp.concatenate([v_prev, kv_ref[rows, vcols]], axis=0)
            zero = jnp.zeros_like(kk)
            k_par = (jnp.where(low_kv, kk, zero), jnp.where(low_kv, zero, kk))
            qs = jnp.concatenate(
                [q_ref[rows, (pairs * c + a) * LANES:(pairs * c + a + 1) * LANES]
                 for a in range(pairs)], axis=0)
            probs = []
            denoms = []
            for par in range(2):
                s = _dot_nt(qs, k_par[par])
                for a in range(pairs):
                    sa = s[a * BLOCK:(a + 1) * BLOCK]
                    s_prev = sa[:, :BLOCK]
                    if bl == 0:
                        s_prev = jnp.where(at_seq_start, NEG, s_prev)
                    f = jnp.where(tri, sa[:, BLOCK:], s_prev)
                    sink = sink_ref[ATTN_GROUP * c + 2 * a + par]
                    m = jnp.maximum(jnp.max(f, axis=1, keepdims=True), sink)
                    p = jnp.exp(f - m)
                    denoms.append(jnp.sum(p, axis=1, keepdims=True) + jnp.exp(sink - m))
                    probs.append(jnp.concatenate(
                        [jnp.where(tri, 0.0, p), jnp.where(tri, p, 0.0)], axis=1).astype(BF16))
            o = _dot(jnp.concatenate(probs, axis=0), vv)
            for a in range(pairs):
                o_even = o[a * BLOCK:(a + 1) * BLOCK] / denoms[a]
                o_odd = o[(pairs + a) * BLOCK:(pairs + a + 1) * BLOCK] / denoms[pairs + a]
                ocols = slice((pairs * c + a) * LANES, (pairs * c + a + 1) * LANES)
                gated = jnp.where(low_o, o_even, o_odd) * g_ref[rows, ocols].astype(F32)
                y_scr[rows, ocols] = gated.astype(BF16)

    for n in range(D_MODEL // SUB_N):
        o_ref[:, _sub_cols(n)] = x_ref[:, _sub_cols(n)] + _dot(y_scr[...], w_ref[:, _sub_cols(n)])


def _odd_out(sinks, qg, kv, x2d, w_out, batch, seq):
    t = x2d.shape[0]
    tm = OUT_TM
    ns = seq // tm
    bpt = tm // BLOCK

    def row(b, i):
        return b * ns + i

    def outer(sink_ref, qg_hbm, kv_hbm, x_hbm, w_ref, o_hbm, y_scr):
        def tile_body(qg_ref, kv_ref, kvp_ref, x_ref, o_ref):
            _odd_out_kernel(sink_ref, qg_ref, kv_ref, kvp_ref, x_ref, w_ref, o_ref, y_scr)

        pltpu.emit_pipeline(
            tile_body,
            grid=(batch, ns),
            in_specs=[
                pl.BlockSpec((tm, 2 * ATTN_WIDTH), lambda b, i: (row(b, i), 0)),
                pl.BlockSpec((tm, 4 * KV_WIDTH), lambda b, i: (row(b, i), 0)),
                pl.BlockSpec((BLOCK, 4 * KV_WIDTH),
                             lambda b, i: (jnp.maximum(row(b, i) * bpt - 1, 0), 0)),
                pl.BlockSpec((tm, D_MODEL), lambda b, i: (row(b, i), 0)),
            ],
            out_specs=[pl.BlockSpec((tm, D_MODEL), lambda b, i: (row(b, i), 0))],
            trace_scopes=False,
        )(qg_hbm, kv_hbm, kv_hbm, x_hbm, o_hbm)

    streamed = pl.BlockSpec(memory_space=pl.ANY)
    return pl.pallas_call(
        outer,
        in_specs=[pl.BlockSpec(memory_space=pltpu.SMEM), streamed, streamed, streamed,
                  pl.BlockSpec(memory_space=pltpu.VMEM)],
        out_specs=streamed,
        out_shape=jax.ShapeDtypeStruct((t, D_MODEL), F32),
        scratch_shapes=[pltpu.VMEM((tm, ATTN_WIDTH), BF16)],
        compiler_params=pltpu.CompilerParams(vmem_limit_bytes=VMEM_LIMIT),
        name="odd_out",
    )(sinks, qg, kv, x2d, w_out)


def _rope_cos_sin(seq, dim):
    inv = 1.0 / (ROPE_THETA ** (np.arange(0, dim, 2, dtype=np.float64) / dim))
    ang = np.arange(seq, dtype=np.float64)[:, None] * inv[None, :]
    return np.cos(ang), np.sin(ang)


def _retention_tables():
    h = np.arange(RET_HEADS, dtype=np.float64)
    log_g = np.log(1.0 - 2.0 ** (-5.0 - h))
    idx = np.arange(RET_CHUNK, dtype=np.float64)
    diff = idx[:, None] - idx[None, :]
    scale = RET_HEAD_DIM ** -0.5
    intra = np.where(diff >= 0, np.exp(log_g[:, None, None] * np.maximum(diff, 0.0)), 0.0)
    q_dec = np.exp(log_g[:, None] * (idx[None, :] + 1.0))
    k_dec = np.exp(log_g[:, None] * (RET_CHUNK - 1.0 - idx[None, :]))
    c_dec = np.exp(log_g * RET_CHUNK)
    wide = (RET_HEADS, RET_CHUNK, RET_HEAD_DIM)
    dmat = jnp.asarray(intra * scale, F32)
    qdec = jnp.asarray(np.broadcast_to(q_dec[:, :, None], wide), F32)
    kdec = jnp.asarray(np.broadcast_to(k_dec[:, :, None] * scale, wide), F32)
    cdec = jnp.asarray(np.broadcast_to(c_dec[:, None, None], (RET_HEADS, 1, RET_HEAD_DIM)), F32)
    return dmat, qdec, kdec, cdec


def _head_rope_tables(seq):
    reps = LANES // ATTN_HEAD_DIM
    cos, sin = _rope_cos_sin(seq, ATTN_HEAD_DIM)
    cos_l = jnp.asarray(np.tile(np.concatenate([cos, cos], axis=1), (1, reps)), F32)
    sin_l = jnp.asarray(np.tile(np.concatenate([-sin, sin], axis=1), (1, reps)), F32)
    return cos_l, sin_l


def _head_gain_rows(norm_w, scale):
    half = ATTN_HEAD_DIM // 2
    reps = LANES // ATTN_HEAD_DIM
    w = norm_w.astype(F32) * scale
    w_partner = jnp.concatenate([w[half:], w[:half]])
    rows = jnp.stack([jnp.tile(w, reps), jnp.tile(w_partner, reps)])
    return jnp.pad(rows, ((0, SUBLANES - rows.shape[0]), (0, 0)))


def kernel(x, ev_norm_w, ev_w_in, ev_conv_w, ev_w_out, od_norm_w, od_w_in,
           od_q_norm_w, od_k_norm_w, od_sinks, od_w_out):
    batch, seq, d = x.shape
    assert d == D_MODEL and seq % IN_TM == 0 and seq % OUT_TM == 0
    x2d = x.reshape(batch * seq, d)

    cos, sin = _rope_cos_sin(seq, RET_HEAD_DIM)
    cw = jnp.pad(ev_conv_w[0], ((0, SUBLANES - CONV_K), (0, 0)))
    qkvg, conv, w_out_e, w_in_o, w_out_o = _even_in(
        x2d, ev_norm_w[0][None, :], ev_w_in[0].astype(BF16),
        jnp.asarray(cos, F32), jnp.asarray(sin, F32), cw,
        (ev_w_out[0], od_w_in[0], od_w_out[0]), seq)
    dmat, qdec, kdec, cdec = _retention_tables()
    x2d, h2d = _even_out(qkvg, conv, x2d, w_out_e, dmat, qdec, kdec, cdec,
                         od_norm_w[0][None, :], batch, seq)

    cos_l, sin_l = _head_rope_tables(seq)
    q_gain = _head_gain_rows(od_q_norm_w[0], ATTN_HEAD_DIM ** -0.5)
    k_gain = _head_gain_rows(od_k_norm_w[0], 1.0)
    head_id = np.arange(MXU_N) // ATTN_HEAD_DIM
    ones_bd = jnp.asarray(head_id[:, None] == head_id[None, :], BF16)
    qg, kv = _odd_in(h2d, w_in_o, cos_l, sin_l, q_gain, k_gain, ones_bd, seq)
    x2d = _odd_out(od_sinks[0].astype(F32), qg, kv, x2d, w_out_o, batch, seq)
    return x2d.reshape(batch, seq, d)
```

```python
import numpy as np
import jax
import jax.numpy as jnp
from jax import lax
from jax.experimental import pallas as pl
from jax.experimental.pallas import tpu as pltpu

D_MODEL = 2048
RET_HEADS = 4
RET_HEAD_DIM = 256
RET_WIDTH = RET_HEADS * RET_HEAD_DIM
CONV_WIDTH = D_MODEL - RET_WIDTH
CONV_K = 3
RET_CHUNK = 256
ATTN_HEAD_DIM = 64
ATTN_Q_HEADS = 32
ATTN_KV_HEADS = 4
ATTN_GROUP = ATTN_Q_HEADS // ATTN_KV_HEADS
ATTN_WIDTH = ATTN_Q_HEADS * ATTN_HEAD_DIM
KV_WIDTH = ATTN_KV_HEADS * ATTN_HEAD_DIM
BLOCK = 128
ROPE_THETA = 10000.0
EPS = 1e-6
NEG = -1e30

LANES = 128
SUBLANES = 8
BF16_ROWS = 16
MXU_N = 256
SUB_N = 512
N_SUB = 4
IN_TM = 1024
IN_TN = N_SUB * SUB_N
OUT_TM = 512
HALO = SUBLANES
V7X_VMEM_BYTES = 64 * 1024 * 1024
VMEM_LIMIT = V7X_VMEM_BYTES * 7 // 8

F32 = jnp.float32
BF16 = jnp.bfloat16


def _dot(a, b):
    return jnp.dot(a, b, preferred_element_type=F32)


def _dot_nt(a, b):
    return lax.dot_general(a, b, (((1,), (1,)), ((), ())), preferred_element_type=F32)


def _dot_tn(a, b):
    return lax.dot_general(a, b, (((0,), (0,)), ((), ())), preferred_element_type=F32)


def _silu(g):
    return (0.5 * g) * (1.0 + jnp.tanh(0.5 * g))


def _rms_rows(x, nw):
    ms = jnp.mean(x * x, axis=-1, keepdims=True)
    return x * lax.rsqrt(ms + EPS) * nw


def _sub_cols(s):
    return slice(s * SUB_N, (s + 1) * SUB_N)


def _even_in_kernel(x_ref, nw_ref, w0_ref, w1_ref, w2_ref, w3_ref,
                    cos_ref, sin_ref, cw_ref, c0_ref, c1_ref, c2_ref,
                    qkvg_ref, conv_ref, c0_out, c1_out, c2_out,
                    h_cur, h_next, cu_scr, tail_scr, *, seq_tiles):
    i = pl.program_id(0)
    j = pl.program_id(1)
    tm = h_cur.shape[0]
    chunk = x_ref.shape[0]
    w_refs = (w0_ref, w1_ref, w2_ref, w3_ref)

    for src, dst in ((c0_ref, c0_out), (c1_ref, c1_out), (c2_ref, c2_out)):
        dst[...] = src[...].astype(BF16)

    def norm_next(step=None):
        if step is None:
            rows = pl.ds(pl.multiple_of(j * chunk, chunk), chunk)
        else:
            rows = slice(step * chunk, (step + 1) * chunk)
        h_next[rows, :] = _rms_rows(x_ref[...], nw_ref[...]).astype(BF16)

    def sub_dot(s):
        return _dot(h_cur[...], w_refs[s][...])

    @pl.when(i == 0)
    def _():
        norm_next()
        tail_scr[...] = jnp.zeros_like(tail_scr)

    projecting = i >= 1

    def rope_then(tail_fn, step):
        if step == 0:
            for kb in range(D_MODEL // MXU_N):
                h_cur[:, kb * MXU_N:(kb + 1) * MXU_N] = h_next[:, kb * MXU_N:(kb + 1) * MXU_N]
        norm_next(step)
        cos = cos_ref[...]
        sin = sin_ref[...]
        for s in range(N_SUB // 2):
            acc = sub_dot(s)
            for hd in range(SUB_N // RET_HEAD_DIM):
                base = s * SUB_N + hd * RET_HEAD_DIM
                x1 = acc[:, hd * RET_HEAD_DIM:hd * RET_HEAD_DIM + LANES]
                x2 = acc[:, hd * RET_HEAD_DIM + LANES:(hd + 1) * RET_HEAD_DIM]
                qkvg_ref[:, base:base + LANES] = (x1 * cos - x2 * sin).astype(BF16)
                qkvg_ref[:, base + LANES:base + RET_HEAD_DIM] = (x2 * cos + x1 * sin).astype(BF16)
        for s in range(N_SUB // 2, N_SUB):
            qkvg_ref[:, _sub_cols(s)] = tail_fn(sub_dot(s)).astype(BF16)

    @pl.when(jnp.logical_and(projecting, j == 0))
    def _():
        rope_then(lambda v: v, 0)

    @pl.when(jnp.logical_and(projecting, j == 1))
    def _():
        rope_then(_silu, 1)

    @pl.when(jnp.logical_and(projecting, j >= 2))
    def _():
        cs = j - 2
        norm_next()
        cu = sub_dot(0) * sub_dot(1)
        at_seq_start = ((i - 1) % seq_tiles) == 0
        halo_rows = slice(HALO - (CONV_K - 1), HALO)
        cu_scr[halo_rows, :] = jnp.where(at_seq_start, 0.0, tail_scr[cs, halo_rows, :])
        cu_scr[HALO:HALO + tm, :] = cu
        tail_scr[cs, halo_rows, :] = cu_scr[HALO + tm - (CONV_K - 1):HALO + tm, :]
        cw = cw_ref[...]
        conv = cw[CONV_K - 1:CONV_K, :] * cu
        for t in range(CONV_K - 1):
            shift = CONV_K - 1 - t
            conv = conv + cw[t:t + 1, :] * cu_scr[HALO - shift:HALO - shift + tm, :]
        gated = sub_dot(2) * conv
        conv_ref[...] = (gated * _silu(sub_dot(3))).astype(BF16)


def _even_in(x2d, nw, w, cos, sin, cw, to_cast, seq):
    t = x2d.shape[0]
    tm = IN_TM
    n_tiles = t // tm
    seq_tiles = seq // tm
    n_ret_steps = 4 * RET_WIDTH // IN_TN
    n_conv_steps = CONV_WIDTH // SUB_N
    nj = n_ret_steps + n_conv_steps
    chunk = tm // nj
    n_proj_steps = n_tiles * nj
    conv_first = tuple((4 * RET_WIDTH + b * CONV_WIDTH) // SUB_N for b in (1, 2, 0, 3))

    def tile(i):
        return jnp.maximum(i - 1, 0)

    def step(i, j):
        return jnp.where(i == 0, 0, j)

    def w_spec(s):
        def index(i, j):
            jj = step(i, j)
            half = N_SUB // 2
            ret_idx = half * jnp.minimum(jj, n_ret_steps - 1) + (s % half) + (s // half) * N_SUB
            conv_idx = conv_first[s] + jj - n_ret_steps
            return 0, jnp.where(jj < n_ret_steps, ret_idx, conv_idx)
        return pl.BlockSpec((D_MODEL, SUB_N), index)

    def cast_spec(a):
        rows = a.shape[0] // n_proj_steps
        assert rows * n_proj_steps == a.shape[0] and rows % BF16_ROWS == 0
        return pl.BlockSpec((rows, a.shape[1]), lambda i, j: (tile(i) * nj + step(i, j), 0))

    def table_spec():
        return pl.BlockSpec((tm, LANES), lambda i, j: (tile(i) % seq_tiles, 0))

    n_cast = len(to_cast)
    in_specs = [
        pl.BlockSpec((chunk, D_MODEL), lambda i, j: (jnp.minimum(i, n_tiles - 1) * nj + j, 0)),
        w_spec(0), w_spec(1), w_spec(2), w_spec(3),
        table_spec(), table_spec(),
        pl.BlockSpec((SUBLANES, SUB_N), lambda i, j: (0, jnp.maximum(step(i, j) - n_ret_steps, 0))),
    ] + [cast_spec(a) for a in to_cast]
    out_specs = [
        pl.BlockSpec((tm, IN_TN), lambda i, j: (tile(i), jnp.minimum(step(i, j), n_ret_steps - 1))),
        pl.BlockSpec((tm, SUB_N), lambda i, j: (tile(i), jnp.maximum(step(i, j) - n_ret_steps, 0))),
    ] + [cast_spec(a) for a in to_cast]

    def outer(*refs):
        x_hbm, nw_ref, w_hbm, cos_hbm, sin_hbm, cw_hbm = refs[:6]
        cast_hbm = refs[6:6 + n_cast]
        outs_hbm = refs[6 + n_cast:8 + 2 * n_cast]
        scratch = refs[8 + 2 * n_cast:]

        def step_body(x_ref, w0, w1, w2, w3, cos_ref, sin_ref, cw_ref, *rest):
            casts_in = rest[:n_cast]
            outs = rest[n_cast:]
            _even_in_kernel(x_ref, nw_ref, w0, w1, w2, w3, cos_ref, sin_ref, cw_ref, *casts_in,
                            *outs, *scratch, seq_tiles=seq_tiles)

        pltpu.emit_pipeline(
            step_body, grid=(n_tiles + 1, nj), in_specs=in_specs, out_specs=out_specs,
            trace_scopes=False,
        )(x_hbm, w_hbm, w_hbm, w_hbm, w_hbm, cos_hbm, sin_hbm, cw_hbm, *cast_hbm, *outs_hbm)

    streamed = pl.BlockSpec(memory_space=pl.ANY)
    return pl.pallas_call(
        outer,
        in_specs=[streamed, pl.BlockSpec(memory_space=pltpu.VMEM)] + [streamed] * (4 + n_cast),
        out_specs=[streamed] * (2 + n_cast),
        out_shape=[
            jax.ShapeDtypeStruct((t, 4 * RET_WIDTH), BF16),
            jax.ShapeDtypeStruct((t, CONV_WIDTH), BF16),
        ] + [jax.ShapeDtypeStruct(a.shape, BF16) for a in to_cast],
        scratch_shapes=[
            pltpu.VMEM((tm, D_MODEL), BF16),
            pltpu.VMEM((tm, D_MODEL), BF16),
            pltpu.VMEM((HALO + tm, SUB_N), F32),
            pltpu.VMEM((n_conv_steps, HALO, SUB_N), F32),
        ],
        compiler_params=pltpu.CompilerParams(vmem_limit_bytes=VMEM_LIMIT),
        name="even_in",
    )(x2d, nw, w, cos, sin, cw, *to_cast)


def _even_out_kernel(qkvg_ref, conv_ref, x_ref, w_ref,
                     dmat_ref, qdec_ref, kdec_ref, cdec_ref, nw_next_ref,
                     o_ref, h_next_ref, state_scr, y_scr):
    tm = x_ref.shape[0]
    q_ref, v_ref, k_ref, g_ref = (
        qkvg_ref.at[:, n * RET_WIDTH:(n + 1) * RET_WIDTH] for n in range(4))

    @pl.when(pl.program_id(1) == 0)
    def _():
        state_scr[...] = jnp.zeros_like(state_scr)

    for c in range(tm // RET_CHUNK):
        rows = slice(c * RET_CHUNK, (c + 1) * RET_CHUNK)
        for h in range(RET_HEADS):
            cols = slice(h * RET_HEAD_DIM, (h + 1) * RET_HEAD_DIM)
            qh = q_ref[rows, cols]
            kh = k_ref[rows, cols]
            vh = v_ref[rows, cols]
            scores = _dot_nt(qh, kh) * dmat_ref[h]
            inner = _dot(scores.astype(BF16), vh)
            state = state_scr[h]
            cross = _dot(qh, state.astype(BF16)) * qdec_ref[h]
            o = inner + cross
            kd = (kh.astype(F32) * kdec_ref[h]).astype(BF16)
            state_scr[h] = cdec_ref[h] * state + _dot_tn(kd, vh)
            ms = jnp.mean(o * o, axis=-1, keepdims=True)
            ro = o * lax.rsqrt(ms + EPS) * g_ref[rows, cols].astype(F32)
            y_scr[rows, cols] = ro.astype(BF16)
    y_scr[:, RET_WIDTH:] = conv_ref[...]

    ssq = jnp.zeros((tm, 1), F32)
    for n in range(D_MODEL // SUB_N):
        xn = x_ref[:, _sub_cols(n)] + _dot(y_scr[...], w_ref[:, _sub_cols(n)])
        o_ref[:, _sub_cols(n)] = xn
        ssq = ssq + jnp.sum(xn * xn, axis=-1, keepdims=True)
    r = lax.rsqrt(ssq * (1.0 / D_MODEL) + EPS)
    for n in range(D_MODEL // SUB_N):
        h_next_ref[:, _sub_cols(n)] = (o_ref[:, _sub_cols(n)] * r * nw_next_ref[:, _sub_cols(n)]).astype(BF16)


def _even_out(qkvg, conv, x2d, w_out, dmat, qdec, kdec, cdec, nw_next, batch, seq):
    t = x2d.shape[0]
    tm = OUT_TM
    ns = seq // tm

    def row(b, i):
        return b * ns + i

    def tile_spec(width):
        return pl.BlockSpec((tm, width), lambda b, i: (row(b, i), 0))

    def outer(qkvg_hbm, conv_hbm, x_hbm, w_ref, dmat_ref, qdec_ref, kdec_ref, cdec_ref, nw_ref,
              o_hbm, h_hbm, state_scr, y_scr):
        def tile_body(qkvg_ref, conv_ref, x_ref, o_ref, h_ref):
            _even_out_kernel(qkvg_ref, conv_ref, x_ref, w_ref, dmat_ref, qdec_ref, kdec_ref,
                             cdec_ref, nw_ref, o_ref, h_ref, state_scr, y_scr)

        pltpu.emit_pipeline(
            tile_body,
            grid=(batch, ns),
            in_specs=[tile_spec(4 * RET_WIDTH), tile_spec(CONV_WIDTH), tile_spec(D_MODEL)],
            out_specs=[tile_spec(D_MODEL), tile_spec(D_MODEL)],
            trace_scopes=False,
        )(qkvg_hbm, conv_hbm, x_hbm, o_hbm, h_hbm)

    streamed = pl.BlockSpec(memory_space=pl.ANY)
    resident = pl.BlockSpec(memory_space=pltpu.VMEM)
    return pl.pallas_call(
        outer,
        in_specs=[streamed, streamed, streamed] + [resident] * 6,
        out_specs=[streamed, streamed],
        out_shape=[
            jax.ShapeDtypeStruct((t, D_MODEL), F32),
            jax.ShapeDtypeStruct((t, D_MODEL), BF16),
        ],
        scratch_shapes=[
            pltpu.VMEM((RET_HEADS, RET_HEAD_DIM, RET_HEAD_DIM), F32),
            pltpu.VMEM((tm, D_MODEL), BF16),
        ],
        compiler_params=pltpu.CompilerParams(vmem_limit_bytes=VMEM_LIMIT),
        name="even_out",
    )(qkvg, conv, x2d, w_out, dmat, qdec, kdec, cdec, nw_next)


def _odd_in_kernel(h_ref, w_ref,
                   cos_ref, sin_ref, qgain_ref, kgain_ref, ones_ref,
                   qg_ref, kv_ref):
    q_blocks = ATTN_WIDTH // SUB_N
    kv_block = q_blocks
    gate_block0 = kv_block + 1

    def sub_dot(block):
        return _dot(h_ref[...], w_ref[:, block * SUB_N:(block + 1) * SUB_N])

    def dup_heads(x):
        lane = lax.broadcasted_iota(jnp.int32, (x.shape[0], LANES), 1)
        low = lane < ATTN_HEAD_DIM
        blocks = []
        for b in range(x.shape[1] // LANES):
            xb = x[:, b * LANES:(b + 1) * LANES]
            swapped = pltpu.roll(xb, ATTN_HEAD_DIM, axis=1)
            blocks += [jnp.where(low, xb, swapped), jnp.where(low, swapped, xb)]
        return jnp.concatenate(blocks, axis=1)

    def norm_rope(acc, gain_ref):
        width = acc.shape[1]
        sq = (acc * acc).astype(BF16)
        ss = jnp.concatenate(
            [_dot(sq[:, c * MXU_N:(c + 1) * MXU_N], ones_ref[...]) for c in range(width // MXU_N)],
            axis=1)
        r = lax.rsqrt(ss * (1.0 / ATTN_HEAD_DIM) + EPS)
        half = ATTN_HEAD_DIM // 2
        lane = lax.broadcasted_iota(jnp.int32, acc.shape, 1)
        first_half = (lane % ATTN_HEAD_DIM) < half
        partner = jnp.where(first_half,
                            pltpu.roll(acc, width - half, axis=1),
                            pltpu.roll(acc, half, axis=1))
        reps = width // LANES
        ta = cos_ref[...] * gain_ref[0:1, :]
        tb = sin_ref[...] * gain_ref[1:2, :]
        ta_w = jnp.concatenate([ta] * reps, axis=1)
        tb_w = jnp.concatenate([tb] * reps, axis=1)
        return r * (acc * ta_w + partner * tb_w)

    def kv_part():
        acc = sub_dot(kv_block)
        k = norm_rope(acc[:, :KV_WIDTH], kgain_ref)
        kv_ref[:, :2 * KV_WIDTH] = dup_heads(k).astype(BF16)
        kv_ref[:, 2 * KV_WIDTH:] = dup_heads(acc[:, KV_WIDTH:]).astype(BF16)

    kv_part()
    for s in range(q_blocks):
        qg_ref[:, _sub_cols(s)] = norm_rope(sub_dot(s), qgain_ref).astype(BF16)
    for s in range(q_blocks):
        qg_ref[:, _sub_cols(q_blocks + s)] = _silu(sub_dot(gate_block0 + s)).astype(BF16)


def _odd_in(h2d, w, cos_l, sin_l, q_gain, k_gain, ones_bd, seq):
    t = h2d.shape[0]
    tm = IN_TM
    seq_tiles = seq // tm
    assert 2 * KV_WIDTH == SUB_N

    def tab_spec():
        return pl.BlockSpec((tm, LANES), lambda i: (i % seq_tiles, 0))

    def outer(h_hbm, w_ref, cos_hbm, sin_hbm, qgain_ref, kgain_ref, ones_ref, qg_hbm, kv_hbm):
        def tile_body(h_ref, cos_ref, sin_ref, qg_ref, kv_ref):
            _odd_in_kernel(h_ref, w_ref, cos_ref, sin_ref, qgain_ref, kgain_ref, ones_ref,
                           qg_ref, kv_ref)

        pltpu.emit_pipeline(
            tile_body,
            grid=(t // tm,),
            in_specs=[pl.BlockSpec((tm, D_MODEL), lambda i: (i, 0)), tab_spec(), tab_spec()],
            out_specs=[
                pl.BlockSpec((tm, 2 * ATTN_WIDTH), lambda i: (i, 0)),
                pl.BlockSpec((tm, 4 * KV_WIDTH), lambda i: (i, 0)),
            ],
            trace_scopes=False,
        )(h_hbm, cos_hbm, sin_hbm, qg_hbm, kv_hbm)

    streamed = pl.BlockSpec(memory_space=pl.ANY)
    resident = pl.BlockSpec(memory_space=pltpu.VMEM)
    return pl.pallas_call(
        outer,
        in_specs=[streamed, resident, streamed, streamed, resident, resident, resident],
        out_specs=[streamed, streamed],
        out_shape=[
            jax.ShapeDtypeStruct((t, 2 * ATTN_WIDTH), BF16),
            jax.ShapeDtypeStruct((t, 4 * KV_WIDTH), BF16),
        ],
        compiler_params=pltpu.CompilerParams(vmem_limit_bytes=VMEM_LIMIT),
        name="odd_in",
    )(h2d, w, cos_l, sin_l, q_gain, k_gain, ones_bd)


def _odd_out_kernel(sink_ref, qg_ref, kv_ref, kvp_ref, x_ref, w_ref,
                    o_ref, y_scr):
    tm = x_ref.shape[0]
    q_ref = qg_ref.at[:, :ATTN_WIDTH]
    g_ref = qg_ref.at[:, ATTN_WIDTH:]
    at_seq_start = pl.program_id(1) == 0
    pairs = ATTN_GROUP // 2
    half_w = 2 * KV_WIDTH

    lane_kv = lax.broadcasted_iota(jnp.int32, (2 * BLOCK, LANES), 1)
    low_kv = lane_kv < ATTN_HEAD_DIM
    rr = lax.broadcasted_iota(jnp.int32, (BLOCK, BLOCK), 0)
    cc = lax.broadcasted_iota(jnp.int32, (BLOCK, BLOCK), 1)
    tri = cc <= rr
    low_o = cc < ATTN_HEAD_DIM

    for bl in range(tm // BLOCK):
        rows = slice(bl * BLOCK, (bl + 1) * BLOCK)
        for c in range(ATTN_KV_HEADS):
            kcols = slice(c * LANES, (c + 1) * LANES)
            vcols = slice(half_w + c * LANES, half_w + (c + 1) * LANES)
            if bl == 0:
                k_prev = kvp_ref[:, kcols]
                v_prev = kvp_ref[:, vcols]
            else:
                prows = slice((bl - 1) * BLOCK, bl * BLOCK)
                k_prev = kv_ref[prows, kcols]
                v_prev = kv_ref[prows, vcols]
            kk = jnp.concatenate([k_prev, kv_ref[rows, kcols]], axis=0)
            vv = jnp.concatenate([v_prev, kv_ref[rows, vcols]], axis=0)
            zero = jnp.zeros_like(kk)
            k_par = (jnp.where(low_kv, kk, zero), jnp.where(low_kv, zero, kk))
            qs = jnp.concatenate(
                [q_ref[rows, (pairs * c + a) * LANES:(pairs * c + a + 1) * LANES]
                 for a in range(pairs)], axis=0)
            probs = []
            denoms = []
            for par in range(2):
                s = _dot_nt(qs, k_par[par])
                for a in range(pairs):
                    sa = s[a * BLOCK:(a + 1) * BLOCK]
                    s_prev = sa[:, :BLOCK]
                    if bl == 0:
                        s_prev = jnp.where(at_seq_start, NEG, s_prev)
                    f = jnp.where(tri, sa[:, BLOCK:], s_prev)
                    sink = sink_ref[ATTN_GROUP * c + 2 * a + par]
                    m = jnp.maximum(jnp.max(f, axis=1, keepdims=True), sink)
                    p = jnp.exp(f - m)
                    denoms.append(jnp.sum(p, axis=1, keepdims=True) + jnp.exp(sink - m))
                    probs.append(jnp.concatenate(
                        [jnp.where(tri, 0.0, p), jnp.where(tri, p, 0.0)], axis=1).astype(BF16))
            o = _dot(jnp.concatenate(probs, axis=0), vv)
            for a in range(pairs):
                o_even = o[a * BLOCK:(a + 1) * BLOCK] / denoms[a]
                o_odd = o[(pairs + a) * BLOCK:(pairs + a + 1) * BLOCK] / denoms[pairs + a]
                ocols = slice((pairs * c + a) * LANES, (pairs * c + a + 1) * LANES)
                gated = jnp.where(low_o, o_even, o_odd) * g_ref[rows, ocols].astype(F32)
                y_scr[rows, ocols] = gated.astype(BF16)

    for n in range(D_MODEL // SUB_N):
        o_ref[:, _sub_cols(n)] = x_ref[:, _sub_cols(n)] + _dot(y_scr[...], w_ref[:, _sub_cols(n)])


def _odd_out(sinks, qg, kv, x2d, w_out, batch, seq):
    t = x2d.shape[0]
    tm = OUT_TM
    ns = seq // tm
    bpt = tm // BLOCK

    def row(b, i):
        return b * ns + i

    def outer(sink_ref, qg_hbm, kv_hbm, x_hbm, w_ref, o_hbm, y_scr):
        def tile_body(qg_ref, kv_ref, kvp_ref, x_ref, o_ref):
            _odd_out_kernel(sink_ref, qg_ref, kv_ref, kvp_ref, x_ref, w_ref, o_ref, y_scr)

        pltpu.emit_pipeline(
            tile_body,
            grid=(batch, ns),
            in_specs=[
                pl.BlockSpec((tm, 2 * ATTN_WIDTH), lambda b, i: (row(b, i), 0)),
                pl.BlockSpec((tm, 4 * KV_WIDTH), lambda b, i: (row(b, i), 0)),
                pl.BlockSpec((BLOCK, 4 * KV_WIDTH),
                             lambda b, i: (jnp.maximum(row(b, i) * bpt - 1, 0), 0)),
                pl.BlockSpec((tm, D_MODEL), lambda b, i: (row(b, i), 0)),
            ],
            out_specs=[pl.BlockSpec((tm, D_MODEL), lambda b, i: (row(b, i), 0))],
            trace_scopes=False,
        )(qg_hbm, kv_hbm, kv_hbm, x_hbm, o_hbm)

    streamed = pl.BlockSpec(memory_space=pl.ANY)
    return pl.pallas_call(
        outer,
        in_specs=[pl.BlockSpec(memory_space=pltpu.SMEM), streamed, streamed, streamed,
                  pl.BlockSpec(memory_space=pltpu.VMEM)],
        out_specs=streamed,
        out_shape=jax.ShapeDtypeStruct((t, D_MODEL), F32),
        scratch_shapes=[pltpu.VMEM((tm, ATTN_WIDTH), BF16)],
        compiler_params=pltpu.CompilerParams(vmem_limit_bytes=VMEM_LIMIT),
        name="odd_out",
    )(sinks, qg, kv, x2d, w_out)


def _rope_cos_sin(seq, dim):
    inv = 1.0 / (ROPE_THETA ** (np.arange(0, dim, 2, dtype=np.float64) / dim))
    ang = np.arange(seq, dtype=np.float64)[:, None] * inv[None, :]
    return np.cos(ang), np.sin(ang)


def _retention_tables():
    h = np.arange(RET_HEADS, dtype=np.float64)
    log_g = np.log(1.0 - 2.0 ** (-5.0 - h))
    idx = np.arange(RET_CHUNK, dtype=np.float64)
    diff = idx[:, None] - idx[None, :]
    scale = RET_HEAD_DIM ** -0.5
    intra = np.where(diff >= 0, np.exp(log_g[:, None, None] * np.maximum(diff, 0.0)), 0.0)
    q_dec = np.exp(log_g[:, None] * (idx[None, :] + 1.0))
    k_dec = np.exp(log_g[:, None] * (RET_CHUNK - 1.0 - idx[None, :]))
    c_dec = np.exp(log_g * RET_CHUNK)
    wide = (RET_HEADS, RET_CHUNK, RET_HEAD_DIM)
    dmat = jnp.asarray(intra * scale, F32)
    qdec = jnp.asarray(np.broadcast_to(q_dec[:, :, None], wide), F32)
    kdec = jnp.asarray(np.broadcast_to(k_dec[:, :, None] * scale, wide), F32)
    cdec = jnp.asarray(np.broadcast_to(c_dec[:, None, None], (RET_HEADS, 1, RET_HEAD_DIM)), F32)
    return dmat, qdec, kdec, cdec


def _head_rope_tables(seq):
    reps = LANES // ATTN_HEAD_DIM
    cos, sin = _rope_cos_sin(seq, ATTN_HEAD_DIM)
    cos_l = jnp.asarray(np.tile(np.concatenate([cos, cos], axis=1), (1, reps)), F32)
    sin_l = jnp.asarray(np.tile(np.concatenate([-sin, sin], axis=1), (1, reps)), F32)
    return cos_l, sin_l


def _head_gain_rows(norm_w, scale):
    half = ATTN_HEAD_DIM // 2
    reps = LANES // ATTN_HEAD_DIM
    w = norm_w.astype(F32) * scale
    w_partner = jnp.concatenate([w[half:], w[:half]])
    rows = jnp.stack([jnp.tile(w, reps), jnp.tile(w_partner, reps)])
    return jnp.pad(rows, ((0, SUBLANES - rows.shape[0]), (0, 0)))


def kernel(x, ev_norm_w, ev_w_in, ev_conv_w, ev_w_out, od_norm_w, od_w_in,
           od_q_norm_w, od_k_norm_w, od_sinks, od_w_out):
    batch, seq, d = x.shape
    assert d == D_MODEL and seq % IN_TM == 0 and seq % OUT_TM == 0
    x2d = x.reshape(batch * seq, d)

    cos, sin = _rope_cos_sin(seq, RET_HEAD_DIM)
    cw = jnp.pad(ev_conv_w[0], ((0, SUBLANES - CONV_K), (0, 0)))
    qkvg, conv, w_out_e, w_in_o, w_out_o = _even_in(
        x2d, ev_norm_w[0][None, :], ev_w_in[0].astype(BF16),
        jnp.asarray(cos, F32), jnp.asarray(sin, F32), cw,
        (ev_w_out[0], od_w_in[0], od_w_out[0]), seq)
    dmat, qdec, kdec, cdec = _retention_tables()
    x2d, h2d = _even_out(qkvg, conv, x2d, w_out_e, dmat, qdec, kdec, cdec,
                         od_norm_w[0][None, :], batch, seq)

    cos_l, sin_l = _head_rope_tables(seq)
    q_gain = _head_gain_rows(od_q_norm_w[0], ATTN_HEAD_DIM ** -0.5)
    k_gain = _head_gain_rows(od_k_norm_w[0], 1.0)
    head_id = np.arange(MXU_N) // ATTN_HEAD_DIM
    ones_bd = jnp.asarray(head_id[:, None] == head_id[None, :], BF16)
    qg, kv = _odd_in(h2d, w_in_o, cos_l, sin_l, q_gain, k_gain, ones_bd, seq)
    x2d = _odd_out(od_sinks[0].astype(F32), qg, kv, x2d, w_out_o, batch, seq)
    return x2d.reshape(batch, seq, d)
```

```python
import numpy as np
import jax
import jax.numpy as jnp
from jax import lax
from jax.experimental import pallas as pl
from jax.experimental.pallas import tpu as pltpu

D_MODEL = 2048
RET_HEADS = 4
RET_HEAD_DIM = 256
RET_WIDTH = RET_HEADS * RET_HEAD_DIM
CONV_WIDTH = D_MODEL - RET_WIDTH
CONV_K = 3
RET_CHUNK = 256
ATTN_HEAD_DIM = 64
ATTN_Q_HEADS = 32
ATTN_KV_HEADS = 4
ATTN_GROUP = ATTN_Q_HEADS // ATTN_KV_HEADS
ATTN_WIDTH = ATTN_Q_HEADS * ATTN_HEAD_DIM
KV_WIDTH = ATTN_KV_HEADS * ATTN_HEAD_DIM
BLOCK = 128
ROPE_THETA = 10000.0
EPS = 1e-6
NEG = -1e30

LANES = 128
SUBLANES = 8
BF16_ROWS = 16
MXU_N = 256
SUB_N = 512
N_SUB = 4
IN_TM = 1024
IN_TN = N_SUB * SUB_N
OUT_TM = 512
HALO = SUBLANES
V7X_VMEM_BYTES = 64 * 1024 * 1024
VMEM_LIMIT = V7X_VMEM_BYTES * 7 // 8

F32 = jnp.float32
BF16 = jnp.bfloat16


def _dot(a, b):
    return jnp.dot(a, b, preferred_element_type=F32)


def _dot_nt(a, b):
    return lax.dot_general(a, b, (((1,), (1,)), ((), ())), preferred_element_type=F32)


def _dot_tn(a, b):
    return lax.dot_general(a, b, (((0,), (0,)), ((), ())), preferred_element_type=F32)


def _silu(g):
    return (0.5 * g) * (1.0 + jnp.tanh(0.5 * g))


def _rms_rows(x, nw):
    ms = jnp.mean(x * x, axis=-1, keepdims=True)
    return x * lax.rsqrt(ms + EPS) * nw


def _sub_cols(s):
    return slice(s * SUB_N, (s + 1) * SUB_N)


def _even_in_kernel(x_ref, nw_ref, w0_ref, w1_ref, w2_ref, w3_ref,
                    cos_ref, sin_ref, cw_ref, c0_ref, c1_ref, c2_ref,
                    qkvg_ref, conv_ref, c0_out, c1_out, c2_out,
                    h_cur, h_next, cu_scr, tail_scr, *, seq_tiles):
    i = pl.program_id(0)
    j = pl.program_id(1)
    tm = h_cur.shape[0]
    chunk = x_ref.shape[0]
    w_refs = (w0_ref, w1_ref, w2_ref, w3_ref)

    for src, dst in ((c0_ref, c0_out), (c1_ref, c1_out), (c2_ref, c2_out)):
        dst[...] = src[...].astype(BF16)

    def norm_next(step=None):
        if step is None:
            rows = pl.ds(pl.multiple_of(j * chunk, chunk), chunk)
        else:
            rows = slice(step * chunk, (step + 1) * chunk)
        h_next[rows, :] = _rms_rows(x_ref[...], nw_ref[...]).astype(BF16)

    def sub_dot(s):
        return _dot(h_cur[...], w_refs[s][...])

    @pl.when(i == 0)
    def _():
        norm_next()
        tail_scr[...] = jnp.zeros_like(tail_scr)

    projecting = i >= 1

    def rope_then(tail_fn, step):
        if step == 0:
            for kb in range(D_MODEL // MXU_N):
                h_cur[:, kb * MXU_N:(kb + 1) * MXU_N] = h_next[:, kb * MXU_N:(kb + 1) * MXU_N]
        norm_next(step)
        cos = cos_ref[...]
        sin = sin_ref[...]
        for s in range(N_SUB // 2):
            acc = sub_dot(s)
            for hd in range(SUB_N // RET_HEAD_DIM):
                base = s * SUB_N + hd * RET_HEAD_DIM
                x1 = acc[:, hd * RET_HEAD_DIM:hd * RET_HEAD_DIM + LANES]
                x2 = acc[:, hd * RET_HEAD_DIM + LANES:(hd + 1) * RET_HEAD_DIM]
                qkvg_ref[:, base:base + LANES] = (x1 * cos - x2 * sin).astype(BF16)
                qkvg_ref[:, base + LANES:base + RET_HEAD_DIM] = (x2 * cos + x1 * sin).astype(BF16)
        for s in range(N_SUB // 2, N_SUB):
            qkvg_ref[:, _sub_cols(s)] = tail_fn(sub_dot(s)).astype(BF16)

    @pl.when(jnp.logical_and(projecting, j == 0))
    def _():
        rope_then(lambda v: v, 0)

    @pl.when(jnp.logical_and(projecting, j == 1))
    def _():
        rope_then(_silu, 1)

    @pl.when(jnp.logical_and(projecting, j >= 2))
    def _():
        cs = j - 2
        norm_next()
        cu = sub_dot(0) * sub_dot(1)
        at_seq_start = ((i - 1) % seq_tiles) == 0
        halo_rows = slice(HALO - (CONV_K - 1), HALO)
        cu_scr[halo_rows, :] = jnp.where(at_seq_start, 0.0, tail_scr[cs, halo_rows, :])
        cu_scr[HALO:HALO + tm, :] = cu
        tail_scr[cs, halo_rows, :] = cu_scr[HALO + tm - (CONV_K - 1):HALO + tm, :]
        cw = cw_ref[...]
        conv = cw[CONV_K - 1:CONV_K, :] * cu
        for t in range(CONV_K - 1):
            shift = CONV_K - 1 - t
            conv = conv + cw[t:t + 1, :] * cu_scr[HALO - shift:HALO - shift + tm, :]
        gated = sub_dot(2) * conv
        conv_ref[...] = (gated * _silu(sub_dot(3))).astype(BF16)


def _even_in(x2d, nw, w, cos, sin, cw, to_cast, seq):
    t = x2d.shape[0]
    tm = IN_TM
    n_tiles = t // tm
    seq_tiles = seq // tm
    n_ret_steps = 4 * RET_WIDTH // IN_TN
    n_conv_steps = CONV_WIDTH // SUB_N
    nj = n_ret_steps + n_conv_steps
    chunk = tm // nj
    n_proj_steps = n_tiles * nj
    conv_first = tuple((4 * RET_WIDTH + b * CONV_WIDTH) // SUB_N for b in (1, 2, 0, 3))

    def tile(i):
        return jnp.maximum(i - 1, 0)

    def step(i, j):
        return jnp.where(i == 0, 0, j)

    def w_spec(s):
        def index(i, j):
            jj = step(i, j)
            half = N_SUB // 2
            ret_idx = half * jnp.minimum(jj, n_ret_steps - 1) + (s % half) + (s // half) * N_SUB
            conv_idx = conv_first[s] + jj - n_ret_steps
            return 0, jnp.where(jj < n_ret_steps, ret_idx, conv_idx)
        return pl.BlockSpec((D_MODEL, SUB_N), index)

    def cast_spec(a):
        rows = a.shape[0] // n_proj_steps
        assert rows * n_proj_steps == a.shape[0] and rows % BF16_ROWS == 0
        return pl.BlockSpec((rows, a.shape[1]), lambda i, j: (tile(i) * nj + step(i, j), 0))

    def table_spec():
        return pl.BlockSpec((tm, LANES), lambda i, j: (tile(i) % seq_tiles, 0))

    n_cast = len(to_cast)
    in_specs = [
        pl.BlockSpec((chunk, D_MODEL), lambda i, j: (jnp.minimum(i, n_tiles - 1) * nj + j, 0)),
        w_spec(0), w_spec(1), w_spec(2), w_spec(3),
        table_spec(), table_spec(),
        pl.BlockSpec((SUBLANES, SUB_N), lambda i, j: (0, jnp.maximum(step(i, j) - n_ret_steps, 0))),
    ] + [cast_spec(a) for a in to_cast]
    out_specs = [
        pl.BlockSpec((tm, IN_TN), lambda i, j: (tile(i), jnp.minimum(step(i, j), n_ret_steps - 1))),
        pl.BlockSpec((tm, SUB_N), lambda i, j: (tile(i), jnp.maximum(step(i, j) - n_ret_steps, 0))),
    ] + [cast_spec(a) for a in to_cast]

    def outer(*refs):
        x_hbm, nw_ref, w_hbm, cos_hbm, sin_hbm, cw_hbm = refs[:6]
        cast_hbm = refs[6:6 + n_cast]
        outs_hbm = refs[6 + n_cast:8 + 2 * n_cast]
        scratch = refs[8 + 2 * n_cast:]

        def step_body(x_ref, w0, w1, w2, w3, cos_ref, sin_ref, cw_ref, *rest):
            casts_in = rest[:n_cast]
            outs = rest[n_cast:]
            _even_in_kernel(x_ref, nw_ref, w0, w1, w2, w3, cos_ref, sin_ref, cw_ref, *casts_in,
                            *outs, *scratch, seq_tiles=seq_tiles)

        pltpu.emit_pipeline(
            step_body, grid=(n_tiles + 1, nj), in_specs=in_specs, out_specs=out_specs,
            trace_scopes=False,
        )(x_hbm, w_hbm, w_hbm, w_hbm, w_hbm, cos_hbm, sin_hbm, cw_hbm, *cast_hbm, *outs_hbm)

    streamed = pl.BlockSpec(memory_space=pl.ANY)
    return pl.pallas_call(
        outer,
        in_specs=[streamed, pl.BlockSpec(memory_space=pltpu.VMEM)] + [streamed] * (4 + n_cast),
        out_specs=[streamed] * (2 + n_cast),
        out_shape=[
            jax.ShapeDtypeStruct((t, 4 * RET_WIDTH), BF16),
            jax.ShapeDtypeStruct((t, CONV_WIDTH), BF16),
        ] + [jax.ShapeDtypeStruct(a.shape, BF16) for a in to_cast],
        scratch_shapes=[
            pltpu.VMEM((tm, D_MODEL), BF16),
            pltpu.VMEM((tm, D_MODEL), BF16),
            pltpu.VMEM((HALO + tm, SUB_N), F32),
            pltpu.VMEM((n_conv_steps, HALO, SUB_N), F32),
        ],
        compiler_params=pltpu.CompilerParams(vmem_limit_bytes=VMEM_LIMIT),
        name="even_in",
    )(x2d, nw, w, cos, sin, cw, *to_cast)


def _even_out_kernel(qkvg_ref, conv_ref, x_ref, w_ref,
                     dmat_ref, qdec_ref, kdec_ref, cdec_ref, nw_next_ref,
                     o_ref, h_next_ref, state_scr, y_scr):
    tm = x_ref.shape[0]
    q_ref, v_ref, k_ref, g_ref = (
        qkvg_ref.at[:, n * RET_WIDTH:(n + 1) * RET_WIDTH] for n in range(4))

    @pl.when(pl.program_id(1) == 0)
    def _():
        state_scr[...] = jnp.zeros_like(state_scr)

    for c in range(tm // RET_CHUNK):
        rows = slice(c * RET_CHUNK, (c + 1) * RET_CHUNK)
        for h in range(RET_HEADS):
            cols = slice(h * RET_HEAD_DIM, (h + 1) * RET_HEAD_DIM)
            qh = q_ref[rows, cols]
            kh = k_ref[rows, cols]
            vh = v_ref[rows, cols]
            scores = _dot_nt(qh, kh) * dmat_ref[h]
            inner = _dot(scores.astype(BF16), vh)
            state = state_scr[h]
            cross = _dot(qh, state.astype(BF16)) * qdec_ref[h]
            o = inner + cross
            kd = (kh.astype(F32) * kdec_ref[h]).astype(BF16)
            state_scr[h] = cdec_ref[h] * state + _dot_tn(kd, vh)
            ms = jnp.mean(o * o, axis=-1, keepdims=True)
            ro = o * lax.rsqrt(ms + EPS) * g_ref[rows, cols].astype(F32)
            y_scr[rows, cols] = ro.astype(BF16)
    y_scr[:, RET_WIDTH:] = conv_ref[...]

    ssq = jnp.zeros((tm, 1), F32)
    for n in range(D_MODEL // SUB_N):
        xn = x_ref[:, _sub_cols(n)] + _dot(y_scr[...], w_ref[:, _sub_cols(n)])
        o_ref[:, _sub_cols(n)] = xn
        ssq = ssq + jnp.sum(xn * xn, axis=-1, keepdims=True)
    r = lax.rsqrt(ssq * (1.0 / D_MODEL) + EPS)
    for n in range(D_MODEL // SUB_N):
        h_next_ref[:, _sub_cols(n)] = (o_ref[:, _sub_cols(n)] * r * nw_next_ref[:, _sub_cols(n)]).astype(BF16)


def _even_out(qkvg, conv, x2d, w_out, dmat, qdec, kdec, cdec, nw_next, batch, seq):
    t = x2d.shape[0]
    tm = OUT_TM
    ns = seq // tm

    def row(b, i):
        return b * ns + i

    def tile_spec(width, buffers=2):
        return pl.BlockSpec((tm, width), lambda b, i: (row(b, i), 0),
                            pipeline_mode=pl.Buffered(buffers))

    def outer(qkvg_hbm, conv_hbm, x_hbm, w_ref, dmat_ref, qdec_ref, kdec_ref, cdec_ref, nw_ref,
              o_hbm, h_hbm, state_scr, y_scr):
        def tile_body(qkvg_ref, conv_ref, x_ref, o_ref, h_ref):
            _even_out_kernel(qkvg_ref, conv_ref, x_ref, w_ref, dmat_ref, qdec_ref, kdec_ref,
                             cdec_ref, nw_ref, o_ref, h_ref, state_scr, y_scr)

        pltpu.emit_pipeline(
            tile_body,
            grid=(batch, ns),
            in_specs=[tile_spec(4 * RET_WIDTH, 3), tile_spec(CONV_WIDTH, 3), tile_spec(D_MODEL, 3)],
            out_specs=[tile_spec(D_MODEL), tile_spec(D_MODEL)],
            trace_scopes=False,
        )(qkvg_hbm, conv_hbm, x_hbm, o_hbm, h_hbm)

    streamed = pl.BlockSpec(memory_space=pl.ANY)
    resident = pl.BlockSpec(memory_space=pltpu.VMEM)
    return pl.pallas_call(
        outer,
        in_specs=[streamed, streamed, streamed] + [resident] * 6,
        out_specs=[streamed, streamed],
        out_shape=[
            jax.ShapeDtypeStruct((t, D_MODEL), F32),
            jax.ShapeDtypeStruct((t, D_MODEL), BF16),
        ],
        scratch_shapes=[
            pltpu.VMEM((RET_HEADS, RET_HEAD_DIM, RET_HEAD_DIM), F32),
            pltpu.VMEM((tm, D_MODEL), BF16),
        ],
        compiler_params=pltpu.CompilerParams(vmem_limit_bytes=VMEM_LIMIT),
        name="even_out",
    )(qkvg, conv, x2d, w_out, dmat, qdec, kdec, cdec, nw_next)


def _odd_in_kernel(h_ref, w_ref,
                   cos_ref, sin_ref, qgain_ref, kgain_ref, ones_ref,
                   qg_ref, kv_ref):
    q_blocks = ATTN_WIDTH // SUB_N
    kv_block = q_blocks
    gate_block0 = kv_block + 1

    def sub_dot(block):
        return _dot(h_ref[...], w_ref[:, block * SUB_N:(block + 1) * SUB_N])

    def dup_heads(x):
        lane = lax.broadcasted_iota(jnp.int32, (x.shape[0], LANES), 1)
        low = lane < ATTN_HEAD_DIM
        blocks = []
        for b in range(x.shape[1] // LANES):
            xb = x[:, b * LANES:(b + 1) * LANES]
            swapped = pltpu.roll(xb, ATTN_HEAD_DIM, axis=1)
            blocks += [jnp.where(low, xb, swapped), jnp.where(low, swapped, xb)]
        return jnp.concatenate(blocks, axis=1)

    def norm_rope(acc, gain_ref):
        width = acc.shape[1]
        sq = (acc * acc).astype(BF16)
        ss = jnp.concatenate(
            [_dot(sq[:, c * MXU_N:(c + 1) * MXU_N], ones_ref[...]) for c in range(width // MXU_N)],
            axis=1)
        r = lax.rsqrt(ss * (1.0 / ATTN_HEAD_DIM) + EPS)
        half = ATTN_HEAD_DIM // 2
        lane = lax.broadcasted_iota(jnp.int32, acc.shape, 1)
        first_half = (lane % ATTN_HEAD_DIM) < half
        partner = jnp.where(first_half,
                            pltpu.roll(acc, width - half, axis=1),
                            pltpu.roll(acc, half, axis=1))
        reps = width // LANES
        ta = cos_ref[...] * gain_ref[0:1, :]
        tb = sin_ref[...] * gain_ref[1:2, :]
        ta_w = jnp.concatenate([ta] * reps, axis=1)
        tb_w = jnp.concatenate([tb] * reps, axis=1)
        return r * (acc * ta_w + partner * tb_w)

    def kv_part():
        acc = sub_dot(kv_block)
        k = norm_rope(acc[:, :KV_WIDTH], kgain_ref)
        kv_ref[:, :2 * KV_WIDTH] = dup_heads(k).astype(BF16)
        kv_ref[:, 2 * KV_WIDTH:] = dup_heads(acc[:, KV_WIDTH:]).astype(BF16)

    kv_part()
    for s in range(q_blocks):
        qg_ref[:, _sub_cols(s)] = norm_rope(sub_dot(s), qgain_ref).astype(BF16)
    for s in range(q_blocks):
        qg_ref[:, _sub_cols(q_blocks + s)] = _silu(sub_dot(gate_block0 + s)).astype(BF16)


def _odd_in(h2d, w, cos_l, sin_l, q_gain, k_gain, ones_bd, seq):
    t = h2d.shape[0]
    tm = IN_TM
    seq_tiles = seq // tm
    assert 2 * KV_WIDTH == SUB_N

    def tab_spec():
        return pl.BlockSpec((tm, LANES), lambda i: (i % seq_tiles, 0))

    def outer(h_hbm, w_ref, cos_hbm, sin_hbm, qgain_ref, kgain_ref, ones_ref, qg_hbm, kv_hbm):
        def tile_body(h_ref, cos_ref, sin_ref, qg_ref, kv_ref):
            _odd_in_kernel(h_ref, w_ref, cos_ref, sin_ref, qgain_ref, kgain_ref, ones_ref,
                           qg_ref, kv_ref)

        pltpu.emit_pipeline(
            tile_body,
            grid=(t // tm,),
            in_specs=[pl.BlockSpec((tm, D_MODEL), lambda i: (i, 0)), tab_spec(), tab_spec()],
            out_specs=[
                pl.BlockSpec((tm, 2 * ATTN_WIDTH), lambda i: (i, 0)),
                pl.BlockSpec((tm, 4 * KV_WIDTH), lambda i: (i, 0)),
            ],
            trace_scopes=False,
        )(h_hbm, cos_hbm, sin_hbm, qg_hbm, kv_hbm)

    streamed = pl.BlockSpec(memory_space=pl.ANY)
    resident = pl.BlockSpec(memory_space=pltpu.VMEM)
    return pl.pallas_call(
        outer,
        in_specs=[streamed, resident, streamed, streamed, resident, resident, resident],
        out_specs=[streamed, streamed],
        out_shape=[
            jax.ShapeDtypeStruct((t, 2 * ATTN_WIDTH), BF16),
            jax.ShapeDtypeStruct((t, 4 * KV_WIDTH), BF16),
        ],
        compiler_params=pltpu.CompilerParams(vmem_limit_bytes=VMEM_LIMIT),
        name="odd_in",
    )(h2d, w, cos_l, sin_l, q_gain, k_gain, ones_bd)


def _odd_out_kernel(sink_ref, qg_ref, kv_ref, kvp_ref, x_ref, w_ref,
                    o_ref, y_scr):
    tm = x_ref.shape[0]
    q_ref = qg_ref.at[:, :ATTN_WIDTH]
    g_ref = qg_ref.at[:, ATTN_WIDTH:]
    at_seq_start = pl.program_id(1) == 0
    pairs = ATTN_GROUP // 2
    half_w = 2 * KV_WIDTH

    lane_kv = lax.broadcasted_iota(jnp.int32, (2 * BLOCK, LANES), 1)
    low_kv = lane_kv < ATTN_HEAD_DIM
    rr = lax.broadcasted_iota(jnp.int32, (BLOCK, BLOCK), 0)
    cc = lax.broadcasted_iota(jnp.int32, (BLOCK, BLOCK), 1)
    tri = cc <= rr
    low_o = cc < ATTN_HEAD_DIM

    for bl in range(tm // BLOCK):
        rows = slice(bl * BLOCK, (bl + 1) * BLOCK)
        for c in range(ATTN_KV_HEADS):
            kcols = slice(c * LANES, (c + 1) * LANES)
            vcols = slice(half_w + c * LANES, half_w + (c + 1) * LANES)
            if bl == 0:
                k_prev = kvp_ref[:, kcols]
                v_prev = kvp_ref[:, vcols]
            else:
                prows = slice((bl - 1) * BLOCK, bl * BLOCK)
                k_prev = kv_ref[prows, kcols]
                v_prev = kv_ref[prows, vcols]
            kk = jnp.concatenate([k_prev, kv_ref[rows, kcols]], axis=0)
            vv = jnp.concatenate([v_prev, kv_ref[rows, vcols]], axis=0)
            zero = jnp.zeros_like(kk)
            k_par = (jnp.where(low_kv, kk, zero), jnp.where(low_kv, zero, kk))
            qs = jnp.concatenate(
                [q_ref[rows, (pairs * c + a) * LANES:(pairs * c + a + 1) * LANES]
                 for a in range(pairs)], axis=0)
            probs = []
            denoms = []
            for par in range(2):
                s = _dot_nt(qs, k_par[par])
                for a in range(pairs):
                    sa = s[a * BLOCK:(a + 1) * BLOCK]
                    s_prev = sa[:, :BLOCK]
                    if bl == 0:
                        s_prev = jnp.where(at_seq_start, NEG, s_prev)
                    f = jnp.where(tri, sa[:, BLOCK:], s_prev)
                    sink = sink_ref[ATTN_GROUP * c + 2 * a + par]
                    m = jnp.maximum(jnp.max(f, axis=1, keepdims=True), sink)
                    p = jnp.exp(f - m)
                    denoms.append(jnp.sum(p, axis=1, keepdims=True) + jnp.exp(sink - m))
                    probs.append(jnp.concatenate(
                        [jnp.where(tri, 0.0, p), jnp.where(tri, p, 0.0)], axis=1).astype(BF16))
            o = _dot(jnp.concatenate(probs, axis=0), vv)
            for a in range(pairs):
                o_even = o[a * BLOCK:(a + 1) * BLOCK] / denoms[a]
                o_odd = o[(pairs + a) * BLOCK:(pairs + a + 1) * BLOCK] / denoms[pairs + a]
                ocols = slice((pairs * c + a) * LANES, (pairs * c + a + 1) * LANES)
                gated = jnp.where(low_o, o_even, o_odd) * g_ref[rows, ocols].astype(F32)
                y_scr[rows, ocols] = gated.astype(BF16)

    for n in range(D_MODEL // SUB_N):
        o_ref[:, _sub_cols(n)] = x_ref[:, _sub_cols(n)] + _dot(y_scr[...], w_ref[:, _sub_cols(n)])


def _odd_out(sinks, qg, kv, x2d, w_out, batch, seq):
    t = x2d.shape[0]
    tm = OUT_TM
    ns = seq // tm
    bpt = tm // BLOCK

    def row(b, i):
        return b * ns + i

    def outer(sink_ref, qg_hbm, kv_hbm, x_hbm, w_ref, o_hbm, y_scr):
        def tile_body(qg_ref, kv_ref, kvp_ref, x_ref, o_ref):
            _odd_out_kernel(sink_ref, qg_ref, kv_ref, kvp_ref, x_ref, w_ref, o_ref, y_scr)

        pltpu.emit_pipeline(
            tile_body,
            grid=(batch, ns),
            in_specs=[
                pl.BlockSpec((tm, 2 * ATTN_WIDTH), lambda b, i: (row(b, i), 0)),
                pl.BlockSpec((tm, 4 * KV_WIDTH), lambda b, i: (row(b, i), 0)),
                pl.BlockSpec((BLOCK, 4 * KV_WIDTH),
                             lambda b, i: (jnp.maximum(row(b, i) * bpt - 1, 0), 0)),
                pl.BlockSpec((tm, D_MODEL), lambda b, i: (row(b, i), 0)),
            ],
            out_specs=[pl.BlockSpec((tm, D_MODEL), lambda b, i: (row(b, i), 0))],
            trace_scopes=False,
        )(qg_hbm, kv_hbm, kv_hbm, x_hbm, o_hbm)

    streamed = pl.BlockSpec(memory_space=pl.ANY)
    return pl.pallas_call(
        outer,
        in_specs=[pl.BlockSpec(memory_space=pltpu.SMEM), streamed, streamed, streamed,
                  pl.BlockSpec(memory_space=pltpu.VMEM)],
        out_specs=streamed,
        out_shape=jax.ShapeDtypeStruct((t, D_MODEL), F32),
        scratch_shapes=[pltpu.VMEM((tm, ATTN_WIDTH), BF16)],
        compiler_params=pltpu.CompilerParams(vmem_limit_bytes=VMEM_LIMIT),
        name="odd_out",
    )(sinks, qg, kv, x2d, w_out)


def _rope_cos_sin(seq, dim):
    inv = 1.0 / (ROPE_THETA ** (np.arange(0, dim, 2, dtype=np.float64) / dim))
    ang = np.arange(seq, dtype=np.float64)[:, None] * inv[None, :]
    return np.cos(ang), np.sin(ang)


def _retention_tables():
    h = np.arange(RET_HEADS, dtype=np.float64)
    log_g = np.log(1.0 - 2.0 ** (-5.0 - h))
    idx = np.arange(RET_CHUNK, dtype=np.float64)
    diff = idx[:, None] - idx[None, :]
    scale = RET_HEAD_DIM ** -0.5
    intra = np.where(diff >= 0, np.exp(log_g[:, None, None] * np.maximum(diff, 0.0)), 0.0)
    q_dec = np.exp(log_g[:, None] * (idx[None, :] + 1.0))
    k_dec = np.exp(log_g[:, None] * (RET_CHUNK - 1.0 - idx[None, :]))
    c_dec = np.exp(log_g * RET_CHUNK)
    wide = (RET_HEADS, RET_CHUNK, RET_HEAD_DIM)
    dmat = jnp.asarray(intra * scale, F32)
    qdec = jnp.asarray(np.broadcast_to(q_dec[:, :, None], wide), F32)
    kdec = jnp.asarray(np.broadcast_to(k_dec[:, :, None] * scale, wide), F32)
    cdec = jnp.asarray(np.broadcast_to(c_dec[:, None, None], (RET_HEADS, 1, RET_HEAD_DIM)), F32)
    return dmat, qdec, kdec, cdec


def _head_rope_tables(seq):
    reps = LANES // ATTN_HEAD_DIM
    cos, sin = _rope_cos_sin(seq, ATTN_HEAD_DIM)
    cos_l = jnp.asarray(np.tile(np.concatenate([cos, cos], axis=1), (1, reps)), F32)
    sin_l = jnp.asarray(np.tile(np.concatenate([-sin, sin], axis=1), (1, reps)), F32)
    return cos_l, sin_l


def _head_gain_rows(norm_w, scale):
    half = ATTN_HEAD_DIM // 2
    reps = LANES // ATTN_HEAD_DIM
    w = norm_w.astype(F32) * scale
    w_partner = jnp.concatenate([w[half:], w[:half]])
    rows = jnp.stack([jnp.tile(w, reps), jnp.tile(w_partner, reps)])
    return jnp.pad(rows, ((0, SUBLANES - rows.shape[0]), (0, 0)))


def kernel(x, ev_norm_w, ev_w_in, ev_conv_w, ev_w_out, od_norm_w, od_w_in,
           od_q_norm_w, od_k_norm_w, od_sinks, od_w_out):
    batch, seq, d = x.shape
    assert d == D_MODEL and seq % IN_TM == 0 and seq % OUT_TM == 0
    x2d = x.reshape(batch * seq, d)

    cos, sin = _rope_cos_sin(seq, RET_HEAD_DIM)
    cw = jnp.pad(ev_conv_w[0], ((0, SUBLANES - CONV_K), (0, 0)))
    qkvg, conv, w_out_e, w_in_o, w_out_o = _even_in(
        x2d, ev_norm_w[0][None, :], ev_w_in[0].astype(BF16),
        jnp.asarray(cos, F32), jnp.asarray(sin, F32), cw,
        (ev_w_out[0], od_w_in[0], od_w_out[0]), seq)
    dmat, qdec, kdec, cdec = _retention_tables()
    x2d, h2d = _even_out(qkvg, conv, x2d, w_out_e, dmat, qdec, kdec, cdec,
                         od_norm_w[0][None, :], batch, seq)

    cos_l, sin_l = _head_rope_tables(seq)
    q_gain = _head_gain_rows(od_q_norm_w[0], ATTN_HEAD_DIM ** -0.5)
    k_gain = _head_gain_rows(od_k_norm_w[0], 1.0)
    head_id = np.arange(MXU_N) // ATTN_HEAD_DIM
    ones_bd = jnp.asarray(head_id[:, None] == head_id[None, :], BF16)
    qg, kv = _odd_in(h2d, w_in_o, cos_l, sin_l, q_gain, k_gain, ones_bd, seq)
    x2d = _odd_out(od_sinks[0].astype(F32), qg, kv, x2d, w_out_o, batch, seq)
    return x2d.reshape(batch, seq, d)
```
